```python
import math
import jax
import jax.numpy as jnp
from jax import lax
import numpy as np

D_MODEL = 1024
BATCH = 16
SEQ = 2048
DEPTH = 4

GRID_W = 64
CTX_LEN = 256
HEAD_DIM = 64
ROPE_FREQS = HEAD_DIM // 4
ROPE_THETA = 10000.0
ATTN_SCALE = HEAD_DIM ** -0.5
NEG_INF = -1e30
Q_BLOCK = 128
WINDOW = 128
A_HEADS = 8
A_KV = 2
W_HEADS = 8
W_KV = 2
HY_WIDTH = 512
HY_SHORT = 3
HY_BANDS = 16
HY_EMB = 1 + 2 * HY_BANDS
HY_HIDDEN = 64
HY_TARGET = 1e-2
HY_FAST_DECAY = 0.3
HY_SLOW_DECAY = 1.5
N_BRANCH = 3
N_GROUPS = 4
EXPERTS_PER_GROUP = 4
N_EXPERTS = N_GROUPS * EXPERTS_PER_GROUP
TOP_K = 2
D_EXPERT = 512
N_MOD = 6
EPS = 1e-6
IN_SPLITS = (A_HEADS * HEAD_DIM, A_KV * HEAD_DIM, A_KV * HEAD_DIM, 3 * HY_WIDTH,
             W_HEADS * HEAD_DIM, W_KV * HEAD_DIM, W_KV * HEAD_DIM, N_BRANCH * D_MODEL)
IN_OFFSETS = tuple(int(o) for o in np.cumsum(IN_SPLITS)[:-1])
D_IN = int(sum(IN_SPLITS))
F32 = jnp.float32

kernel_name = 'hybrid_flow_trunk'


def rms_norm(t, g):
    tf = t.astype(F32)
    return (tf * lax.rsqrt(jnp.mean(tf * tf, axis=-1, keepdims=True) + EPS) * g.astype(F32)).astype(t.dtype)


def modulate(h, shift, scale):
    return h * (1 + scale) + shift


def heads(t, n):
    return t.reshape(t.shape[:-1] + (n, HEAD_DIM))


def group_q(q, n_kv):
    b, l, h, d = q.shape
    return q.reshape(b, l, n_kv, h // n_kv, d)


def axial_rope(t, cos, sin):
    tf = t.reshape(t.shape[:-1] + (2, 2, ROPE_FREQS))
    a, b = tf[..., 0, :], tf[..., 1, :]
    c, s = cos[:, None], sin[:, None]
    out = jnp.stack([a * c - b * s, b * c + a * s], axis=-2)
    return out.reshape(t.shape).astype(t.dtype)


def gqa_attend(q, k, v, sink=None, mask=None):
    s = jnp.einsum('bqkgd,bskd->bkgqs', q, k).astype(F32) * ATTN_SCALE
    if mask is not None:
        s = jnp.where(mask, s, NEG_INF)
    if sink is not None:
        sk = jnp.broadcast_to(sink.astype(F32)[:, :, None, None], s.shape[:-1] + (1,))
        p = jax.nn.softmax(jnp.concatenate([s, sk], axis=-1), axis=-1)[..., :-1]
    else:
        p = jax.nn.softmax(s, axis=-1)
    return jnp.einsum('bkgqs,bskd->bqkgd', p.astype(v.dtype), v)


def to_blocks(q, n_kv):
    b, s, h, d = q.shape
    return jnp.moveaxis(q.reshape(b, s // Q_BLOCK, Q_BLOCK, n_kv, h // n_kv, d), 1, 0)


def from_blocks(o):
    nb, b, qb, kv, g, d = o.shape
    return jnp.moveaxis(o, 0, 1).reshape(b, nb * qb, kv * g * d)


def band_blocks(t):
    b, s, kv, d = t.shape
    tp = jnp.pad(t, ((0, 0), (Q_BLOCK, Q_BLOCK), (0, 0), (0, 0))).reshape(b, s // Q_BLOCK + 2, Q_BLOCK, kv, d)
    band = jnp.concatenate([tp[:, :-2], tp[:, 1:-1], tp[:, 2:]], axis=2)
    return jnp.moveaxis(band, 1, 0)


def hyena_kernel(n, w1, b1, freq, w2, b2, w3):
    pos = jnp.arange(n, dtype=F32)[:, None]
    t = pos / (n - 1)
    bands = jnp.linspace(1e-4, HY_BANDS - 1, HY_BANDS, dtype=F32)
    ang = 2 * math.pi * bands * pos / n
    z = jnp.concatenate([t, jnp.cos(ang), -jnp.sin(ang)], axis=-1)
    h = jnp.sin(freq * (z @ w1 + b1))
    h = jnp.sin(freq * (h @ w2 + b2))
    h = (h @ w3).astype(F32).reshape(n, 2, HY_WIDTH)
    deltas = jnp.abs(jnp.linspace(math.log(HY_TARGET) / HY_SLOW_DECAY, math.log(HY_TARGET) / HY_FAST_DECAY,
                                  HY_WIDTH, dtype=F32))
    h = h * jnp.exp(-t * deltas)[:, None, :]
    kern = jnp.concatenate([h[:, 0], jnp.zeros((1, HY_WIDTH), F32), h[:0:-1, 1]], axis=0)
    return kern / jnp.sum(jnp.abs(kern), axis=0, keepdims=True)


def bidir_long_conv(u, kern):
    n = u.shape[1]
    uf = jnp.fft.rfft(u.astype(F32), n=2 * n, axis=1)
    kf = jnp.fft.rfft(kern, n=2 * n, axis=0)
    return jnp.fft.irfft(uf * kf[None], n=2 * n, axis=1)[:, :n]


def hyena_mixer(u, short_w, short_b, kern, skip):
    uc = lax.conv_general_dilated(u, short_w[:, None, :], (1,), ((HY_SHORT // 2, HY_SHORT // 2),),
                                  dimension_numbers=('NWC', 'WIO', 'NWC'),
                                  feature_group_count=u.shape[-1]) + short_b
    x0, x1, v = jnp.split(uc.astype(F32), 3, axis=-1)
    v = v * x1
    y = bidir_long_conv(v, kern) + skip.astype(F32) * v
    return (x0 * y).astype(u.dtype)


def merge_branches(ya, yh, yw, gate_logits, w_branch, w_out):
    ga, gh, gw = jnp.split(jax.nn.sigmoid(gate_logits), N_BRANCH, axis=-1)
    m = ga * (ya @ w_branch[0]) + gh * (yh @ w_branch[1]) + gw * (yw @ w_branch[2])
    return m @ w_out


def hier_moe(t, rg_w, rg_b, re_w, re_b, w1, w3, w2):
    n = t.shape[0]
    lg = (t @ rg_w + rg_b).astype(F32)
    g_sel = jnp.argmax(lg, axis=-1)
    p_g = jnp.take_along_axis(jax.nn.softmax(lg, axis=-1), g_sel[:, None], axis=-1)
    le = (t @ re_w + re_b).astype(F32).reshape(n, N_GROUPS, EXPERTS_PER_GROUP)
    le_sel = jnp.take_along_axis(le, g_sel[:, None, None], axis=1)[:, 0]
    top_v, top_i = lax.top_k(le_sel, TOP_K)
    p_e = jax.nn.softmax(top_v, axis=-1) * p_g
    eid = g_sel[:, None] * EXPERTS_PER_GROUP + top_i
    combine = jnp.sum(jax.nn.one_hot(eid, N_EXPERTS, dtype=F32) * p_e[..., None], axis=1).astype(t.dtype)
    out = jnp.zeros_like(t)
    for e in range(N_EXPERTS):
        h = jax.nn.silu(t @ w1[e]) * (t @ w3[e])
        out = out + combine[:, e:e + 1] * (h @ w2[e])
    return out


def trunk_layer(x, xc, c, c_ctx, cos, sin, win_mask, update_ctx,
                ada_w, ada_b, norm1_g, norm2_g, w_in, qk_g,
                hy_short_w, hy_short_b, hy_f_w1, hy_f_b1, hy_f_freq, hy_f_w2, hy_f_b2, hy_f_w3, hy_bias,
                sinks, w_branch, w_out, router_g_w, router_g_b, router_e_w, router_e_b, exp_w1, exp_w3, exp_w2):
    bsz, n_tok, _ = x.shape
    n_ctx = xc.shape[1]
    mod = jnp.split((jax.nn.silu(c) @ ada_w + ada_b)[:, None, :], N_MOD, axis=-1)
    mod_c = jnp.split(jax.nn.silu(c_ctx) @ ada_w + ada_b, N_MOD, axis=-1)

    h = modulate(rms_norm(x, norm1_g), mod[0], mod[1])
    hc = modulate(rms_norm(xc, norm1_g), mod_c[0], mod_c[1])
    qa, ka, va, uh, qw, kw, vw, gts = jnp.split(h @ w_in, IN_OFFSETS, axis=-1)
    qa_c, ka_c, va_c, uh_c, qw_c, kw_c, vw_c, gts_c = jnp.split(hc @ w_in, IN_OFFSETS, axis=-1)

    kac = rms_norm(heads(ka_c, A_KV), qk_g[1])
    vac = heads(va_c, A_KV)
    k_all = jnp.concatenate([kac, axial_rope(rms_norm(heads(ka, A_KV), qk_g[1]), cos, sin)], axis=1)
    v_all = jnp.concatenate([vac, heads(va, A_KV)], axis=1)
    qa_l = axial_rope(rms_norm(heads(qa, A_HEADS), qk_g[0]), cos, sin)
    ya = from_blocks(lax.map(lambda qb: gqa_attend(qb, k_all, v_all), to_blocks(qa_l, A_KV)))

    sink = sinks.reshape(W_KV, W_HEADS // W_KV)
    kwc = rms_norm(heads(kw_c, W_KV), qk_g[3])
    vwc = heads(vw_c, W_KV)
    kw_l = axial_rope(rms_norm(heads(kw, W_KV), qk_g[3]), cos, sin)
    qw_l = axial_rope(rms_norm(heads(qw, W_HEADS), qk_g[2]), cos, sin)

    def window_block(args):
        qb, kb, vb, mb = args
        return gqa_attend(qb, jnp.concatenate([kb, kwc], axis=1), jnp.concatenate([vb, vwc], axis=1), sink, mb)

    yw = from_blocks(lax.map(window_block, (to_blocks(qw_l, W_KV), band_blocks(kw_l),
                                            band_blocks(heads(vw, W_KV)), win_mask)))

    kern = hyena_kernel(n_tok, hy_f_w1, hy_f_b1, hy_f_freq, hy_f_w2, hy_f_b2, hy_f_w3)
    yh = hyena_mixer(uh, hy_short_w, hy_short_b, kern, hy_bias)
    x = x + mod[2] * merge_branches(ya, yh, yw, gts, w_branch, w_out)

    if update_ctx:
        ya_c = gqa_attend(group_q(rms_norm(heads(qa_c, A_HEADS), qk_g[0]), A_KV), kac, vac).reshape(bsz, n_ctx, -1)
        yw_c = gqa_attend(group_q(rms_norm(heads(qw_c, W_HEADS), qk_g[2]), W_KV), kwc, vwc, sink).reshape(bsz, n_ctx, -1)
        kern_c = hyena_kernel(n_ctx, hy_f_w1, hy_f_b1, hy_f_freq, hy_f_w2, hy_f_b2, hy_f_w3)
        yh_c = hyena_mixer(uh_c, hy_short_w, hy_short_b, kern_c, hy_bias)
        xc = xc + mod_c[2] * merge_branches(ya_c, yh_c, yw_c, gts_c, w_branch, w_out)

    tokens = modulate(rms_norm(x, norm2_g), mod[3], mod[4]).reshape(-1, D_MODEL)
    if update_ctx:
        tokens_c = modulate(rms_norm(xc, norm2_g), mod_c[3], mod_c[4]).reshape(-1, D_MODEL)
        tokens = jnp.concatenate([tokens, tokens_c], axis=0)
    f = hier_moe(tokens, router_g_w, router_g_b, router_e_w, router_e_b, exp_w1, exp_w3, exp_w2)
    n_lat = bsz * n_tok
    x = x + mod[5] * f[:n_lat].reshape(x.shape)
    if update_ctx:
        xc = xc + mod_c[5] * f[n_lat:].reshape(xc.shape)
    return x, xc


def setup_inputs(seed: int = 0) -> dict:
    key = jax.random.key(seed)
    ks = jax.random.split(key, 32)
    L, D = DEPTH, D_MODEL

    def nrm(k, shape, scale):
        return jax.random.normal(k, shape, F32) * scale

    return {
        'x': nrm(ks[0], (BATCH, SEQ, D), 1.0),
        'c': nrm(ks[1], (BATCH, D), 1.0),
        'ctx': nrm(ks[2], (BATCH, CTX_LEN, D), 1.0),
        'c_ctx': nrm(ks[3], (D,), 1.0),
        'ada_w': nrm(ks[4], (L, D, N_MOD * D), 0.5 * D ** -0.5),
        'ada_b': nrm(ks[5], (L, N_MOD * D), 0.02),
        'norm1_g': 1.0 + nrm(ks[6], (L, D), 0.05),
        'norm2_g': 1.0 + nrm(ks[7], (L, D), 0.05),
        'w_in': nrm(ks[8], (L, D, D_IN), D ** -0.5),
        'qk_g': 1.0 + nrm(ks[9], (L, 4, HEAD_DIM), 0.05),
        'hy_short_w': nrm(ks[10], (L, HY_SHORT, 3 * HY_WIDTH), HY_SHORT ** -0.5),
        'hy_short_b': nrm(ks[11], (L, 3 * HY_WIDTH), 0.02),
        'hy_f_w1': nrm(ks[12], (L, HY_EMB, HY_HIDDEN), HY_EMB ** -0.5),
        'hy_f_b1': nrm(ks[13], (L, HY_HIDDEN), 0.1),
        'hy_f_freq': 1.0 + nrm(ks[14], (L, HY_HIDDEN), 0.05),
        'hy_f_w2': nrm(ks[15], (L, HY_HIDDEN, HY_HIDDEN), HY_HIDDEN ** -0.5),
        'hy_f_b2': nrm(ks[16], (L, HY_HIDDEN), 0.1),
        'hy_f_w3': nrm(ks[17], (L, HY_HIDDEN, 2 * HY_WIDTH), HY_HIDDEN ** -0.5),
        'hy_bias': nrm(ks[18], (L, HY_WIDTH), 0.1),
        'sinks': nrm(ks[19], (L, W_HEADS), 1.0),
        'w_branch': nrm(ks[20], (L, N_BRANCH, HY_WIDTH, D), HY_WIDTH ** -0.5),
        'w_out': nrm(ks[21], (L, D, D), D ** -0.5),
        'router_g_w': nrm(ks[22], (L, D, N_GROUPS), D ** -0.5),
        'router_g_b': nrm(ks[23], (L, N_GROUPS), 0.01),
        'router_e_w': nrm(ks[24], (L, D, N_EXPERTS), D ** -0.5),
        'router_e_b': nrm(ks[25], (L, N_EXPERTS), 0.01),
        'exp_w1': nrm(ks[26], (L, N_EXPERTS, D, D_EXPERT), D ** -0.5),
        'exp_w3': nrm(ks[27], (L, N_EXPERTS, D, D_EXPERT), D ** -0.5),
        'exp_w2': nrm(ks[28], (L, N_EXPERTS, D_EXPERT, D), D_EXPERT ** -0.5),
    }


def reference(x, c, ctx, c_ctx, ada_w, ada_b, norm1_g, norm2_g, w_in, qk_g,
              hy_short_w, hy_short_b, hy_f_w1, hy_f_b1, hy_f_freq, hy_f_w2, hy_f_b2, hy_f_w3, hy_bias,
              sinks, w_branch, w_out, router_g_w, router_g_b, router_e_w, router_e_b, exp_w1, exp_w3, exp_w2):
    n_tok = x.shape[1]
    n_ctx = ctx.shape[1]
    rows = n_tok // GRID_W
    row_id = jnp.repeat(jnp.arange(rows), GRID_W)
    col_id = jnp.tile(jnp.arange(GRID_W), rows)
    inv_freq = ROPE_THETA ** (-jnp.arange(ROPE_FREQS, dtype=F32) / ROPE_FREQS)
    ang = jnp.stack([row_id[:, None] * inv_freq, col_id[:, None] * inv_freq], axis=1)
    cos, sin = jnp.cos(ang), jnp.sin(ang)
    nb = n_tok // Q_BLOCK
    blk = jnp.arange(nb)[:, None, None]
    qpos = blk * Q_BLOCK + jnp.arange(Q_BLOCK)[None, :, None]
    kpos = (blk - 1) * Q_BLOCK + jnp.arange(3 * Q_BLOCK)[None, None, :]
    band = (jnp.abs(kpos - qpos) <= WINDOW) & (kpos >= 0) & (kpos < n_tok)
    win_mask = jnp.concatenate([band, jnp.ones((nb, Q_BLOCK, n_ctx), dtype=bool)], axis=-1)

    xc = ctx
    for l in range(DEPTH):
        x, xc = trunk_layer(x, xc, c, c_ctx, cos, sin, win_mask, l < DEPTH - 1,
                            ada_w[l], ada_b[l], norm1_g[l], norm2_g[l], w_in[l], qk_g[l],
                            hy_short_w[l], hy_short_b[l], hy_f_w1[l], hy_f_b1[l], hy_f_freq[l], hy_f_w2[l],
                            hy_f_b2[l], hy_f_w3[l], hy_bias[l], sinks[l], w_branch[l], w_out[l],
                            router_g_w[l], router_g_b[l], router_e_w[l], router_e_b[l],
                            exp_w1[l], exp_w3[l], exp_w2[l])
    return x
```

```python
import functools
import math

import jax
import jax.numpy as jnp
from jax import lax
from jax.experimental import pallas as pl
from jax.experimental.pallas import tpu as pltpu

F32 = jnp.float32
BF16 = jnp.bfloat16
HIGHEST = lax.Precision.HIGHEST

D_MODEL = 1024
GRID_W = 64
HEAD_DIM = 64
ROPE_FREQS = HEAD_DIM // 4
ROPE_THETA = 10000.0
ATTN_SCALE = HEAD_DIM ** -0.5
NEG_INF = -1e30
WINDOW = 128
A_HEADS = 8
A_KV = 2
W_HEADS = 8
W_KV = 2
HY_WIDTH = 512
HY_BANDS = 16
HY_HIDDEN = 64
HY_TARGET = 1e-2
HY_FAST_DECAY = 0.3
HY_SLOW_DECAY = 1.5
N_BRANCH = 3
N_GROUPS = 4
EXPERTS_PER_GROUP = 4
N_EXPERTS = N_GROUPS * EXPERTS_PER_GROUP
D_EXPERT = 512
N_MOD = 6
EPS = 1e-6

QA_W = A_HEADS * HEAD_DIM
KA_W = A_KV * HEAD_DIM
QW_W = W_HEADS * HEAD_DIM
KW_W = W_KV * HEAD_DIM
O_QA = 0
O_KA = O_QA + QA_W
O_VA = O_KA + KA_W
O_UH = O_VA + KA_W
O_QW = O_UH + 3 * HY_WIDTH
O_KW = O_QW + QW_W
O_VW = O_KW + KW_W
O_GT = O_VW + KW_W
D_IN = O_GT + N_BRANCH * D_MODEL

MOD_ROWS = 24
LANES = 128
ROUTER_W = LANES
VMEM_LIMIT = 56 * 1024 * 1024


def _cparams(sem):
    return pltpu.CompilerParams(dimension_semantics=sem, vmem_limit_bytes=VMEM_LIMIT)


def _const_spec(shape):
    nd = len(shape)
    return pl.BlockSpec(shape, lambda *_: (0,) * nd, pipeline_mode=pl.Buffered(1))


def _dot(a, b):
    return jnp.dot(a, b, preferred_element_type=F32)


def _dot_nt(a, b):
    return lax.dot_general(a, b, (((1,), (1,)), ((), ())), preferred_element_type=F32)


def _dot_hi(a, b):
    return jnp.dot(a, b, precision=HIGHEST, preferred_element_type=F32)


def _ada_kernel(cc_ref, w_ref, b_ref, o_ref):
    cc = cc_ref[...]
    act = cc * jax.nn.sigmoid(cc)
    o_ref[0] = _dot_hi(act, w_ref[0]) + b_ref[0]


def _ada_table(cc, ada_w, ada_b):
    depth = ada_w.shape[0]
    tn = 1536
    nmod = ada_w.shape[2]
    out = pl.pallas_call(
        _ada_kernel,
        out_shape=jax.ShapeDtypeStruct((depth, MOD_ROWS, nmod), F32),
        grid=(depth, nmod // tn),
        in_specs=[
            pl.BlockSpec((MOD_ROWS, D_MODEL), lambda l, j: (0, 0)),
            pl.BlockSpec((1, D_MODEL, tn), lambda l, j: (l, 0, j)),
            pl.BlockSpec((1, 1, tn), lambda l, j: (l, 0, j)),
        ],
        out_specs=pl.BlockSpec((1, MOD_ROWS, tn), lambda l, j: (l, 0, j)),
        compiler_params=_cparams(("arbitrary", "arbitrary")),
        name="ada_table",
    )(cc, ada_w, ada_b.reshape(depth, 1, nmod))
    return out.reshape(depth, MOD_ROWS, N_MOD, D_MODEL)


def _in_proj_kernel(*refs, rope):
    if rope:
        (x_ref, mod_ref, g1_ref, w_ref, gqa_ref, gka_ref, gqw_ref, gkw_ref, hs_ref, cos_ref, sin_ref,
         qa_ref, ka_ref, va_ref, uh_ref, qw_ref, kw_ref, vw_ref, gt_ref) = refs
    else:
        (x_ref, mod_ref, g1_ref, w_ref, gqa_ref, gka_ref, gqw_ref, gkw_ref, hs_ref,
         qa_ref, ka_ref, va_ref, uh_ref, qw_ref, kw_ref, vw_ref, gt_ref) = refs
        cos_ref = sin_ref = None
    x = x_ref[0]
    ms = jnp.mean(x * x, axis=-1, keepdims=True)
    h = x * lax.rsqrt(ms + EPS) * g1_ref[...]
    h = (h * (1.0 + mod_ref[0, 1:2, :]) + mod_ref[0, 0:1, :]).astype(BF16)

    def proj(lo, width):
        return _dot(h, w_ref[:, lo:lo + width])

    def head_norm_rope(t, g_ref):
        width = t.shape[-1]
        sq = t * t
        sq_hi = sq.astype(BF16)
        sq_lo = (sq - sq_hi.astype(F32)).astype(BF16)
        ones_blk = hs_ref[0:width, 0:width]
        ssum = _dot(sq_hi, ones_blk) + _dot(sq_lo, ones_blk)
        tn = t * lax.rsqrt(ssum * (1.0 / HEAD_DIM) + EPS) * g_ref[...]
        if rope:
            reps = width // cos_ref.shape[-1]
            cos = cos_ref[...]
            sin = sin_ref[...]
            if reps > 1:
                cos = jnp.concatenate([cos] * reps, axis=-1)
                sin = jnp.concatenate([sin] * reps, axis=-1)
            lane = lax.broadcasted_iota(jnp.int32, tn.shape, 1)
            first = (lane & (2 * ROPE_FREQS - 1)) < ROPE_FREQS
            partner = jnp.where(first, pltpu.roll(tn, width - ROPE_FREQS, axis=1), pltpu.roll(tn, ROPE_FREQS, axis=1))
            tn = tn * cos + partner * sin
        return tn

    def store_heads(o_ref, t):
        for hh in range(t.shape[-1] // HEAD_DIM):
            o_ref[0, hh] = t[:, hh * HEAD_DIM:(hh + 1) * HEAD_DIM].astype(o_ref.dtype)

    store_heads(qa_ref, head_norm_rope(proj(O_QA, QA_W), gqa_ref) * ATTN_SCALE)
    store_heads(ka_ref, head_norm_rope(proj(O_KA, KA_W), gka_ref))
    store_heads(va_ref, proj(O_VA, KA_W))
    uh_ref[0] = proj(O_UH, 3 * HY_WIDTH)
    store_heads(qw_ref, head_norm_rope(proj(O_QW, QW_W), gqw_ref) * ATTN_SCALE)
    store_heads(kw_ref, head_norm_rope(proj(O_KW, KW_W), gkw_ref))
    store_heads(vw_ref, proj(O_VW, KW_W))
    gt_ref[0] = jax.nn.sigmoid(proj(O_GT, N_BRANCH * D_MODEL)).astype(gt_ref.dtype)


def _in_proj(x, mod, mod_row, g1, w_in, gq, hs, rope_tabs):
    bsz, s, d = x.shape
    tm = min(256, s)
    rope = rope_tabs is not None
    in_specs = [
        pl.BlockSpec((1, tm, d), lambda b, t: (b, t, 0)),
        pl.BlockSpec((1, N_MOD, d), lambda b, t: (mod_row(b), 0, 0)),
        _const_spec((1, d)),
        _const_spec((d, D_IN)),
        _const_spec((1, QA_W)), _const_spec((1, KA_W)), _const_spec((1, QW_W)), _const_spec((1, KW_W)),
        _const_spec((QA_W, QA_W)),
    ]
    args = [x, mod, g1, w_in, gq[0], gq[1], gq[2], gq[3], hs]
    if rope:
        in_specs += [pl.BlockSpec((tm, KA_W), lambda b, t: (t, 0))] * 2
        args += list(rope_tabs)

    def head_out(n_heads):
        return (jax.ShapeDtypeStruct((bsz, n_heads, s, HEAD_DIM), BF16),
                pl.BlockSpec((1, n_heads, tm, HEAD_DIM), lambda b, t: (b, 0, t, 0)))

    def flat_out(width, dtype):
        return (jax.ShapeDtypeStruct((bsz, s, width), dtype), pl.BlockSpec((1, tm, width), lambda b, t: (b, t, 0)))

    outs = [head_out(A_HEADS), head_out(A_KV), head_out(A_KV), flat_out(3 * HY_WIDTH, F32),
            head_out(W_HEADS), head_out(W_KV), head_out(W_KV), flat_out(N_BRANCH * D_MODEL, BF16)]
    return pl.pallas_call(
        functools.partial(_in_proj_kernel, rope=rope),
        out_shape=[o[0] for o in outs],
        grid=(bsz, s // tm),
        in_specs=in_specs,
        out_specs=[o[1] for o in outs],
        compiler_params=_cparams(("parallel", "parallel")),
        name="in_proj",
    )(*args)


def _attn_kernel(*refs, n_g, tq, mode, s_len):
    refs = list(refs)
    sink_ref = refs.pop(0) if mode in ("window", "ctx_sink") else None
    q_ref = refs.pop(0)
    if mode == "global":
        kl_ref, vl_ref = refs.pop(0), refs.pop(0)
    elif mode == "window":
        kwin_refs = [refs.pop(0) for _ in range(3)]
        vwin_refs = [refs.pop(0) for _ in range(3)]
    kc_ref, vc_ref, o_ref = refs
    kv = pl.program_id(1)
    kc = kc_ref[0, 0]
    vc = vc_ref[0, 0]
    if mode == "global":
        kl = kl_ref[0, 0]
        vl = vl_ref[0, 0]
    elif mode == "window":
        qt = pl.program_id(2)
        kl = jnp.concatenate([r[0, 0] for r in kwin_refs], axis=0)
        vl = jnp.concatenate([r[0, 0] for r in vwin_refs], axis=0)
        qpos = lax.broadcasted_iota(jnp.int32, (tq, 3 * tq), 0) + qt * tq
        kpos = lax.broadcasted_iota(jnp.int32, (tq, 3 * tq), 1) + (qt - 1) * tq
        valid = (kpos - qpos <= WINDOW) & (qpos - kpos <= WINDOW) & (kpos >= 0) & (kpos < s_len)
    outs = []
    for g in range(n_g):
        q = q_ref[0, g]
        sc = _dot_nt(q, kc)
        m = jnp.max(sc, axis=-1, keepdims=True)
        if mode in ("global", "window"):
            sl = _dot_nt(q, kl)
            if mode == "window":
                sl = jnp.where(valid, sl, NEG_INF)
            m = jnp.maximum(m, jnp.max(sl, axis=-1, keepdims=True))
        if sink_ref is not None:
            sink = sink_ref[kv * n_g + g]
            m = jnp.maximum(m, sink)
        pc = jnp.exp(sc - m)
        den = jnp.sum(pc, axis=-1, keepdims=True)
        acc = _dot(pc.astype(BF16), vc)
        if mode in ("global", "window"):
            plat = jnp.exp(sl - m)
            den = den + jnp.sum(plat, axis=-1, keepdims=True)
            acc = acc + _dot(plat.astype(BF16), vl)
        if sink_ref is not None:
            den = den + jnp.exp(sink - m)
        outs.append(acc / den)
    o_ref[0] = jnp.concatenate(outs, axis=-1).astype(o_ref.dtype)


def _attention(q, kc, vc, *, mode, kl=None, vl=None, sinks=None, tq):
    bsz, n_h, sq, hd = q.shape
    n_kv = kc.shape[1]
    n_g = n_h // n_kv
    sc_len = kc.shape[2]
    tq = min(tq, sq)
    nq = sq // tq
    in_specs, args = [], []
    if mode in ("window", "ctx_sink"):
        in_specs.append(pl.BlockSpec(memory_space=pltpu.SMEM))
        args.append(sinks)
    in_specs.append(pl.BlockSpec((1, n_g, tq, hd), lambda b, k, t: (b, k, t, 0)))
    args.append(q)
    if mode == "global":
        sl_len = kl.shape[2]
        in_specs += [pl.BlockSpec((1, 1, sl_len, hd), lambda b, k, t: (b, k, 0, 0))] * 2
        args += [kl, vl]
    elif mode == "window":
        win = [pl.BlockSpec((1, 1, tq, hd), lambda b, k, t: (b, k, jnp.maximum(t - 1, 0), 0)),
               pl.BlockSpec((1, 1, tq, hd), lambda b, k, t: (b, k, t, 0)),
               pl.BlockSpec((1, 1, tq, hd), lambda b, k, t: (b, k, jnp.minimum(t + 1, nq - 1), 0))]
        in_specs += win + win
        args += [kl, kl, kl, vl, vl, vl]
    in_specs += [pl.BlockSpec((1, 1, sc_len, hd), lambda b, k, t: (b, k, 0, 0))] * 2
    args += [kc, vc]
    return pl.pallas_call(
        functools.partial(_attn_kernel, n_g=n_g, tq=tq, mode=mode, s_len=sq),
        out_shape=jax.ShapeDtypeStruct((bsz, sq, n_h * hd), BF16),
        grid=(bsz, n_kv, nq),
        in_specs=in_specs,
        out_specs=pl.BlockSpec((1, tq, n_g * hd), lambda b, k, t: (b, t, k)),
        compiler_params=_cparams(("parallel", "parallel", "parallel")),
        name="attn_" + mode,
    )(*args)


def _dft_tables(n):
    k = jnp.arange(n, dtype=jnp.int32)[:, None]
    s = jnp.arange(n, dtype=jnp.int32)[None, :]
    ang = ((k * s) % (2 * n)).astype(F32) * (math.pi / n)
    cos = jnp.cos(ang)
    sin = jnp.where(k == 0, jnp.where(s % 2 == 0, 1.0, -1.0), jnp.sin(ang))
    fwd = jnp.concatenate([cos, sin], axis=0).astype(BF16)
    return fwd, fwd.T


def _filter_kernel(band_ref, w1_ref, b1_ref, fr_ref, w2_ref, b2_ref, w3_ref, dl_ref, fs_ref, fd_ref, nrm_ref, *, n):
    pos_i = lax.broadcasted_iota(jnp.int32, (n, LANES), 0)
    lane = lax.broadcasted_iota(jnp.int32, (n, LANES), 1)
    pos = pos_i.astype(F32)
    t = pos / (n - 1)
    ang = band_ref[...] * pos / n
    z = jnp.where(lane == 0, t,
                  jnp.where(lane <= HY_BANDS, jnp.cos(ang),
                            jnp.where(lane <= 2 * HY_BANDS, -jnp.sin(ang), 0.0)))
    freq = fr_ref[...]
    h = jnp.sin(freq * (_dot_hi(z, w1_ref[...]) + b1_ref[...]))
    h = jnp.sin(freq * (_dot_hi(h, w2_ref[...]) + b2_ref[...]))
    h = _dot_hi(h, w3_ref[...])
    tc = t[:, 0:1]
    win = jnp.exp(-tc * dl_ref[...])
    fwd = h[:, :HY_WIDTH] * win
    bwd = jnp.where(pos_i[:, 0:1] == 0, 0.0, h[:, HY_WIDTH:] * win)
    fs_ref[...] = fwd + bwd
    fd_ref[...] = fwd - bwd
    nrm_ref[...] = jnp.sum(jnp.abs(fwd) + jnp.abs(bwd), axis=0, keepdims=True)


def _hyena_filter(n, band_row, w1p, b1, freq, w2, b2, w3, deltas):
    return pl.pallas_call(
        functools.partial(_filter_kernel, n=n),
        out_shape=[jax.ShapeDtypeStruct((n, HY_WIDTH), F32), jax.ShapeDtypeStruct((n, HY_WIDTH), F32),
                   jax.ShapeDtypeStruct((1, HY_WIDTH), F32)],
        compiler_params=pltpu.CompilerParams(vmem_limit_bytes=VMEM_LIMIT),
        name="hyena_filter",
    )(band_row, w1p, b1, freq, w2, b2, w3, deltas)


def _spectrum_kernel(f_ref, x_ref, o_ref):
    o_ref[...] = _dot(f_ref[...], x_ref[...].astype(BF16))


def _filter_spectrum(fwd_tab, x):
    n2, n = fwd_tab.shape
    w = x.shape[1]
    tc = 256
    return pl.pallas_call(
        _spectrum_kernel,
        out_shape=jax.ShapeDtypeStruct((n2, w), F32),
        grid=(w // tc,),
        in_specs=[_const_spec((n2, n)), pl.BlockSpec((n, tc), lambda j: (0, j))],
        out_specs=pl.BlockSpec((n2, tc), lambda j: (0, j)),
        compiler_params=_cparams(("arbitrary",)),
        name="hyena_filter_spectrum",
    )(fwd_tab, x)


def _short_conv(x, w_ref, b_ref):
    n = x.shape[0]
    row = lax.broadcasted_iota(jnp.int32, x.shape, 0)
    prev = jnp.where(row == 0, 0.0, pltpu.roll(x, 1, axis=0))
    nxt = jnp.where(row == n - 1, 0.0, pltpu.roll(x, n - 1, axis=0))
    return w_ref[0:1, :] * prev + w_ref[1:2, :] * x + w_ref[2:3, :] * nxt + b_ref[...]


def _hyena_fwd_kernel(f_ref, x1_ref, v_ref, w1_ref, b1_ref, wv_ref, bv_ref, kr_ref, ka_ref, kn_ref, nrm_ref,
                      y_ref, u_ref, *, n):
    x1 = _short_conv(x1_ref[0], w1_ref, b1_ref)
    v = _short_conv(v_ref[0], wv_ref, bv_ref)
    ub = (v * x1).astype(BF16)
    u_ref[0] = ub
    spec = _dot(f_ref[...], ub)
    ur = spec[0:n]
    ua = spec[n:2 * n]
    row0 = lax.broadcasted_iota(jnp.int32, ur.shape, 0) == 0
    scale = jnp.where(row0, 1.0 / (2 * n), 1.0 / n) / nrm_ref[...]
    kr = kr_ref[...]
    ka = jnp.where(row0, 0.0, ka_ref[...])
    k4 = jnp.where(row0, kn_ref[...], kr)
    y_ref[0, 0:n] = ((ur * kr - ua * ka) * scale).astype(y_ref.dtype)
    y_ref[0, n:2 * n] = ((ur * ka + ua * k4) * scale).astype(y_ref.dtype)


def _hyena_inv_kernel(ft_ref, y_ref, u_ref, x0_ref, w0_ref, b0_ref, skip_ref, o_ref):
    conv = _dot(ft_ref[...], y_ref[0])
    x0 = _short_conv(x0_ref[0], w0_ref, b0_ref)
    o_ref[0] = (x0 * (conv + skip_ref[...] * u_ref[0].astype(F32))).astype(o_ref.dtype)


def _hyena(uh, tabs, spec, knyq, nrm, short_w, short_b, skip):
    fwd_tab, inv_tab = tabs
    bsz, n, _ = uh.shape
    tc = 256
    nct = HY_WIDTH // tc
    chan = lambda off: (lambda c, b: (b, 0, off + c))
    par = lambda off: (lambda c, b: (0, off + c))
    y, u = pl.pallas_call(
        functools.partial(_hyena_fwd_kernel, n=n),
        out_shape=[jax.ShapeDtypeStruct((bsz, 2 * n, HY_WIDTH), BF16), jax.ShapeDtypeStruct((bsz, n, HY_WIDTH), BF16)],
        grid=(nct, bsz),
        in_specs=[
            _const_spec((2 * n, n)),
            pl.BlockSpec((1, n, tc), chan(nct)), pl.BlockSpec((1, n, tc), chan(2 * nct)),
            pl.BlockSpec((3, tc), par(nct)), pl.BlockSpec((1, tc), par(nct)),
            pl.BlockSpec((3, tc), par(2 * nct)), pl.BlockSpec((1, tc), par(2 * nct)),
            pl.BlockSpec((n, tc), lambda c, b: (0, c)), pl.BlockSpec((n, tc), lambda c, b: (1, nct + c)),
            pl.BlockSpec((1, tc), par(0)), pl.BlockSpec((1, tc), par(0)),
        ],
        out_specs=[pl.BlockSpec((1, 2 * n, tc), lambda c, b: (b, 0, c)), pl.BlockSpec((1, n, tc), lambda c, b: (b, 0, c))],
        compiler_params=_cparams(("arbitrary", "arbitrary")),
        name="hyena_fwd",
    )(fwd_tab, uh, uh, short_w, short_b, short_w, short_b, spec, spec, knyq, nrm)
    return pl.pallas_call(
        _hyena_inv_kernel,
        out_shape=jax.ShapeDtypeStruct((bsz, n, HY_WIDTH), BF16),
        grid=(nct, bsz),
        in_specs=[
            _const_spec((n, 2 * n)),
            pl.BlockSpec((1, 2 * n, tc), lambda c, b: (b, 0, c)), pl.BlockSpec((1, n, tc), lambda c, b: (b, 0, c)),
            pl.BlockSpec((1, n, tc), chan(0)), pl.BlockSpec((3, tc), par(0)), pl.BlockSpec((1, tc), par(0)),
            pl.BlockSpec((1, tc), par(0)),
        ],
        out_specs=pl.BlockSpec((1, n, tc), lambda c, b: (b, 0, c)),
        compiler_params=_cparams(("arbitrary", "arbitrary")),
        name="hyena_inv",
    )(inv_tab, y, u, uh, short_w, short_b, skip)


def _merge_kernel(x_ref, mod_ref, ya_ref, yh_ref, yw_ref, gt_ref, wb_ref, wo_ref, g2_ref, rw_ref, rb_ref,
                  xo_ref, tok_ref, comb_ref):
    gt = gt_ref[0]
    m = gt[:, 0:D_MODEL].astype(F32) * _dot(ya_ref[0], wb_ref[0])
    m = m + gt[:, D_MODEL:2 * D_MODEL].astype(F32) * _dot(yh_ref[0], wb_ref[1])
    m = m + gt[:, 2 * D_MODEL:3 * D_MODEL].astype(F32) * _dot(yw_ref[0], wb_ref[2])
    x = x_ref[0] + mod_ref[0, 2:3, :] * _dot(m.astype(BF16), wo_ref[...])
    xo_ref[0] = x
    ms = jnp.mean(x * x, axis=-1, keepdims=True)
    tok = x * lax.rsqrt(ms + EPS) * g2_ref[...]
    tok = tok * (1.0 + mod_ref[0, 4:5, :]) + mod_ref[0, 3:4, :]
    tok_ref[0] = tok.astype(tok_ref.dtype)

    lg = _dot_hi(tok, rw_ref[...]) + rb_ref[...]
    lane = lax.broadcasted_iota(jnp.int32, lg.shape, 1).astype(F32)
    big = float(ROUTER_W)

    def first_lane(mask):
        return jnp.min(jnp.where(mask, lane, big), axis=-1, keepdims=True)

    is_g = lane < N_GROUPS
    gmax = jnp.max(jnp.where(is_g, lg, NEG_INF), axis=-1, keepdims=True)
    g_sel = first_lane(jnp.where(is_g, lg, NEG_INF) == gmax)
    p_g = 1.0 / jnp.sum(jnp.where(is_g, jnp.exp(lg - gmax), 0.0), axis=-1, keepdims=True)
    lo = N_GROUPS + g_sel * EXPERTS_PER_GROUP
    in_grp = (lane >= lo) & (lane < lo + EXPERTS_PER_GROUP)
    le = jnp.where(in_grp, lg, NEG_INF)
    v1 = jnp.max(le, axis=-1, keepdims=True)
    i1 = first_lane(le == v1)
    le = jnp.where(lane == i1, NEG_INF, le)
    v2 = jnp.max(le, axis=-1, keepdims=True)
    i2 = first_lane(le == v2)
    e2 = jnp.exp(v2 - v1)
    p1 = p_g / (1.0 + e2)
    p2 = p_g * e2 / (1.0 + e2)
    comb_ref[0] = jnp.where(lane == i1, p1, jnp.where(lane == i2, p2, 0.0))


def _merge(x, mod, mod_row, ya, yh, yw, gates, w_branch, w_out, g2, rw, rb):
    bsz, s, d = x.shape
    tm = min(256, s)
    row = lambda b, t: (b, t, 0)
    return pl.pallas_call(
        _merge_kernel,
        out_shape=[jax.ShapeDtypeStruct((bsz, s, d), F32), jax.ShapeDtypeStruct((bsz, s, d), BF16),
                   jax.ShapeDtypeStruct((bsz, s, ROUTER_W), F32)],
        grid=(bsz, s // tm),
        in_specs=[
            pl.BlockSpec((1, tm, d), row),
            pl.BlockSpec((1, N_MOD, d), lambda b, t: (mod_row(b), 0, 0)),
            pl.BlockSpec((1, tm, QA_W), row), pl.BlockSpec((1, tm, HY_WIDTH), row), pl.BlockSpec((1, tm, QW_W), row),
            pl.BlockSpec((1, tm, N_BRANCH * d), row),
            _const_spec((N_BRANCH, HY_WIDTH, d)), _const_spec((d, d)), _const_spec((1, d)),
            _const_spec((d, ROUTER_W)), _const_spec((1, ROUTER_W)),
        ],
        out_specs=[pl.BlockSpec((1, tm, d), row), pl.BlockSpec((1, tm, d), row), pl.BlockSpec((1, tm, ROUTER_W), row)],
        compiler_params=_cparams(("parallel", "parallel")),
        name="merge_router",
    )(x, mod, ya, yh, yw, gates, w_branch, w_out, g2, rw, rb)


def _moe_kernel(x_ref, mod_ref, tok_ref, comb_ref, w1_ref, w3_ref, w2_ref, o_ref, acc_ref):
    e = pl.program_id(2)

    @pl.when(e == 0)
    def _():
        acc_ref[...] = jnp.zeros_like(acc_ref)

    tok = tok_ref[0]
    h1 = _dot(tok, w1_ref[0])
    h = (h1 * jax.nn.sigmoid(h1)) * _dot(tok, w3_ref[0])
    comb = comb_ref[0]
    lane = lax.broadcasted_iota(jnp.int32, comb.shape, 1)
    ce = jnp.sum(jnp.where(lane == N_GROUPS + e, comb, 0.0), axis=-1, keepdims=True)
    acc_ref[...] += ce * _dot(h.astype(BF16), w2_ref[0])

    @pl.when(e == N_EXPERTS - 1)
    def _():
        o_ref[0] = x_ref[0] + mod_ref[0, 5:6, :] * acc_ref[...]


def _moe(x, mod, mod_row, tok, comb, w1, w3, w2):
    bsz, s, d = x.shape
    tm = min(1024, s)
    row = lambda b, t, e: (b, t, 0)
    return pl.pallas_call(
        _moe_kernel,
        out_shape=jax.ShapeDtypeStruct((bsz, s, d), F32),
        grid=(bsz, s // tm, N_EXPERTS),
        in_specs=[
            pl.BlockSpec((1, tm, d), row),
            pl.BlockSpec((1, N_MOD, d), lambda b, t, e: (mod_row(b), 0, 0)),
            pl.BlockSpec((1, tm, d), row),
            pl.BlockSpec((1, tm, ROUTER_W), row),
            pl.BlockSpec((1, d, D_EXPERT), lambda b, t, e: (e, 0, 0)),
            pl.BlockSpec((1, d, D_EXPERT), lambda b, t, e: (e, 0, 0)),
            pl.BlockSpec((1, D_EXPERT, d), lambda b, t, e: (e, 0, 0)),
        ],
        out_specs=pl.BlockSpec((1, tm, d), row),
        scratch_shapes=[pltpu.VMEM((tm, d), F32)],
        compiler_params=_cparams(("parallel", "parallel", "arbitrary")),
        name="moe_dense",
    )(x, mod, tok, comb, w1, w3, w2)


def _rope_tables(n_tok):
    rows = n_tok // GRID_W
    row_id = jnp.repeat(jnp.arange(rows), GRID_W)
    col_id = jnp.tile(jnp.arange(GRID_W), rows)
    inv_freq = ROPE_THETA ** (-jnp.arange(ROPE_FREQS, dtype=F32) / ROPE_FREQS)
    ang_r = row_id[:, None] * inv_freq
    ang_c = col_id[:, None] * inv_freq
    cos_h = jnp.concatenate([jnp.cos(ang_r), jnp.cos(ang_r), jnp.cos(ang_c), jnp.cos(ang_c)], axis=-1)
    sin_h = jnp.concatenate([-jnp.sin(ang_r), jnp.sin(ang_r), -jnp.sin(ang_c), jnp.sin(ang_c)], axis=-1)
    reps = KA_W // HEAD_DIM
    return jnp.tile(cos_h, (1, reps)), jnp.tile(sin_h, (1, reps))


def kernel(x, c, ctx, c_ctx, ada_w, ada_b, norm1_g, norm2_g, w_in, qk_g, hy_short_w, hy_short_b, hy_f_w1, hy_f_b1,
           hy_f_freq, hy_f_w2, hy_f_b2, hy_f_w3, hy_bias, sinks, w_branch, w_out, router_g_w, router_g_b,
           router_e_w, router_e_b, exp_w1, exp_w3, exp_w2):
    bsz, n_tok, d = x.shape
    n_ctx = ctx.shape[1]
    depth = ada_w.shape[0]
    assert d == D_MODEL and bsz < MOD_ROWS and n_tok % 256 == 0 and n_ctx % 256 == 0

    cc = jnp.zeros((MOD_ROWS, d), F32).at[:bsz].set(c).at[bsz].set(c_ctx)
    mods = _ada_table(cc, ada_w, ada_b)
    lat_row = lambda b: b
    ctx_row = lambda b: bsz

    rope_tabs = _rope_tables(n_tok)
    head_ones = (jnp.arange(QA_W)[:, None] // HEAD_DIM == jnp.arange(QA_W)[None, :] // HEAD_DIM).astype(BF16)
    tabs_lat = _dft_tables(n_tok)
    tabs_ctx = _dft_tables(n_ctx)
    bands = jnp.linspace(1e-4, HY_BANDS - 1, HY_BANDS, dtype=F32)
    band_row = jnp.zeros((1, LANES), F32).at[0, 1:1 + HY_BANDS].set(bands).at[0, 1 + HY_BANDS:1 + 2 * HY_BANDS].set(bands)
    band_row = 2 * math.pi * band_row
    deltas = jnp.abs(jnp.linspace(math.log(HY_TARGET) / HY_SLOW_DECAY, math.log(HY_TARGET) / HY_FAST_DECAY,
                                  HY_WIDTH, dtype=F32))[None, :]

    xc = ctx
    for l in range(depth):
        update_ctx = l < depth - 1
        mod = mods[l]
        w_in_b = w_in[l].astype(BF16)
        gq = [jnp.tile(qk_g[l, 0], A_HEADS)[None], jnp.tile(qk_g[l, 1], A_KV)[None],
              jnp.tile(qk_g[l, 2], W_HEADS)[None], jnp.tile(qk_g[l, 3], W_KV)[None]]
        g1 = norm1_g[l][None]
        qa, ka, va, uh, qw, kw, vw, gts = _in_proj(x, mod, lat_row, g1, w_in_b, gq, head_ones, rope_tabs)
        qa_c, ka_c, va_c, uh_c, qw_c, kw_c, vw_c, gts_c = _in_proj(xc, mod, ctx_row, g1, w_in_b, gq, head_ones, None)

        ya = _attention(qa, ka_c, va_c, mode="global", kl=ka, vl=va, tq=512)
        yw = _attention(qw, kw_c, vw_c, mode="window", kl=kw, vl=vw, sinks=sinks[l], tq=256)

        w1p = jnp.zeros((LANES, HY_HIDDEN), F32).at[:hy_f_w1.shape[1]].set(hy_f_w1[l])
        fargs = (band_row, w1p, hy_f_b1[l][None], hy_f_freq[l][None], hy_f_w2[l], hy_f_b2[l][None], hy_f_w3[l], deltas)
        skip = hy_bias[l][None]
        sb = hy_short_b[l][None]

        def hyena(u_in, n, tabs):
            fsum, fdiff, nrm = _hyena_filter(n, *fargs)
            spec = _filter_spectrum(tabs[0], jnp.concatenate([fsum, fdiff], axis=1))
            return _hyena(u_in, tabs, spec, spec[n:n + 1, :HY_WIDTH], nrm, hy_short_w[l], sb, skip)

        yh = hyena(uh, n_tok, tabs_lat)

        wb = w_branch[l].astype(BF16)
        wo = w_out[l].astype(BF16)
        g2 = norm2_g[l][None]
        rw = jnp.zeros((d, ROUTER_W), F32).at[:, :N_GROUPS].set(router_g_w[l]).at[:, N_GROUPS:N_GROUPS + N_EXPERTS].set(router_e_w[l])
        rb = jnp.zeros((1, ROUTER_W), F32).at[0, :N_GROUPS].set(router_g_b[l]).at[0, N_GROUPS:N_GROUPS + N_EXPERTS].set(router_e_b[l])
        w1 = exp_w1[l].astype(BF16)
        w3 = exp_w3[l].astype(BF16)
        w2 = exp_w2[l].astype(BF16)

        x1, tok, comb = _merge(x, mod, lat_row, ya, yh, yw, gts, wb, wo, g2, rw, rb)
        x = _moe(x1, mod, lat_row, tok, comb, w1, w3, w2)

        if update_ctx:
            ya_c = _attention(qa_c, ka_c, va_c, mode="ctx", tq=256)
            yw_c = _attention(qw_c, kw_c, vw_c, mode="ctx_sink", sinks=sinks[l], tq=256)
            yh_c = hyena(uh_c, n_ctx, tabs_ctx)
            xc1, tok_c, comb_c = _merge(xc, mod, ctx_row, ya_c, yh_c, yw_c, gts_c, wb, wo, g2, rw, rb)
            xc = _moe(xc1, mod, ctx_row, tok_c, comb_c, w1, w3, w2)
    return x
```

```python
import functools
import math

import jax
import jax.numpy as jnp
from jax import lax
from jax.experimental import pallas as pl
from jax.experimental.pallas import tpu as pltpu
from jax.experimental.pallas import tpu_sc as plsc

F32 = jnp.float32
BF16 = jnp.bfloat16
HIGHEST = lax.Precision.HIGHEST

D_MODEL = 1024
GRID_W = 64
HEAD_DIM = 64
ROPE_FREQS = HEAD_DIM // 4
ROPE_THETA = 10000.0
ATTN_SCALE = HEAD_DIM ** -0.5
NEG_INF = -1e30
WINDOW = 128
A_HEADS = 8
A_KV = 2
W_HEADS = 8
W_KV = 2
HY_WIDTH = 512
HY_BANDS = 16
HY_HIDDEN = 64
HY_TARGET = 1e-2
HY_FAST_DECAY = 0.3
HY_SLOW_DECAY = 1.5
N_BRANCH = 3
N_GROUPS = 4
EXPERTS_PER_GROUP = 4
N_EXPERTS = N_GROUPS * EXPERTS_PER_GROUP
D_EXPERT = 512
N_MOD = 6
EPS = 1e-6

QA_W = A_HEADS * HEAD_DIM
KA_W = A_KV * HEAD_DIM
QW_W = W_HEADS * HEAD_DIM
KW_W = W_KV * HEAD_DIM
O_QA = 0
O_KA = O_QA + QA_W
O_VA = O_KA + KA_W
O_UH = O_VA + KA_W
O_QW = O_UH + 3 * HY_WIDTH
O_KW = O_QW + QW_W
O_VW = O_KW + KW_W
O_GT = O_VW + KW_W
D_IN = O_GT + N_BRANCH * D_MODEL

MOD_ROWS = 24
LANES = 128
ROUTER_W = LANES
VMEM_LIMIT = 56 * 1024 * 1024
MOE_TM = 512
SC_CHUNK = 64


def _cparams(sem):
    return pltpu.CompilerParams(dimension_semantics=sem, vmem_limit_bytes=VMEM_LIMIT)


def _const_spec(shape):
    nd = len(shape)
    return pl.BlockSpec(shape, lambda *_: (0,) * nd, pipeline_mode=pl.Buffered(1))


def _dot(a, b):
    return jnp.dot(a, b, preferred_element_type=F32)


def _dot_nt(a, b):
    return lax.dot_general(a, b, (((1,), (1,)), ((), ())), preferred_element_type=F32)


def _dot_hi(a, b):
    return jnp.dot(a, b, precision=HIGHEST, preferred_element_type=F32)


def _pack_bf16_pairs(t):
    half = t.shape[-1] // 2
    hi = lax.bitcast_convert_type(t[:, :half].astype(BF16).astype(F32), jnp.uint32)
    lo = lax.bitcast_convert_type(t[:, half:].astype(BF16).astype(F32), jnp.uint32)
    return hi | (lo >> 16)


def _unpack_bf16_pairs(p):
    hi = lax.bitcast_convert_type(p & jnp.uint32(0xFFFF0000), F32)
    lo = lax.bitcast_convert_type(p << 16, F32)
    return jnp.concatenate([hi, lo], axis=-1)


def _ada_kernel(cc_ref, w_ref, b_ref, o_ref):
    cc = cc_ref[...]
    act = cc * jax.nn.sigmoid(cc)
    o_ref[0] = _dot_hi(act, w_ref[0]) + b_ref[0]


def _ada_table(cc, ada_w, ada_b):
    depth = ada_w.shape[0]
    tn = 1536
    nmod = ada_w.shape[2]
    out = pl.pallas_call(
        _ada_kernel,
        out_shape=jax.ShapeDtypeStruct((depth, MOD_ROWS, nmod), F32),
        grid=(depth, nmod // tn),
        in_specs=[
            pl.BlockSpec((MOD_ROWS, D_MODEL), lambda l, j: (0, 0)),
            pl.BlockSpec((1, D_MODEL, tn), lambda l, j: (l, 0, j)),
            pl.BlockSpec((1, 1, tn), lambda l, j: (l, 0, j)),
        ],
        out_specs=pl.BlockSpec((1, MOD_ROWS, tn), lambda l, j: (l, 0, j)),
        compiler_params=_cparams(("arbitrary", "arbitrary")),
        name="ada_table",
    )(cc, ada_w, ada_b.reshape(depth, 1, nmod))
    return out.reshape(depth, MOD_ROWS, N_MOD, D_MODEL)


def _in_proj_kernel(*refs, rope):
    if rope:
        (x_ref, mod_ref, g1_ref, w_ref, gqa_ref, gka_ref, gqw_ref, gkw_ref, hs_ref, cos_ref, sin_ref,
         qa_ref, ka_ref, va_ref, uh_ref, qw_ref, kw_ref, vw_ref, gt_ref) = refs
    else:
        (x_ref, mod_ref, g1_ref, w_ref, gqa_ref, gka_ref, gqw_ref, gkw_ref, hs_ref,
         qa_ref, ka_ref, va_ref, uh_ref, qw_ref, kw_ref, vw_ref, gt_ref) = refs
        cos_ref = sin_ref = None
    x = x_ref[0]
    ms = jnp.mean(x * x, axis=-1, keepdims=True)
    h = x * lax.rsqrt(ms + EPS) * g1_ref[...]
    h = (h * (1.0 + mod_ref[0, 1:2, :]) + mod_ref[0, 0:1, :]).astype(BF16)

    def proj(lo, width):
        return _dot(h, w_ref[:, lo:lo + width])

    def head_norm_rope(t, g_ref):
        width = t.shape[-1]
        sq = t * t
        sq_hi = sq.astype(BF16)
        sq_lo = (sq - sq_hi.astype(F32)).astype(BF16)
        ones_blk = hs_ref[0:width, 0:width]
        ssum = _dot(sq_hi, ones_blk) + _dot(sq_lo, ones_blk)
        tn = t * lax.rsqrt(ssum * (1.0 / HEAD_DIM) + EPS) * g_ref[...]
        if rope:
            reps = width // cos_ref.shape[-1]
            cos = cos_ref[...]
            sin = sin_ref[...]
            if reps > 1:
                cos = jnp.concatenate([cos] * reps, axis=-1)
                sin = jnp.concatenate([sin] * reps, axis=-1)
            lane = lax.broadcasted_iota(jnp.int32, tn.shape, 1)
            first = (lane & (2 * ROPE_FREQS - 1)) < ROPE_FREQS
            partner = jnp.where(first, pltpu.roll(tn, width - ROPE_FREQS, axis=1), pltpu.roll(tn, ROPE_FREQS, axis=1))
            tn = tn * cos + partner * sin
        return tn

    def store_heads(o_ref, t):
        for hh in range(t.shape[-1] // HEAD_DIM):
            o_ref[0, hh] = t[:, hh * HEAD_DIM:(hh + 1) * HEAD_DIM].astype(o_ref.dtype)

    store_heads(qa_ref, head_norm_rope(proj(O_QA, QA_W), gqa_ref) * ATTN_SCALE)
    store_heads(ka_ref, head_norm_rope(proj(O_KA, KA_W), gka_ref))
    store_heads(va_ref, proj(O_VA, KA_W))
    uh_ref[0] = proj(O_UH, 3 * HY_WIDTH)
    store_heads(qw_ref, head_norm_rope(proj(O_QW, QW_W), gqw_ref) * ATTN_SCALE)
    store_heads(kw_ref, head_norm_rope(proj(O_KW, KW_W), gkw_ref))
    store_heads(vw_ref, proj(O_VW, KW_W))
    gt_ref[0] = jax.nn.sigmoid(proj(O_GT, N_BRANCH * D_MODEL)).astype(gt_ref.dtype)


def _in_proj(x, mod, mod_row, g1, w_in, gq, hs, rope_tabs):
    bsz, s, d = x.shape
    tm = min(256, s)
    rope = rope_tabs is not None
    in_specs = [
        pl.BlockSpec((1, tm, d), lambda b, t: (b, t, 0)),
        pl.BlockSpec((1, N_MOD, d), lambda b, t: (mod_row(b), 0, 0)),
        _const_spec((1, d)),
        _const_spec((d, D_IN)),
        _const_spec((1, QA_W)), _const_spec((1, KA_W)), _const_spec((1, QW_W)), _const_spec((1, KW_W)),
        _const_spec((QA_W, QA_W)),
    ]
    args = [x, mod, g1, w_in, gq[0], gq[1], gq[2], gq[3], hs]
    if rope:
        in_specs += [pl.BlockSpec((tm, KA_W), lambda b, t: (t, 0))] * 2
        args += list(rope_tabs)

    def head_out(n_heads):
        return (jax.ShapeDtypeStruct((bsz, n_heads, s, HEAD_DIM), BF16),
                pl.BlockSpec((1, n_heads, tm, HEAD_DIM), lambda b, t: (b, 0, t, 0)))

    def flat_out(width, dtype):
        return (jax.ShapeDtypeStruct((bsz, s, width), dtype), pl.BlockSpec((1, tm, width), lambda b, t: (b, t, 0)))

    outs = [head_out(A_HEADS), head_out(A_KV), head_out(A_KV), flat_out(3 * HY_WIDTH, F32),
            head_out(W_HEADS), head_out(W_KV), head_out(W_KV), flat_out(N_BRANCH * D_MODEL, BF16)]
    return pl.pallas_call(
        functools.partial(_in_proj_kernel, rope=rope),
        out_shape=[o[0] for o in outs],
        grid=(bsz, s // tm),
        in_specs=in_specs,
        out_specs=[o[1] for o in outs],
        compiler_params=_cparams(("parallel", "parallel")),
        name="in_proj",
    )(*args)


def _attn_kernel(*refs, n_g, tq, mode, s_len):
    refs = list(refs)
    sink_ref = refs.pop(0) if mode in ("window", "ctx_sink") else None
    q_ref = refs.pop(0)
    if mode == "global":
        kl_ref, vl_ref = refs.pop(0), refs.pop(0)
    elif mode == "window":
        n_win = tq // WINDOW + 2
        kwin_refs = [refs.pop(0) for _ in range(n_win)]
        vwin_refs = [refs.pop(0) for _ in range(n_win)]
    kc_ref, vc_ref, o_ref = refs
    kv = pl.program_id(1)
    kc = kc_ref[0, 0]
    vc = vc_ref[0, 0]
    if mode == "global":
        kl = kl_ref[0, 0]
        vl = vl_ref[0, 0]
    elif mode == "window":
        qt = pl.program_id(2)
        kl = jnp.concatenate([r[0, 0] for r in kwin_refs], axis=0)
        vl = jnp.concatenate([r[0, 0] for r in vwin_refs], axis=0)
        qpos = lax.broadcasted_iota(jnp.int32, (tq, n_win * WINDOW), 0) + qt * tq
        kpos = lax.broadcasted_iota(jnp.int32, (tq, n_win * WINDOW), 1) + (qt * tq - WINDOW)
        valid = (kpos - qpos <= WINDOW) & (qpos - kpos <= WINDOW) & (kpos >= 0) & (kpos < s_len)
        band_bias = jnp.where(valid, 0.0, NEG_INF)
    outs = []
    for g in range(n_g):
        q = q_ref[0, g]
        sc = _dot_nt(q, kc)
        m = jnp.max(sc, axis=-1, keepdims=True)
        if mode in ("global", "window"):
            sl = _dot_nt(q, kl)
            if mode == "window":
                sl = sl + band_bias
            m = jnp.maximum(m, jnp.max(sl, axis=-1, keepdims=True))
        if sink_ref is not None:
            sink = sink_ref[kv * n_g + g]
            m = jnp.maximum(m, sink)
        pc = jnp.exp(sc - m)
        den = jnp.sum(pc, axis=-1, keepdims=True)
        acc = _dot(pc.astype(BF16), vc)
        if mode in ("global", "window"):
            plat = jnp.exp(sl - m)
            den = den + jnp.sum(plat, axis=-1, keepdims=True)
            acc = acc + _dot(plat.astype(BF16), vl)
        if sink_ref is not None:
            den = den + jnp.exp(sink - m)
        outs.append(acc / den)
    o_ref[0] = jnp.concatenate(outs, axis=-1).astype(o_ref.dtype)


def _attention(q, kc, vc, *, mode, kl=None, vl=None, sinks=None, tq):
    bsz, n_h, sq, hd = q.shape
    n_kv = kc.shape[1]
    n_g = n_h // n_kv
    sc_len = kc.shape[2]
    tq = min(tq, sq)
    nq = sq // tq
    in_specs, args = [], []
    if mode in ("window", "ctx_sink"):
        in_specs.append(pl.BlockSpec(memory_space=pltpu.SMEM))
        args.append(sinks)
    in_specs.append(pl.BlockSpec((1, n_g, tq, hd), lambda b, k, t: (b, k, t, 0)))
    args.append(q)
    if mode == "global":
        sl_len = kl.shape[2]
        in_specs += [pl.BlockSpec((1, 1, sl_len, hd), lambda b, k, t: (b, k, 0, 0))] * 2
        args += [kl, vl]
    elif mode == "window":
        per_q = tq // WINDOW
        n_blk = sq // WINDOW
        win = [pl.BlockSpec((1, 1, WINDOW, hd),
                            lambda b, k, t, i=i: (b, k, jnp.clip(t * per_q - 1 + i, 0, n_blk - 1), 0))
               for i in range(per_q + 2)]
        in_specs += win + win
        args += [kl] * len(win) + [vl] * len(win)
    in_specs += [pl.BlockSpec((1, 1, sc_len, hd), lambda b, k, t: (b, k, 0, 0))] * 2
    args += [kc, vc]
    return pl.pallas_call(
        functools.partial(_attn_kernel, n_g=n_g, tq=tq, mode=mode, s_len=sq),
        out_shape=jax.ShapeDtypeStruct((bsz, sq, n_h * hd), BF16),
        grid=(bsz, n_kv, nq),
        in_specs=in_specs,
        out_specs=pl.BlockSpec((1, tq, n_g * hd), lambda b, k, t: (b, t, k)),
        compiler_params=_cparams(("parallel", "parallel", "parallel")),
        name="attn_" + mode,
    )(*args)


def _dft_tables(n):
    k = jnp.arange(n, dtype=jnp.int32)[:, None]
    s = jnp.arange(n, dtype=jnp.int32)[None, :]
    ang = ((k * s) % (2 * n)).astype(F32) * (math.pi / n)
    cos = jnp.cos(ang)
    sin = jnp.where(k == 0, jnp.where(s % 2 == 0, 1.0, -1.0), jnp.sin(ang))
    fwd = jnp.concatenate([cos, sin], axis=0).astype(BF16)
    return fwd, fwd.T


def _filter_kernel(band_ref, w1_ref, b1_ref, fr_ref, w2_ref, b2_ref, w3_ref, dl_ref, fs_ref, fd_ref, nrm_ref, *, n):
    pos_i = lax.broadcasted_iota(jnp.int32, (n, LANES), 0)
    lane = lax.broadcasted_iota(jnp.int32, (n, LANES), 1)
    pos = pos_i.astype(F32)
    t = pos / (n - 1)
    ang = band_ref[...] * pos / n
    z = jnp.where(lane == 0, t,
                  jnp.where(lane <= HY_BANDS, jnp.cos(ang),
                            jnp.where(lane <= 2 * HY_BANDS, -jnp.sin(ang), 0.0)))
    freq = fr_ref[...]
    h = jnp.sin(freq * (_dot_hi(z, w1_ref[...]) + b1_ref[...]))
    h = jnp.sin(freq * (_dot_hi(h, w2_ref[...]) + b2_ref[...]))
    h = _dot_hi(h, w3_ref[...])
    tc = t[:, 0:1]
    win = jnp.exp(-tc * dl_ref[...])
    fwd = h[:, :HY_WIDTH] * win
    bwd = jnp.where(pos_i[:, 0:1] == 0, 0.0, h[:, HY_WIDTH:] * win)
    fs_ref[...] = fwd + bwd
    fd_ref[...] = fwd - bwd
    nrm_ref[...] = jnp.sum(jnp.abs(fwd) + jnp.abs(bwd), axis=0, keepdims=True)


def _hyena_filter(n, band_row, w1p, b1, freq, w2, b2, w3, deltas):
    return pl.pallas_call(
        functools.partial(_filter_kernel, n=n),
        out_shape=[jax.ShapeDtypeStruct((n, HY_WIDTH), F32), jax.ShapeDtypeStruct((n, HY_WIDTH), F32),
                   jax.ShapeDtypeStruct((1, HY_WIDTH), F32)],
        compiler_params=pltpu.CompilerParams(vmem_limit_bytes=VMEM_LIMIT),
        name="hyena_filter",
    )(band_row, w1p, b1, freq, w2, b2, w3, deltas)


def _spectrum_kernel(f_ref, x_ref, o_ref):
    o_ref[...] = _dot(f_ref[...], x_ref[...].astype(BF16))


def _filter_spectrum(fwd_tab, x):
    n2, n = fwd_tab.shape
    w = x.shape[1]
    tc = 256
    return pl.pallas_call(
        _spectrum_kernel,
        out_shape=jax.ShapeDtypeStruct((n2, w), F32),
        grid=(w // tc,),
        in_specs=[_const_spec((n2, n)), pl.BlockSpec((n, tc), lambda j: (0, j))],
        out_specs=pl.BlockSpec((n2, tc), lambda j: (0, j)),
        compiler_params=_cparams(("arbitrary",)),
        name="hyena_filter_spectrum",
    )(fwd_tab, x)


def _short_conv(x, w_ref, b_ref):
    n = x.shape[0]
    row = lax.broadcasted_iota(jnp.int32, x.shape, 0)
    prev = jnp.where(row == 0, 0.0, pltpu.roll(x, 1, axis=0))
    nxt = jnp.where(row == n - 1, 0.0, pltpu.roll(x, n - 1, axis=0))
    return w_ref[0:1, :] * prev + w_ref[1:2, :] * x + w_ref[2:3, :] * nxt + b_ref[...]


def _hyena_fwd_kernel(f_ref, x1_ref, v_ref, w1_ref, b1_ref, wv_ref, bv_ref, kr_ref, ka_ref, kn_ref, nrm_ref,
                      y_ref, u_ref, *, n):
    x1 = _short_conv(x1_ref[0], w1_ref, b1_ref)
    v = _short_conv(v_ref[0], wv_ref, bv_ref)
    ub = (v * x1).astype(BF16)
    u_ref[0] = ub
    spec = _dot(f_ref[...], ub)
    ur = spec[0:n]
    ua = spec[n:2 * n]
    row0 = lax.broadcasted_iota(jnp.int32, ur.shape, 0) == 0
    scale = jnp.where(row0, 1.0 / (2 * n), 1.0 / n) / nrm_ref[...]
    kr = kr_ref[...]
    ka = jnp.where(row0, 0.0, ka_ref[...])
    k4 = jnp.where(row0, kn_ref[...], kr)
    y_ref[0, 0:n] = ((ur * kr - ua * ka) * scale).astype(y_ref.dtype)
    y_ref[0, n:2 * n] = ((ur * ka + ua * k4) * scale).astype(y_ref.dtype)


def _hyena_inv_kernel(ft_ref, y_ref, u_ref, x0_ref, w0_ref, b0_ref, skip_ref, o_ref):
    conv = _dot(ft_ref[...], y_ref[0])
    x0 = _short_conv(x0_ref[0], w0_ref, b0_ref)
    o_ref[0] = (x0 * (conv + skip_ref[...] * u_ref[0].astype(F32))).astype(o_ref.dtype)


def _hyena(uh, tabs, spec, knyq, nrm, short_w, short_b, skip):
    fwd_tab, inv_tab = tabs
    bsz, n, _ = uh.shape
    tc = 256
    nct = HY_WIDTH // tc
    chan = lambda off: (lambda c, b: (b, 0, off + c))
    par = lambda off: (lambda c, b: (0, off + c))
    y, u = pl.pallas_call(
        functools.partial(_hyena_fwd_kernel, n=n),
        out_shape=[jax.ShapeDtypeStruct((bsz, 2 * n, HY_WIDTH), BF16), jax.ShapeDtypeStruct((bsz, n, HY_WIDTH), BF16)],
        grid=(nct, bsz),
        in_specs=[
            _const_spec((2 * n, n)),
            pl.BlockSpec((1, n, tc), chan(nct)), pl.BlockSpec((1, n, tc), chan(2 * nct)),
            pl.BlockSpec((3, tc), par(nct)), pl.BlockSpec((1, tc), par(nct)),
            pl.BlockSpec((3, tc), par(2 * nct)), pl.BlockSpec((1, tc), par(2 * nct)),
            pl.BlockSpec((n, tc), lambda c, b: (0, c)), pl.BlockSpec((n, tc), lambda c, b: (1, nct + c)),
            pl.BlockSpec((1, tc), par(0)), pl.BlockSpec((1, tc), par(0)),
        ],
        out_specs=[pl.BlockSpec((1, 2 * n, tc), lambda c, b: (b, 0, c)), pl.BlockSpec((1, n, tc), lambda c, b: (b, 0, c))],
        compiler_params=_cparams(("arbitrary", "arbitrary")),
        name="hyena_fwd",
    )(fwd_tab, uh, uh, short_w, short_b, short_w, short_b, spec, spec, knyq, nrm)
    return pl.pallas_call(
        _hyena_inv_kernel,
        out_shape=jax.ShapeDtypeStruct((bsz, n, HY_WIDTH), BF16),
        grid=(nct, bsz),
        in_specs=[
            _const_spec((n, 2 * n)),
            pl.BlockSpec((1, 2 * n, tc), lambda c, b: (b, 0, c)), pl.BlockSpec((1, n, tc), lambda c, b: (b, 0, c)),
            pl.BlockSpec((1, n, tc), chan(0)), pl.BlockSpec((3, tc), par(0)), pl.BlockSpec((1, tc), par(0)),
            pl.BlockSpec((1, tc), par(0)),
        ],
        out_specs=pl.BlockSpec((1, n, tc), lambda c, b: (b, 0, c)),
        compiler_params=_cparams(("arbitrary", "arbitrary")),
        name="hyena_inv",
    )(inv_tab, y, u, uh, short_w, short_b, skip)


def _merge_kernel(x_ref, mod_ref, ya_ref, yh_ref, yw_ref, gt_ref, wb_ref, wo_ref, g2_ref, rw_ref, rb_ref,
                  xo_ref, tok_ref, route_ref):
    gt = gt_ref[0]
    m = gt[:, 0:D_MODEL].astype(F32) * _dot(ya_ref[0], wb_ref[0])
    m = m + gt[:, D_MODEL:2 * D_MODEL].astype(F32) * _dot(yh_ref[0], wb_ref[1])
    m = m + gt[:, 2 * D_MODEL:3 * D_MODEL].astype(F32) * _dot(yw_ref[0], wb_ref[2])
    x = x_ref[0] + mod_ref[0, 2:3, :] * _dot(m.astype(BF16), wo_ref[...])
    xo_ref[0] = x
    ms = jnp.mean(x * x, axis=-1, keepdims=True)
    tok = x * lax.rsqrt(ms + EPS) * g2_ref[...]
    tok = tok * (1.0 + mod_ref[0, 4:5, :]) + mod_ref[0, 3:4, :]
    tok_ref[0] = _pack_bf16_pairs(tok)

    lg = _dot_hi(tok, rw_ref[...]) + rb_ref[...]
    lane = lax.broadcasted_iota(jnp.int32, lg.shape, 1).astype(F32)
    big = float(ROUTER_W)

    def first_lane(mask):
        return jnp.min(jnp.where(mask, lane, big), axis=-1, keepdims=True)

    is_g = lane < N_GROUPS
    gmax = jnp.max(jnp.where(is_g, lg, NEG_INF), axis=-1, keepdims=True)
    g_sel = first_lane(jnp.where(is_g, lg, NEG_INF) == gmax)
    p_g = 1.0 / jnp.sum(jnp.where(is_g, jnp.exp(lg - gmax), 0.0), axis=-1, keepdims=True)
    lo = N_GROUPS + g_sel * EXPERTS_PER_GROUP
    in_grp = (lane >= lo) & (lane < lo + EXPERTS_PER_GROUP)
    le = jnp.where(in_grp, lg, NEG_INF)
    v1 = jnp.max(le, axis=-1, keepdims=True)
    i1 = first_lane(le == v1)
    le = jnp.where(lane == i1, NEG_INF, le)
    v2 = jnp.max(le, axis=-1, keepdims=True)
    i2 = first_lane(le == v2)
    e2 = jnp.exp(v2 - v1)
    p1 = p_g / (1.0 + e2)
    p2 = p_g * e2 / (1.0 + e2)
    route_ref[0] = jnp.where(lane == 0, i1 - N_GROUPS, jnp.where(lane == 1, i2 - N_GROUPS,
                             jnp.where(lane == 2, p1, jnp.where(lane == 3, p2, 0.0))))


def _merge(x, mod, mod_row, ya, yh, yw, gates, w_branch, w_out, g2, rw, rb):
    bsz, s, d = x.shape
    tm = min(256, s)
    row = lambda b, t: (b, t, 0)
    return pl.pallas_call(
        _merge_kernel,
        out_shape=[jax.ShapeDtypeStruct((bsz, s, d), F32), jax.ShapeDtypeStruct((bsz, s, d // 2), jnp.uint32),
                   jax.ShapeDtypeStruct((bsz, s, ROUTER_W), F32)],
        grid=(bsz, s // tm),
        in_specs=[
            pl.BlockSpec((1, tm, d), row),
            pl.BlockSpec((1, N_MOD, d), lambda b, t: (mod_row(b), 0, 0)),
            pl.BlockSpec((1, tm, QA_W), row), pl.BlockSpec((1, tm, HY_WIDTH), row), pl.BlockSpec((1, tm, QW_W), row),
            pl.BlockSpec((1, tm, N_BRANCH * d), row),
            _const_spec((N_BRANCH, HY_WIDTH, d)), _const_spec((d, d)), _const_spec((1, d)),
            _const_spec((d, ROUTER_W)), _const_spec((1, ROUTER_W)),
        ],
        out_specs=[pl.BlockSpec((1, tm, d), row), pl.BlockSpec((1, tm, d // 2), row), pl.BlockSpec((1, tm, ROUTER_W), row)],
        compiler_params=_cparams(("parallel", "parallel")),
        name="merge_router",
    )(x, mod, ya, yh, yw, gates, w_branch, w_out, g2, rw, rb)


def _route_plan(experts):
    n, k = experts.shape
    n_tiles = (n * k) // MOE_TM + N_EXPERTS
    flat = experts.reshape(-1)
    onehot = (flat[:, None] == jnp.arange(N_EXPERTS, dtype=jnp.int32)[None, :]).astype(jnp.int32)
    csum = jnp.cumsum(onehot, axis=0)
    count = csum[-1]
    rank = jnp.sum((csum - onehot) * onehot, axis=1)
    size = ((count + MOE_TM - 1) // MOE_TM) * MOE_TM
    end = jnp.cumsum(size)
    start = end - size
    slot = jnp.sum(onehot * start[None, :], axis=1) + rank
    slot = slot.reshape(n, k).T.reshape(-1)
    tile_row = jnp.arange(n_tiles, dtype=jnp.int32) * MOE_TM
    tile_e = jnp.minimum(jnp.sum((tile_row[:, None] >= end[None, :]).astype(jnp.int32), axis=1), N_EXPERTS - 1)
    tile_oh = (tile_e[:, None] == jnp.arange(N_EXPERTS, dtype=jnp.int32)[None, :]).astype(jnp.int32)
    filled = jnp.sum(tile_oh * (start + count)[None, :], axis=1)
    tile_rows = jnp.clip(filled - tile_row, 0, MOE_TM)
    return slot, tile_e, tile_rows, n_tiles


def _sc_mesh():
    info = plsc.get_sparse_core_info()
    mesh = plsc.VectorSubcoreMesh(core_axis_name="c", subcore_axis_name="s")
    return mesh, info.num_cores, info.num_cores * info.num_subcores


def _sc_scatter_rows(rows, slot, n_out):
    mesh, n_cores, n_workers = _sc_mesh()
    n, w = rows.shape
    n_k = slot.shape[0] // n
    per_worker = n // n_workers
    n_chunks = per_worker // SC_CHUNK
    assert per_worker * n_workers == n and n_chunks * SC_CHUNK == per_worker

    @functools.partial(
        pl.kernel, mesh=mesh,
        out_type=jax.ShapeDtypeStruct((n_out, w), rows.dtype),
        scratch_types=[pltpu.VMEM((SC_CHUNK,), jnp.int32), pltpu.VMEM((SC_CHUNK, w), rows.dtype),
                       pltpu.SemaphoreType.DMA],
    )
    def scatter(rows_hbm, slot_hbm, out_hbm, idx_v, rows_v, sem):
        base = (lax.axis_index("s") * n_cores + lax.axis_index("c")) * per_worker

        @pl.loop(0, n_chunks)
        def _(j):
            off = base + j * SC_CHUNK
            pltpu.sync_copy(rows_hbm.at[pl.ds(off, SC_CHUNK)], rows_v)
            for q in range(n_k):
                pltpu.sync_copy(slot_hbm.at[pl.ds(q * n + off, SC_CHUNK)], idx_v)
                pltpu.async_copy(rows_v, out_hbm.at[idx_v], sem).wait()

    return scatter(rows, slot)


def _sc_gather_rows(table, idx):
    mesh, n_cores, n_workers = _sc_mesh()
    m = idx.shape[0]
    w = table.shape[1]
    per_worker = m // n_workers
    n_chunks = per_worker // SC_CHUNK
    assert per_worker * n_workers == m and n_chunks * SC_CHUNK == per_worker

    @functools.partial(
        pl.kernel, mesh=mesh,
        out_type=jax.ShapeDtypeStruct((m, w), table.dtype),
        scratch_types=[pltpu.VMEM((SC_CHUNK,), jnp.int32), pltpu.VMEM((SC_CHUNK, w), table.dtype),
                       pltpu.SemaphoreType.DMA],
    )
    def gather(table_hbm, idx_hbm, out_hbm, idx_v, rows_v, sem):
        base = (lax.axis_index("s") * n_cores + lax.axis_index("c")) * per_worker

        @pl.loop(0, n_chunks)
        def _(j):
            off = base + j * SC_CHUNK
            pltpu.sync_copy(idx_hbm.at[pl.ds(off, SC_CHUNK)], idx_v)
            pltpu.async_copy(table_hbm.at[idx_v], rows_v, sem).wait()
            pltpu.sync_copy(rows_v, out_hbm.at[pl.ds(off, SC_CHUNK)])

    return gather(table, idx)


def _experts_kernel(te_ref, tr_ref, x_ref, w1_ref, w3_ref, w2_ref, o_ref):
    n_rows = tr_ref[pl.program_id(0)]

    @pl.when(n_rows > 0)
    def _():
        xp = x_ref[...]
        row = lax.broadcasted_iota(jnp.int32, xp.shape, 0)
        x = _unpack_bf16_pairs(jnp.where(row < n_rows, xp, jnp.uint32(0))).astype(BF16)
        h1 = _dot(x, w1_ref[0])
        h = (h1 * jax.nn.sigmoid(h1)) * _dot(x, w3_ref[0])
        o_ref[...] = _pack_bf16_pairs(_dot(h.astype(BF16), w2_ref[0]))

    @pl.when(n_rows == 0)
    def _():
        o_ref[...] = jnp.zeros_like(o_ref)


def _experts(x_sorted, tile_e, tile_rows, w1, w3, w2):
    n_slots, wp = x_sorted.shape
    d = 2 * wp
    wspec = lambda shape: pl.BlockSpec(shape, lambda t, te, tr: (te[t], 0, 0))
    return pl.pallas_call(
        _experts_kernel,
        out_shape=jax.ShapeDtypeStruct((n_slots, wp), jnp.uint32),
        grid_spec=pltpu.PrefetchScalarGridSpec(
            num_scalar_prefetch=2,
            grid=(n_slots // MOE_TM,),
            in_specs=[pl.BlockSpec((MOE_TM, wp), lambda t, te, tr: (t, 0)),
                      wspec((1, d, D_EXPERT)), wspec((1, d, D_EXPERT)), wspec((1, D_EXPERT, d))],
            out_specs=pl.BlockSpec((MOE_TM, wp), lambda t, te, tr: (t, 0)),
        ),
        compiler_params=_cparams(("arbitrary",)),
        name="moe_experts",
    )(tile_e, tile_rows, x_sorted, w1, w3, w2)


def _combine_kernel(x_ref, mod_ref, ya_ref, yb_ref, route_ref, o_ref):
    route = route_ref[0]
    y = route[:, 2:3] * _unpack_bf16_pairs(ya_ref[...]) + route[:, 3:4] * _unpack_bf16_pairs(yb_ref[...])
    o_ref[0] = x_ref[0] + mod_ref[0, 5:6, :] * y


def _combine(x, mod, mod_row, y_tok, route, row_off, n_all):
    bsz, s, d = x.shape
    tm = min(512, s)
    nt = s // tm
    off0 = row_off // tm
    off1 = (n_all + row_off) // tm
    row = lambda b, t: (b, t, 0)
    return pl.pallas_call(
        _combine_kernel,
        out_shape=jax.ShapeDtypeStruct((bsz, s, d), F32),
        grid=(bsz, nt),
        in_specs=[
            pl.BlockSpec((1, tm, d), row),
            pl.BlockSpec((1, N_MOD, d), lambda b, t: (mod_row(b), 0, 0)),
            pl.BlockSpec((tm, d // 2), lambda b, t: (off0 + b * nt + t, 0)),
            pl.BlockSpec((tm, d // 2), lambda b, t: (off1 + b * nt + t, 0)),
            pl.BlockSpec((1, tm, ROUTER_W), row),
        ],
        out_specs=pl.BlockSpec((1, tm, d), row),
        compiler_params=_cparams(("parallel", "parallel")),
        name="moe_combine",
    )(x, mod, y_tok, y_tok, route)


def _moe(streams, mod, w1, w3, w2):
    toks = jnp.concatenate([st[2].reshape(-1, st[2].shape[-1]) for st in streams], axis=0)
    experts = jnp.concatenate([st[3][..., 0:2].reshape(-1, 2) for st in streams], axis=0).astype(jnp.int32)
    n_all = toks.shape[0]
    slot, tile_e, tile_rows, n_tiles = _route_plan(experts)
    x_sorted = _sc_scatter_rows(toks, slot, n_tiles * MOE_TM)
    y_sorted = _experts(x_sorted, tile_e, tile_rows, w1, w3, w2)
    y_tok = _sc_gather_rows(y_sorted, slot)
    outs, row_off = [], 0
    for x1, mod_row, tok, route in streams:
        outs.append(_combine(x1, mod, mod_row, y_tok, route, row_off, n_all))
        row_off += tok.shape[0] * tok.shape[1]
    return outs


def _rope_tables(n_tok):
    rows = n_tok // GRID_W
    row_id = jnp.repeat(jnp.arange(rows), GRID_W)
    col_id = jnp.tile(jnp.arange(GRID_W), rows)
    inv_freq = ROPE_THETA ** (-jnp.arange(ROPE_FREQS, dtype=F32) / ROPE_FREQS)
    ang_r = row_id[:, None] * inv_freq
    ang_c = col_id[:, None] * inv_freq
    cos_h = jnp.concatenate([jnp.cos(ang_r), jnp.cos(ang_r), jnp.cos(ang_c), jnp.cos(ang_c)], axis=-1)
    sin_h = jnp.concatenate([-jnp.sin(ang_r), jnp.sin(ang_r), -jnp.sin(ang_c), jnp.sin(ang_c)], axis=-1)
    reps = KA_W // HEAD_DIM
    return jnp.tile(cos_h, (1, reps)), jnp.tile(sin_h, (1, reps))


def kernel(x, c, ctx, c_ctx, ada_w, ada_b, norm1_g, norm2_g, w_in, qk_g, hy_short_w, hy_short_b, hy_f_w1, hy_f_b1,
           hy_f_freq, hy_f_w2, hy_f_b2, hy_f_w3, hy_bias, sinks, w_branch, w_out, router_g_w, router_g_b,
           router_e_w, router_e_b, exp_w1, exp_w3, exp_w2):
    bsz, n_tok, d = x.shape
    n_ctx = ctx.shape[1]
    depth = ada_w.shape[0]
    assert d == D_MODEL and bsz < MOD_ROWS and n_tok % 256 == 0 and n_ctx % 256 == 0

    cc = jnp.zeros((MOD_ROWS, d), F32).at[:bsz].set(c).at[bsz].set(c_ctx)
    mods = _ada_table(cc, ada_w, ada_b)
    lat_row = lambda b: b
    ctx_row = lambda b: bsz

    rope_tabs = _rope_tables(n_tok)
    head_ones = (jnp.arange(QA_W)[:, None] // HEAD_DIM == jnp.arange(QA_W)[None, :] // HEAD_DIM).astype(BF16)
    tabs_lat = _dft_tables(n_tok)
    tabs_ctx = _dft_tables(n_ctx)
    bands = jnp.linspace(1e-4, HY_BANDS - 1, HY_BANDS, dtype=F32)
    band_row = jnp.zeros((1, LANES), F32).at[0, 1:1 + HY_BANDS].set(bands).at[0, 1 + HY_BANDS:1 + 2 * HY_BANDS].set(bands)
    band_row = 2 * math.pi * band_row
    deltas = jnp.abs(jnp.linspace(math.log(HY_TARGET) / HY_SLOW_DECAY, math.log(HY_TARGET) / HY_FAST_DECAY,
                                  HY_WIDTH, dtype=F32))[None, :]

    xc = ctx
    for l in range(depth):
        update_ctx = l < depth - 1
        mod = mods[l]
        w_in_b = w_in[l].astype(BF16)
        gq = [jnp.tile(qk_g[l, 0], A_HEADS)[None], jnp.tile(qk_g[l, 1], A_KV)[None],
              jnp.tile(qk_g[l, 2], W_HEADS)[None], jnp.tile(qk_g[l, 3], W_KV)[None]]
        g1 = norm1_g[l][None]
        qa, ka, va, uh, qw, kw, vw, gts = _in_proj(x, mod, lat_row, g1, w_in_b, gq, head_ones, rope_tabs)
        qa_c, ka_c, va_c, uh_c, qw_c, kw_c, vw_c, gts_c = _in_proj(xc, mod, ctx_row, g1, w_in_b, gq, head_ones, None)

        ya = _attention(qa, ka_c, va_c, mode="global", kl=ka, vl=va, tq=512)
        yw = _attention(qw, kw_c, vw_c, mode="window", kl=kw, vl=vw, sinks=sinks[l], tq=256)

        w1p = jnp.zeros((LANES, HY_HIDDEN), F32).at[:hy_f_w1.shape[1]].set(hy_f_w1[l])
        fargs = (band_row, w1p, hy_f_b1[l][None], hy_f_freq[l][None], hy_f_w2[l], hy_f_b2[l][None], hy_f_w3[l], deltas)
        skip = hy_bias[l][None]
        sb = hy_short_b[l][None]

        def hyena(u_in, n, tabs):
            fsum, fdiff, nrm = _hyena_filter(n, *fargs)
            spec = _filter_spectrum(tabs[0], jnp.concatenate([fsum, fdiff], axis=1))
            return _hyena(u_in, tabs, spec, spec[n:n + 1, :HY_WIDTH], nrm, hy_short_w[l], sb, skip)

        yh = hyena(uh, n_tok, tabs_lat)

        wb = w_branch[l].astype(BF16)
        wo = w_out[l].astype(BF16)
        g2 = norm2_g[l][None]
        rw = jnp.zeros((d, ROUTER_W), F32).at[:, :N_GROUPS].set(router_g_w[l]).at[:, N_GROUPS:N_GROUPS + N_EXPERTS].set(router_e_w[l])
        rb = jnp.zeros((1, ROUTER_W), F32).at[0, :N_GROUPS].set(router_g_b[l]).at[0, N_GROUPS:N_GROUPS + N_EXPERTS].set(router_e_b[l])
        w1 = exp_w1[l].astype(BF16)
        w3 = exp_w3[l].astype(BF16)
        w2 = exp_w2[l].astype(BF16)

        x1, tok, route = _merge(x, mod, lat_row, ya, yh, yw, gts, wb, wo, g2, rw, rb)
        streams = [(x1, lat_row, tok, route)]

        if update_ctx:
            ya_c = _attention(qa_c, ka_c, va_c, mode="ctx", tq=256)
            yw_c = _attention(qw_c, kw_c, vw_c, mode="ctx_sink", sinks=sinks[l], tq=256)
            yh_c = hyena(uh_c, n_ctx, tabs_ctx)
            xc1, tok_c, route_c = _merge(xc, mod, ctx_row, ya_c, yh_c, yw_c, gts_c, wb, wo, g2, rw, rb)
            streams.append((xc1, ctx_row, tok_c, route_c))
        outs = _moe(streams, mod, w1, w3, w2)
        x = outs[0]
        if update_ctx:
            xc = outs[1]
    return x
```

```python
import functools
import math

import jax
import jax.numpy as jnp
from jax import lax
from jax.experimental import pallas as pl
from jax.experimental.pallas import tpu as pltpu
from jax.experimental.pallas import tpu_sc as plsc

F32 = jnp.float32
BF16 = jnp.bfloat16
HIGHEST = lax.Precision.HIGHEST

D_MODEL = 1024
GRID_W = 64
HEAD_DIM = 64
ROPE_FREQS = HEAD_DIM // 4
ROPE_THETA = 10000.0
ATTN_SCALE = HEAD_DIM ** -0.5
LOG2E = math.log2(math.e)
Q_SCALE = ATTN_SCALE * LOG2E
V_AUG = 128
NEG_INF = -1e30
WINDOW = 128
A_HEADS = 8
A_KV = 2
W_HEADS = 8
W_KV = 2
HY_WIDTH = 512
HY_BANDS = 16
HY_HIDDEN = 64
HY_TARGET = 1e-2
HY_FAST_DECAY = 0.3
HY_SLOW_DECAY = 1.5
N_BRANCH = 3
N_GROUPS = 4
EXPERTS_PER_GROUP = 4
N_EXPERTS = N_GROUPS * EXPERTS_PER_GROUP
D_EXPERT = 512
N_MOD = 6
EPS = 1e-6

QA_W = A_HEADS * HEAD_DIM
KA_W = A_KV * HEAD_DIM
QW_W = W_HEADS * HEAD_DIM
KW_W = W_KV * HEAD_DIM
O_QA = 0
O_KA = O_QA + QA_W
O_VA = O_KA + KA_W
O_UH = O_VA + KA_W
O_QW = O_UH + 3 * HY_WIDTH
O_KW = O_QW + QW_W
O_VW = O_KW + KW_W
O_GT = O_VW + KW_W
D_IN = O_GT + N_BRANCH * D_MODEL

MOD_ROWS = 24
LANES = 128
ROUTER_W = LANES
VMEM_LIMIT = 56 * 1024 * 1024
MOE_TM = 1024
SC_CHUNK = 64


def _cparams(sem):
    return pltpu.CompilerParams(dimension_semantics=sem, vmem_limit_bytes=VMEM_LIMIT)


def _const_spec(shape):
    nd = len(shape)
    return pl.BlockSpec(shape, lambda *_: (0,) * nd, pipeline_mode=pl.Buffered(1))


def _dot(a, b):
    return jnp.dot(a, b, preferred_element_type=F32)


def _dot_nt(a, b):
    return lax.dot_general(a, b, (((1,), (1,)), ((), ())), preferred_element_type=F32)


def _dot_hi(a, b):
    return jnp.dot(a, b, precision=HIGHEST, preferred_element_type=F32)


def _pack_bf16_pairs(t):
    half = t.shape[-1] // 2
    hi = lax.bitcast_convert_type(t[:, :half].astype(BF16).astype(F32), jnp.uint32)
    lo = lax.bitcast_convert_type(t[:, half:].astype(BF16).astype(F32), jnp.uint32)
    return hi | (lo >> 16)


def _unpack_bf16_pairs(p):
    hi = lax.bitcast_convert_type(p & jnp.uint32(0xFFFF0000), F32)
    lo = lax.bitcast_convert_type(p << 16, F32)
    return jnp.concatenate([hi, lo], axis=-1)


def _ada_kernel(cc_ref, w_ref, b_ref, o_ref):
    cc = cc_ref[...]
    act = cc * jax.nn.sigmoid(cc)
    o_ref[0] = _dot_hi(act, w_ref[0]) + b_ref[0]


def _ada_table(cc, ada_w, ada_b):
    depth = ada_w.shape[0]
    tn = 1536
    nmod = ada_w.shape[2]
    out = pl.pallas_call(
        _ada_kernel,
        out_shape=jax.ShapeDtypeStruct((depth, MOD_ROWS, nmod), F32),
        grid=(depth, nmod // tn),
        in_specs=[
            pl.BlockSpec((MOD_ROWS, D_MODEL), lambda l, j: (0, 0)),
            pl.BlockSpec((1, D_MODEL, tn), lambda l, j: (l, 0, j)),
            pl.BlockSpec((1, 1, tn), lambda l, j: (l, 0, j)),
        ],
        out_specs=pl.BlockSpec((1, MOD_ROWS, tn), lambda l, j: (l, 0, j)),
        compiler_params=_cparams(("arbitrary", "arbitrary")),
        name="ada_table",
    )(cc, ada_w, ada_b.reshape(depth, 1, nmod))
    return out.reshape(depth, MOD_ROWS, N_MOD, D_MODEL)


def _in_proj_kernel(*refs, rope):
    if rope:
        (x_ref, mod_ref, g1_ref, w_ref, gqa_ref, gka_ref, gqw_ref, gkw_ref, hs_ref, cos_ref, sin_ref,
         qa_ref, ka_ref, va_ref, uh_ref, qw_ref, kw_ref, vw_ref, gt_ref) = refs
    else:
        (x_ref, mod_ref, g1_ref, w_ref, gqa_ref, gka_ref, gqw_ref, gkw_ref, hs_ref,
         qa_ref, ka_ref, va_ref, uh_ref, qw_ref, kw_ref, vw_ref, gt_ref) = refs
        cos_ref = sin_ref = None
    x = x_ref[0]
    ms = jnp.mean(x * x, axis=-1, keepdims=True)
    h = x * lax.rsqrt(ms + EPS) * g1_ref[...]
    h = (h * (1.0 + mod_ref[0, 1:2, :]) + mod_ref[0, 0:1, :]).astype(BF16)

    def proj(lo, width):
        return _dot(h, w_ref[:, lo:lo + width])

    def head_norm_rope(t, g_ref):
        width = t.shape[-1]
        sq = t * t
        sq_hi = sq.astype(BF16)
        sq_lo = (sq - sq_hi.astype(F32)).astype(BF16)
        ones_blk = hs_ref[0:width, 0:width]
        ssum = _dot(sq_hi, ones_blk) + _dot(sq_lo, ones_blk)
        tn = t * lax.rsqrt(ssum * (1.0 / HEAD_DIM) + EPS) * g_ref[...]
        if rope:
            reps = width // cos_ref.shape[-1]
            cos = cos_ref[...]
            sin = sin_ref[...]
            if reps > 1:
                cos = jnp.concatenate([cos] * reps, axis=-1)
                sin = jnp.concatenate([sin] * reps, axis=-1)
            lane = lax.broadcasted_iota(jnp.int32, tn.shape, 1)
            first = (lane & (2 * ROPE_FREQS - 1)) < ROPE_FREQS
            partner = jnp.where(first, pltpu.roll(tn, width - ROPE_FREQS, axis=1), pltpu.roll(tn, ROPE_FREQS, axis=1))
            tn = tn * cos + partner * sin
        return tn

    def store_heads(o_ref, t):
        for hh in range(t.shape[-1] // HEAD_DIM):
            o_ref[0, hh] = t[:, hh * HEAD_DIM:(hh + 1) * HEAD_DIM].astype(o_ref.dtype)

    def store_values(o_ref, t):
        rows = t.shape[0]
        tail = (lax.broadcasted_iota(jnp.int32, (rows, V_AUG - HEAD_DIM), 1) == 0).astype(F32)
        for hh in range(t.shape[-1] // HEAD_DIM):
            o_ref[0, hh] = jnp.concatenate([t[:, hh * HEAD_DIM:(hh + 1) * HEAD_DIM], tail], axis=-1).astype(o_ref.dtype)

    gt_ref[0] = jax.nn.sigmoid(proj(O_GT, N_BRANCH * D_MODEL)).astype(gt_ref.dtype)
    store_heads(qa_ref, head_norm_rope(proj(O_QA, QA_W), gqa_ref) * Q_SCALE)
    store_heads(qw_ref, head_norm_rope(proj(O_QW, QW_W), gqw_ref) * Q_SCALE)
    store_heads(ka_ref, head_norm_rope(proj(O_KA, KA_W), gka_ref))
    store_heads(kw_ref, head_norm_rope(proj(O_KW, KW_W), gkw_ref))
    store_values(va_ref, proj(O_VA, KA_W))
    store_values(vw_ref, proj(O_VW, KW_W))
    uh_ref[0] = proj(O_UH, 3 * HY_WIDTH)


def _in_proj(x, mod, mod_row, g1, w_in, gq, hs, rope_tabs):
    bsz, s, d = x.shape
    tm = min(512, s)
    rope = rope_tabs is not None
    in_specs = [
        pl.BlockSpec((1, tm, d), lambda b, t: (b, t, 0)),
        pl.BlockSpec((1, N_MOD, d), lambda b, t: (mod_row(b), 0, 0)),
        _const_spec((1, d)),
        _const_spec((d, D_IN)),
        _const_spec((1, QA_W)), _const_spec((1, KA_W)), _const_spec((1, QW_W)), _const_spec((1, KW_W)),
        _const_spec((QA_W, QA_W)),
    ]
    args = [x, mod, g1, w_in, gq[0], gq[1], gq[2], gq[3], hs]
    if rope:
        in_specs += [pl.BlockSpec((tm, KA_W), lambda b, t: (t, 0))] * 2
        args += list(rope_tabs)

    def head_out(n_heads, width=HEAD_DIM):
        return (jax.ShapeDtypeStruct((bsz, n_heads, s, width), BF16),
                pl.BlockSpec((1, n_heads, tm, width), lambda b, t: (b, 0, t, 0)))

    def flat_out(width, dtype):
        return (jax.ShapeDtypeStruct((bsz, s, width), dtype), pl.BlockSpec((1, tm, width), lambda b, t: (b, t, 0)))

    outs = [head_out(A_HEADS), head_out(A_KV), head_out(A_KV, V_AUG), flat_out(3 * HY_WIDTH, F32),
            head_out(W_HEADS), head_out(W_KV), head_out(W_KV, V_AUG), flat_out(N_BRANCH * D_MODEL, BF16)]
    return pl.pallas_call(
        functools.partial(_in_proj_kernel, rope=rope),
        out_shape=[o[0] for o in outs],
        grid=(bsz, s // tm),
        in_specs=in_specs,
        out_specs=[o[1] for o in outs],
        compiler_params=_cparams(("parallel", "parallel")),
        name="in_proj",
    )(*args)


def _attn_kernel(*refs, n_g, tq, mode, s_len):
    refs = list(refs)
    sink_ref = refs.pop(0) if mode in ("window", "ctx_sink") else None
    q_ref = refs.pop(0)
    if mode == "global":
        kl_ref, vl_ref = refs.pop(0), refs.pop(0)
    elif mode == "window":
        n_win = tq // WINDOW + 2
        kwin_refs = [refs.pop(0) for _ in range(n_win)]
        vwin_refs = [refs.pop(0) for _ in range(n_win)]
    kc_ref, vc_ref, o_ref = refs
    n_kv = kc_ref.shape[1]
    if mode == "window":
        qt = pl.program_id(1)
        qpos = lax.broadcasted_iota(jnp.int32, (tq, n_win * WINDOW), 0) + qt * tq
        kpos = lax.broadcasted_iota(jnp.int32, (tq, n_win * WINDOW), 1) + (qt * tq - WINDOW)
        valid = (kpos - qpos <= WINDOW) & (qpos - kpos <= WINDOW) & (kpos >= 0) & (kpos < s_len)
        band_bias = jnp.where(valid, 0.0, NEG_INF)

    def attend(q, kv, sink):
        sc = _dot_nt(q, kc_ref[0, kv])
        m = jnp.max(sc, axis=-1, keepdims=True)
        if mode == "global":
            sl = _dot_nt(q, kl_ref[0, kv])
        elif mode == "window":
            kl = jnp.concatenate([r[0, kv] for r in kwin_refs], axis=0)
            sl = _dot_nt(q, kl) + band_bias
        if mode in ("global", "window"):
            m = jnp.maximum(m, jnp.max(sl, axis=-1, keepdims=True))
        if sink is not None:
            m = jnp.maximum(m, sink)
        acc = _dot(jnp.exp2(sc - m).astype(BF16), vc_ref[0, kv])
        if mode in ("global", "window"):
            vl = vl_ref[0, kv] if mode == "global" else jnp.concatenate([r[0, kv] for r in vwin_refs], axis=0)
            acc = acc + _dot(jnp.exp2(sl - m).astype(BF16), vl)
        den = acc[:, HEAD_DIM:HEAD_DIM + 1]
        if sink is not None:
            den = den + jnp.exp2(sink - m)
        return acc[:, :HEAD_DIM] / den

    outs = []
    for h in range(n_kv * n_g):
        sink = None if sink_ref is None else sink_ref[h] * LOG2E
        outs.append(attend(q_ref[0, h], h // n_g, sink))
    o_ref[0] = jnp.concatenate(outs, axis=-1).astype(o_ref.dtype)


def _attention(q, kc, vc, *, mode, kl=None, vl=None, sinks=None, tq):
    bsz, n_h, sq, hd = q.shape
    n_kv = kc.shape[1]
    n_g = n_h // n_kv
    sc_len = kc.shape[2]
    tq = min(tq, sq)
    nq = sq // tq
    in_specs, args = [], []
    if mode in ("window", "ctx_sink"):
        in_specs.append(pl.BlockSpec(memory_space=pltpu.SMEM))
        args.append(sinks)
    in_specs.append(pl.BlockSpec((1, n_h, tq, hd), lambda b, t: (b, 0, t, 0)))
    args.append(q)
    whole = lambda a: pl.BlockSpec((1,) + a.shape[1:], lambda b, t: (b, 0, 0, 0))
    if mode == "global":
        in_specs += [whole(kl), whole(vl)]
        args += [kl, vl]
    elif mode == "window":
        per_q = tq // WINDOW
        n_blk = sq // WINDOW
        win = lambda a: [pl.BlockSpec((1, n_kv, WINDOW, a.shape[-1]),
                                      lambda b, t, i=i: (b, 0, jnp.clip(t * per_q - 1 + i, 0, n_blk - 1), 0))
                         for i in range(per_q + 2)]
        in_specs += win(kl) + win(vl)
        args += [kl] * (per_q + 2) + [vl] * (per_q + 2)
    in_specs += [whole(kc), whole(vc)]
    args += [kc, vc]
    return pl.pallas_call(
        functools.partial(_attn_kernel, n_g=n_g, tq=tq, mode=mode, s_len=sq),
        out_shape=jax.ShapeDtypeStruct((bsz, sq, n_h * hd), BF16),
        grid=(bsz, nq),
        in_specs=in_specs,
        out_specs=pl.BlockSpec((1, tq, n_h * hd), lambda b, t: (b, t, 0)),
        compiler_params=_cparams(("parallel", "parallel")),
        name="attn_" + mode,
    )(*args)


def _dft_tables(n):
    k = jnp.arange(n, dtype=jnp.int32)[:, None]
    s = jnp.arange(n, dtype=jnp.int32)[None, :]
    ang = ((k * s) % (2 * n)).astype(F32) * (math.pi / n)
    cos = jnp.cos(ang)
    sin = jnp.where(k == 0, jnp.where(s % 2 == 0, 1.0, -1.0), jnp.sin(ang))
    fwd = jnp.concatenate([cos, sin], axis=0).astype(BF16)
    return fwd, fwd.T


def _filter_kernel(band_ref, w1_ref, b1_ref, fr_ref, w2_ref, b2_ref, w3_ref, dl_ref, fs_ref, fd_ref, nrm_ref, *, n):
    pos_i = lax.broadcasted_iota(jnp.int32, (n, LANES), 0)
    lane = lax.broadcasted_iota(jnp.int32, (n, LANES), 1)
    pos = pos_i.astype(F32)
    t = pos / (n - 1)
    ang = band_ref[...] * pos / n
    z = jnp.where(lane == 0, t,
                  jnp.where(lane <= HY_BANDS, jnp.cos(ang),
                            jnp.where(lane <= 2 * HY_BANDS, -jnp.sin(ang), 0.0)))
    freq = fr_ref[...]
    h = jnp.sin(freq * (_dot_hi(z, w1_ref[...]) + b1_ref[...]))
    h = jnp.sin(freq * (_dot_hi(h, w2_ref[...]) + b2_ref[...]))
    h = _dot_hi(h, w3_ref[...])
    tc = t[:, 0:1]
    win = jnp.exp(-tc * dl_ref[...])
    fwd = h[:, :HY_WIDTH] * win
    bwd = jnp.where(pos_i[:, 0:1] == 0, 0.0, h[:, HY_WIDTH:] * win)
    fs_ref[...] = fwd + bwd
    fd_ref[...] = fwd - bwd
    nrm_ref[...] = jnp.sum(jnp.abs(fwd) + jnp.abs(bwd), axis=0, keepdims=True)


def _hyena_filter(n, band_row, w1p, b1, freq, w2, b2, w3, deltas):
    return pl.pallas_call(
        functools.partial(_filter_kernel, n=n),
        out_shape=[jax.ShapeDtypeStruct((n, HY_WIDTH), F32), jax.ShapeDtypeStruct((n, HY_WIDTH), F32),
                   jax.ShapeDtypeStruct((1, HY_WIDTH), F32)],
        compiler_params=pltpu.CompilerParams(vmem_limit_bytes=VMEM_LIMIT),
        name="hyena_filter",
    )(band_row, w1p, b1, freq, w2, b2, w3, deltas)


def _spectrum_kernel(f_ref, x_ref, o_ref):
    o_ref[...] = _dot(f_ref[...], x_ref[...].astype(BF16))


def _filter_spectrum(fwd_tab, x):
    n2, n = fwd_tab.shape
    w = x.shape[1]
    tc = 256
    return pl.pallas_call(
        _spectrum_kernel,
        out_shape=jax.ShapeDtypeStruct((n2, w), F32),
        grid=(w // tc,),
        in_specs=[_const_spec((n2, n)), pl.BlockSpec((n, tc), lambda j: (0, j))],
        out_specs=pl.BlockSpec((n2, tc), lambda j: (0, j)),
        compiler_params=_cparams(("arbitrary",)),
        name="hyena_filter_spectrum",
    )(fwd_tab, x)


def _short_conv(x, w_ref, b_ref):
    n = x.shape[0]
    row = lax.broadcasted_iota(jnp.int32, x.shape, 0)
    prev = jnp.where(row == 0, 0.0, pltpu.roll(x, 1, axis=0))
    nxt = jnp.where(row == n - 1, 0.0, pltpu.roll(x, n - 1, axis=0))
    return w_ref[0:1, :] * prev + w_ref[1:2, :] * x + w_ref[2:3, :] * nxt + b_ref[...]


def _hyena_fwd_kernel(f_ref, x1_ref, v_ref, w1_ref, b1_ref, wv_ref, bv_ref, kr_ref, ka_ref, kn_ref, nrm_ref,
                      y_ref, u_ref, *, n):
    x1 = _short_conv(x1_ref[0], w1_ref, b1_ref)
    v = _short_conv(v_ref[0], wv_ref, bv_ref)
    ub = (v * x1).astype(BF16)
    u_ref[0] = ub
    spec = _dot(f_ref[...], ub)
    ur = spec[0:n]
    ua = spec[n:2 * n]
    row0 = lax.broadcasted_iota(jnp.int32, ur.shape, 0) == 0
    scale = jnp.where(row0, 1.0 / (2 * n), 1.0 / n) / nrm_ref[...]
    kr = kr_ref[...]
    ka = jnp.where(row0, 0.0, ka_ref[...])
    k4 = jnp.where(row0, kn_ref[...], kr)
    y_ref[0, 0:n] = ((ur * kr - ua * ka) * scale).astype(y_ref.dtype)
    y_ref[0, n:2 * n] = ((ur * ka + ua * k4) * scale).astype(y_ref.dtype)


def _hyena_inv_kernel(ft_ref, y_ref, u_ref, x0_ref, w0_ref, b0_ref, skip_ref, o_ref):
    conv = _dot(ft_ref[...], y_ref[0])
    x0 = _short_conv(x0_ref[0], w0_ref, b0_ref)
    o_ref[0] = (x0 * (conv + skip_ref[...] * u_ref[0].astype(F32))).astype(o_ref.dtype)


def _hyena(uh, tabs, spec, knyq, nrm, short_w, short_b, skip):
    fwd_tab, inv_tab = tabs
    bsz, n, _ = uh.shape
    tc = 256
    nct = HY_WIDTH // tc
    chan = lambda off: (lambda c, b: (b, 0, off + c))
    par = lambda off: (lambda c, b: (0, off + c))
    y, u = pl.pallas_call(
        functools.partial(_hyena_fwd_kernel, n=n),
        out_shape=[jax.ShapeDtypeStruct((bsz, 2 * n, HY_WIDTH), BF16), jax.ShapeDtypeStruct((bsz, n, HY_WIDTH), BF16)],
        grid=(nct, bsz),
        in_specs=[
            _const_spec((2 * n, n)),
            pl.BlockSpec((1, n, tc), chan(nct)), pl.BlockSpec((1, n, tc), chan(2 * nct)),
            pl.BlockSpec((3, tc), par(nct)), pl.BlockSpec((1, tc), par(nct)),
            pl.BlockSpec((3, tc), par(2 * nct)), pl.BlockSpec((1, tc), par(2 * nct)),
            pl.BlockSpec((n, tc), lambda c, b: (0, c)), pl.BlockSpec((n, tc), lambda c, b: (1, nct + c)),
            pl.BlockSpec((1, tc), par(0)), pl.BlockSpec((1, tc), par(0)),
        ],
        out_specs=[pl.BlockSpec((1, 2 * n, tc), lambda c, b: (b, 0, c)), pl.BlockSpec((1, n, tc), lambda c, b: (b, 0, c))],
        compiler_params=_cparams(("arbitrary", "arbitrary")),
        name="hyena_fwd",
    )(fwd_tab, uh, uh, short_w, short_b, short_w, short_b, spec, spec, knyq, nrm)
    return pl.pallas_call(
        _hyena_inv_kernel,
        out_shape=jax.ShapeDtypeStruct((bsz, n, HY_WIDTH), BF16),
        grid=(nct, bsz),
        in_specs=[
            _const_spec((n, 2 * n)),
            pl.BlockSpec((1, 2 * n, tc), lambda c, b: (b, 0, c)), pl.BlockSpec((1, n, tc), lambda c, b: (b, 0, c)),
            pl.BlockSpec((1, n, tc), chan(0)), pl.BlockSpec((3, tc), par(0)), pl.BlockSpec((1, tc), par(0)),
            pl.BlockSpec((1, tc), par(0)),
        ],
        out_specs=pl.BlockSpec((1, n, tc), lambda c, b: (b, 0, c)),
        compiler_params=_cparams(("arbitrary", "arbitrary")),
        name="hyena_inv",
    )(inv_tab, y, u, uh, short_w, short_b, skip)


def _merge_kernel(x_ref, mod_ref, ya_ref, yh_ref, yw_ref, gt_ref, wb_ref, wo_ref, g2_ref, rw_ref, rb_ref,
                  xo_ref, tok_ref, route_ref):
    gt = gt_ref[0]
    m = gt[:, 0:D_MODEL].astype(F32) * _dot(ya_ref[0], wb_ref[0])
    m = m + gt[:, D_MODEL:2 * D_MODEL].astype(F32) * _dot(yh_ref[0], wb_ref[1])
    m = m + gt[:, 2 * D_MODEL:3 * D_MODEL].astype(F32) * _dot(yw_ref[0], wb_ref[2])
    x = x_ref[0] + mod_ref[0, 2:3, :] * _dot(m.astype(BF16), wo_ref[...])
    xo_ref[0] = x
    ms = jnp.mean(x * x, axis=-1, keepdims=True)
    tok = x * lax.rsqrt(ms + EPS) * g2_ref[...]
    tok = tok * (1.0 + mod_ref[0, 4:5, :]) + mod_ref[0, 3:4, :]
    tok_ref[0] = _pack_bf16_pairs(tok)

    tok_hi = tok.astype(BF16)
    tok_lo = (tok - tok_hi.astype(F32)).astype(BF16)
    lg2 = _dot(tok_hi, rw_ref[...]) + _dot(tok_lo, rw_ref[...])
    lg = lg2[:, :ROUTER_W] + lg2[:, ROUTER_W:] + rb_ref[...]
    lane = lax.broadcasted_iota(jnp.int32, lg.shape, 1).astype(F32)
    big = float(ROUTER_W)

    def first_lane(mask):
        return jnp.min(jnp.where(mask, lane, big), axis=-1, keepdims=True)

    is_g = lane < N_GROUPS
    gmax = jnp.max(jnp.where(is_g, lg, NEG_INF), axis=-1, keepdims=True)
    g_sel = first_lane(jnp.where(is_g, lg, NEG_INF) == gmax)
    p_g = 1.0 / jnp.sum(jnp.where(is_g, jnp.exp(lg - gmax), 0.0), axis=-1, keepdims=True)
    lo = N_GROUPS + g_sel * EXPERTS_PER_GROUP
    in_grp = (lane >= lo) & (lane < lo + EXPERTS_PER_GROUP)
    le = jnp.where(in_grp, lg, NEG_INF)
    v1 = jnp.max(le, axis=-1, keepdims=True)
    i1 = first_lane(le == v1)
    le = jnp.where(lane == i1, NEG_INF, le)
    v2 = jnp.max(le, axis=-1, keepdims=True)
    i2 = first_lane(le == v2)
    e2 = jnp.exp(v2 - v1)
    p1 = p_g / (1.0 + e2)
    p2 = p_g * e2 / (1.0 + e2)
    route_ref[0] = jnp.where(lane == 0, i1 - N_GROUPS, jnp.where(lane == 1, i2 - N_GROUPS,
                             jnp.where(lane == 2, p1, jnp.where(lane == 3, p2, 0.0))))


def _merge(x, mod, mod_row, ya, yh, yw, gates, w_branch, w_out, g2, rw, rb):
    bsz, s, d = x.shape
    tm = min(512, s)
    row = lambda b, t: (b, t, 0)
    return pl.pallas_call(
        _merge_kernel,
        out_shape=[jax.ShapeDtypeStruct((bsz, s, d), F32), jax.ShapeDtypeStruct((bsz, s, d // 2), jnp.uint32),
                   jax.ShapeDtypeStruct((bsz, s, ROUTER_W), F32)],
        grid=(bsz, s // tm),
        in_specs=[
            pl.BlockSpec((1, tm, d), row),
            pl.BlockSpec((1, N_MOD, d), lambda b, t: (mod_row(b), 0, 0)),
            pl.BlockSpec((1, tm, QA_W), row), pl.BlockSpec((1, tm, HY_WIDTH), row), pl.BlockSpec((1, tm, QW_W), row),
            pl.BlockSpec((1, tm, N_BRANCH * d), row),
            _const_spec((N_BRANCH, HY_WIDTH, d)), _const_spec((d, d)), _const_spec((1, d)),
            _const_spec((d, 2 * ROUTER_W)), _const_spec((1, ROUTER_W)),
        ],
        out_specs=[pl.BlockSpec((1, tm, d), row), pl.BlockSpec((1, tm, d // 2), row), pl.BlockSpec((1, tm, ROUTER_W), row)],
        compiler_params=_cparams(("parallel", "parallel")),
        name="merge_router",
    )(x, mod, ya, yh, yw, gates, w_branch, w_out, g2, rw, rb)


def _route_plan(experts):
    n, k = experts.shape
    n_tiles = (n * k) // MOE_TM + N_EXPERTS
    flat = experts.reshape(-1)
    onehot = (flat[:, None] == jnp.arange(N_EXPERTS, dtype=jnp.int32)[None, :]).astype(jnp.int32)
    csum = jnp.cumsum(onehot, axis=0)
    count = csum[-1]
    rank = jnp.sum((csum - onehot) * onehot, axis=1)
    size = ((count + MOE_TM - 1) // MOE_TM) * MOE_TM
    end = jnp.cumsum(size)
    start = end - size
    slot = jnp.sum(onehot * start[None, :], axis=1) + rank
    slot = slot.reshape(n, k).T.reshape(-1)
    tile_row = jnp.arange(n_tiles, dtype=jnp.int32) * MOE_TM
    tile_e = jnp.minimum(jnp.sum((tile_row[:, None] >= end[None, :]).astype(jnp.int32), axis=1), N_EXPERTS - 1)
    tile_oh = (tile_e[:, None] == jnp.arange(N_EXPERTS, dtype=jnp.int32)[None, :]).astype(jnp.int32)
    filled = jnp.sum(tile_oh * (start + count)[None, :], axis=1)
    tile_rows = jnp.clip(filled - tile_row, 0, MOE_TM)
    return slot, tile_e, tile_rows, n_tiles


def _sc_mesh():
    info = plsc.get_sparse_core_info()
    mesh = plsc.VectorSubcoreMesh(core_axis_name="c", subcore_axis_name="s")
    return mesh, info.num_cores, info.num_cores * info.num_subcores


def _sc_scatter_rows(rows, slot, n_out):
    mesh, n_cores, n_workers = _sc_mesh()
    n, w = rows.shape
    n_k = slot.shape[0] // n
    per_worker = n // n_workers
    n_chunks = per_worker // SC_CHUNK
    assert per_worker * n_workers == n and n_chunks * SC_CHUNK == per_worker

    @functools.partial(
        pl.kernel, mesh=mesh,
        out_type=jax.ShapeDtypeStruct((n_out, w), rows.dtype),
        scratch_types=[pltpu.VMEM((SC_CHUNK,), jnp.int32), pltpu.VMEM((SC_CHUNK, w), rows.dtype),
                       pltpu.SemaphoreType.DMA],
    )
    def scatter(rows_hbm, slot_hbm, out_hbm, idx_v, rows_v, sem):
        base = (lax.axis_index("s") * n_cores + lax.axis_index("c")) * per_worker

        @pl.loop(0, n_chunks)
        def _(j):
            off = base + j * SC_CHUNK
            pltpu.sync_copy(rows_hbm.at[pl.ds(off, SC_CHUNK)], rows_v)
            for q in range(n_k):
                pltpu.sync_copy(slot_hbm.at[pl.ds(q * n + off, SC_CHUNK)], idx_v)
                pltpu.async_copy(rows_v, out_hbm.at[idx_v], sem).wait()

    return scatter(rows, slot)


def _sc_gather_rows(table, idx):
    mesh, n_cores, n_workers = _sc_mesh()
    m = idx.shape[0]
    w = table.shape[1]
    per_worker = m // n_workers
    n_chunks = per_worker // SC_CHUNK
    assert per_worker * n_workers == m and n_chunks * SC_CHUNK == per_worker

    @functools.partial(
        pl.kernel, mesh=mesh,
        out_type=jax.ShapeDtypeStruct((m, w), table.dtype),
        scratch_types=[pltpu.VMEM((SC_CHUNK,), jnp.int32), pltpu.VMEM((SC_CHUNK, w), table.dtype),
                       pltpu.SemaphoreType.DMA],
    )
    def gather(table_hbm, idx_hbm, out_hbm, idx_v, rows_v, sem):
        base = (lax.axis_index("s") * n_cores + lax.axis_index("c")) * per_worker

        @pl.loop(0, n_chunks)
        def _(j):
            off = base + j * SC_CHUNK
            pltpu.sync_copy(idx_hbm.at[pl.ds(off, SC_CHUNK)], idx_v)
            pltpu.async_copy(table_hbm.at[idx_v], rows_v, sem).wait()
            pltpu.sync_copy(rows_v, out_hbm.at[pl.ds(off, SC_CHUNK)])

    return gather(table, idx)


def _experts_kernel(te_ref, tr_ref, x_ref, w1_ref, w3_ref, w2_ref, o_ref, w1_b, w3_b, w2_b):
    t = pl.program_id(0)
    n_rows = tr_ref[t]

    @pl.when((t == 0) | (te_ref[t] != te_ref[jnp.maximum(t - 1, 0)]))
    def _():
        w1_b[...] = w1_ref[0, 0].astype(BF16)
        w3_b[...] = w3_ref[0, 0].astype(BF16)
        w2_b[...] = w2_ref[0, 0].astype(BF16)

    @pl.when(n_rows > 0)
    def _():
        xp = x_ref[...]
        row = lax.broadcasted_iota(jnp.int32, xp.shape, 0)
        x = _unpack_bf16_pairs(jnp.where(row < n_rows, xp, jnp.uint32(0))).astype(BF16)
        h1 = _dot(x, w1_b[...])
        h = (h1 * jax.nn.sigmoid(h1)) * _dot(x, w3_b[...])
        o_ref[...] = _pack_bf16_pairs(_dot(h.astype(BF16), w2_b[...]))

    @pl.when(n_rows == 0)
    def _():
        o_ref[...] = jnp.zeros_like(o_ref)


def _experts(x_sorted, tile_e, tile_rows, layer, w1, w3, w2):
    n_slots, wp = x_sorted.shape
    d = 2 * wp
    wspec = lambda shape: pl.BlockSpec(shape, lambda t, te, tr: (layer, te[t], 0, 0))
    return pl.pallas_call(
        _experts_kernel,
        out_shape=jax.ShapeDtypeStruct((n_slots, wp), jnp.uint32),
        grid_spec=pltpu.PrefetchScalarGridSpec(
            num_scalar_prefetch=2,
            grid=(n_slots // MOE_TM,),
            in_specs=[pl.BlockSpec((MOE_TM, wp), lambda t, te, tr: (t, 0)),
                      wspec((1, 1, d, D_EXPERT)), wspec((1, 1, d, D_EXPERT)), wspec((1, 1, D_EXPERT, d))],
            out_specs=pl.BlockSpec((MOE_TM, wp), lambda t, te, tr: (t, 0)),
            scratch_shapes=[pltpu.VMEM((d, D_EXPERT), BF16), pltpu.VMEM((d, D_EXPERT), BF16),
                            pltpu.VMEM((D_EXPERT, d), BF16)],
        ),
        compiler_params=_cparams(("arbitrary",)),
        name="moe_experts",
    )(tile_e, tile_rows, x_sorted, w1, w3, w2)


def _combine_kernel(x_ref, mod_ref, ya_ref, yb_ref, route_ref, o_ref):
    route = route_ref[0]
    y = route[:, 2:3] * _unpack_bf16_pairs(ya_ref[...]) + route[:, 3:4] * _unpack_bf16_pairs(yb_ref[...])
    o_ref[0] = x_ref[0] + mod_ref[0, 5:6, :] * y


def _combine(x, mod, mod_row, y_tok, route, row_off, n_all):
    bsz, s, d = x.shape
    tm = min(512, s)
    nt = s // tm
    off0 = row_off // tm
    off1 = (n_all + row_off) // tm
    row = lambda b, t: (b, t, 0)
    return pl.pallas_call(
        _combine_kernel,
        out_shape=jax.ShapeDtypeStruct((bsz, s, d), F32),
        grid=(bsz, nt),
        in_specs=[
            pl.BlockSpec((1, tm, d), row),
            pl.BlockSpec((1, N_MOD, d), lambda b, t: (mod_row(b), 0, 0)),
            pl.BlockSpec((tm, d // 2), lambda b, t: (off0 + b * nt + t, 0)),
            pl.BlockSpec((tm, d // 2), lambda b, t: (off1 + b * nt + t, 0)),
            pl.BlockSpec((1, tm, ROUTER_W), row),
        ],
        out_specs=pl.BlockSpec((1, tm, d), row),
        compiler_params=_cparams(("parallel", "parallel")),
        name="moe_combine",
    )(x, mod, y_tok, y_tok, route)


def _moe(streams, mod, layer, w1, w3, w2):
    toks = jnp.concatenate([st[2].reshape(-1, st[2].shape[-1]) for st in streams], axis=0)
    experts = jnp.concatenate([st[3][..., 0:2].reshape(-1, 2) for st in streams], axis=0).astype(jnp.int32)
    n_all = toks.shape[0]
    slot, tile_e, tile_rows, n_tiles = _route_plan(experts)
    x_sorted = _sc_scatter_rows(toks, slot, n_tiles * MOE_TM)
    y_sorted = _experts(x_sorted, tile_e, tile_rows, layer, w1, w3, w2)
    y_tok = _sc_gather_rows(y_sorted, slot)
    outs, row_off = [], 0
    for x1, mod_row, tok, route in streams:
        outs.append(_combine(x1, mod, mod_row, y_tok, route, row_off, n_all))
        row_off += tok.shape[0] * tok.shape[1]
    return outs


def _rope_tables(n_tok):
    rows = n_tok // GRID_W
    row_id = jnp.repeat(jnp.arange(rows), GRID_W)
    col_id = jnp.tile(jnp.arange(GRID_W), rows)
    inv_freq = ROPE_THETA ** (-jnp.arange(ROPE_FREQS, dtype=F32) / ROPE_FREQS)
    ang_r = row_id[:, None] * inv_freq
    ang_c = col_id[:, None] * inv_freq
    cos_h = jnp.concatenate([jnp.cos(ang_r), jnp.cos(ang_r), jnp.cos(ang_c), jnp.cos(ang_c)], axis=-1)
    sin_h = jnp.concatenate([-jnp.sin(ang_r), jnp.sin(ang_r), -jnp.sin(ang_c), jnp.sin(ang_c)], axis=-1)
    reps = KA_W // HEAD_DIM
    return jnp.tile(cos_h, (1, reps)), jnp.tile(sin_h, (1, reps))


def kernel(x, c, ctx, c_ctx, ada_w, ada_b, norm1_g, norm2_g, w_in, qk_g, hy_short_w, hy_short_b, hy_f_w1, hy_f_b1,
           hy_f_freq, hy_f_w2, hy_f_b2, hy_f_w3, hy_bias, sinks, w_branch, w_out, router_g_w, router_g_b,
           router_e_w, router_e_b, exp_w1, exp_w3, exp_w2):
    bsz, n_tok, d = x.shape
    n_ctx = ctx.shape[1]
    depth = ada_w.shape[0]
    assert d == D_MODEL and bsz < MOD_ROWS and n_tok % 256 == 0 and n_ctx % 256 == 0

    cc = jnp.zeros((MOD_ROWS, d), F32).at[:bsz].set(c).at[bsz].set(c_ctx)
    mods = _ada_table(cc, ada_w, ada_b)
    lat_row = lambda b: b
    ctx_row = lambda b: bsz

    rope_tabs = _rope_tables(n_tok)
    head_ones = (jnp.arange(QA_W)[:, None] // HEAD_DIM == jnp.arange(QA_W)[None, :] // HEAD_DIM).astype(BF16)
    tabs_lat = _dft_tables(n_tok)
    tabs_ctx = _dft_tables(n_ctx)
    bands = jnp.linspace(1e-4, HY_BANDS - 1, HY_BANDS, dtype=F32)
    band_row = jnp.zeros((1, LANES), F32).at[0, 1:1 + HY_BANDS].set(bands).at[0, 1 + HY_BANDS:1 + 2 * HY_BANDS].set(bands)
    band_row = 2 * math.pi * band_row
    deltas = jnp.abs(jnp.linspace(math.log(HY_TARGET) / HY_SLOW_DECAY, math.log(HY_TARGET) / HY_FAST_DECAY,
                                  HY_WIDTH, dtype=F32))[None, :]

    xc = ctx
    for l in range(depth):
        update_ctx = l < depth - 1
        mod = mods[l]
        w_in_b = w_in[l].astype(BF16)
        gq = [jnp.tile(qk_g[l, 0], A_HEADS)[None], jnp.tile(qk_g[l, 1], A_KV)[None],
              jnp.tile(qk_g[l, 2], W_HEADS)[None], jnp.tile(qk_g[l, 3], W_KV)[None]]
        g1 = norm1_g[l][None]
        qa, ka, va, uh, qw, kw, vw, gts = _in_proj(x, mod, lat_row, g1, w_in_b, gq, head_ones, rope_tabs)
        qa_c, ka_c, va_c, uh_c, qw_c, kw_c, vw_c, gts_c = _in_proj(xc, mod, ctx_row, g1, w_in_b, gq, head_ones, None)

        ya = _attention(qa, ka_c, va_c, mode="global", kl=ka, vl=va, tq=512)
        yw = _attention(qw, kw_c, vw_c, mode="window", kl=kw, vl=vw, sinks=sinks[l], tq=256)

        w1p = jnp.zeros((LANES, HY_HIDDEN), F32).at[:hy_f_w1.shape[1]].set(hy_f_w1[l])
        fargs = (band_row, w1p, hy_f_b1[l][None], hy_f_freq[l][None], hy_f_w2[l], hy_f_b2[l][None], hy_f_w3[l], deltas)
        skip = hy_bias[l][None]
        sb = hy_short_b[l][None]

        def hyena(u_in, n, tabs):
            fsum, fdiff, nrm = _hyena_filter(n, *fargs)
            spec = _filter_spectrum(tabs[0], jnp.concatenate([fsum, fdiff], axis=1))
            return _hyena(u_in, tabs, spec, spec[n:n + 1, :HY_WIDTH], nrm, hy_short_w[l], sb, skip)

        yh = hyena(uh, n_tok, tabs_lat)

        wb = w_branch[l].astype(BF16)
        wo = w_out[l].astype(BF16)
        g2 = norm2_g[l][None]
        rw = jnp.zeros((d, ROUTER_W), F32).at[:, :N_GROUPS].set(router_g_w[l]).at[:, N_GROUPS:N_GROUPS + N_EXPERTS].set(router_e_w[l])
        rb = jnp.zeros((1, ROUTER_W), F32).at[0, :N_GROUPS].set(router_g_b[l]).at[0, N_GROUPS:N_GROUPS + N_EXPERTS].set(router_e_b[l])
        rw_hi = rw.astype(BF16)
        rw = jnp.concatenate([rw_hi, (rw - rw_hi.astype(F32)).astype(BF16)], axis=1)
        w1, w3, w2 = exp_w1, exp_w3, exp_w2

        x1, tok, route = _merge(x, mod, lat_row, ya, yh, yw, gts, wb, wo, g2, rw, rb)
        streams = [(x1, lat_row, tok, route)]

        if update_ctx:
            ya_c = _attention(qa_c, ka_c, va_c, mode="ctx", tq=256)
            yw_c = _attention(qw_c, kw_c, vw_c, mode="ctx_sink", sinks=sinks[l], tq=256)
            yh_c = hyena(uh_c, n_ctx, tabs_ctx)
            xc1, tok_c, route_c = _merge(xc, mod, ctx_row, ya_c, yh_c, yw_c, gts_c, wb, wo, g2, rw, rb)
            streams.append((xc1, ctx_row, tok_c, route_c))
        outs = _moe(streams, mod, l, w1, w3, w2)
        x = outs[0]
        if update_ctx:
            xc = outs[1]
    return x
```

```python
import functools
import math

import jax
import jax.numpy as jnp
from jax import lax
from jax.experimental import pallas as pl
from jax.experimental.pallas import tpu as pltpu
from jax.experimental.pallas import tpu_sc as plsc

F32 = jnp.float32
BF16 = jnp.bfloat16
HIGHEST = lax.Precision.HIGHEST

D_MODEL = 1024
GRID_W = 64
HEAD_DIM = 64
ROPE_FREQS = HEAD_DIM // 4
ROPE_THETA = 10000.0
ATTN_SCALE = HEAD_DIM ** -0.5
LOG2E = math.log2(math.e)
Q_SCALE = ATTN_SCALE * LOG2E
V_AUG = 128
NEG_INF = -1e30
WINDOW = 128
A_HEADS = 8
A_KV = 2
W_HEADS = 8
W_KV = 2
HY_WIDTH = 512
HY_BANDS = 16
HY_HIDDEN = 64
HY_TARGET = 1e-2
HY_FAST_DECAY = 0.3
HY_SLOW_DECAY = 1.5
N_BRANCH = 3
N_GROUPS = 4
EXPERTS_PER_GROUP = 4
N_EXPERTS = N_GROUPS * EXPERTS_PER_GROUP
D_EXPERT = 512
N_MOD = 6
EPS = 1e-6

QA_W = A_HEADS * HEAD_DIM
KA_W = A_KV * HEAD_DIM
QW_W = W_HEADS * HEAD_DIM
KW_W = W_KV * HEAD_DIM
O_QA = 0
O_KA = O_QA + QA_W
O_VA = O_KA + KA_W
O_UH = O_VA + KA_W
O_QW = O_UH + 3 * HY_WIDTH
O_KW = O_QW + QW_W
O_VW = O_KW + KW_W
O_GT = O_VW + KW_W
D_IN = O_GT + N_BRANCH * D_MODEL

MOD_ROWS = 24
LANES = 128
ROUTER_W = LANES
ROUTE_T_ROWS = 8
VMEM_LIMIT = 56 * 1024 * 1024
MOE_TM = 1024
SC_CHUNK = 64


def _cparams(sem):
    return pltpu.CompilerParams(dimension_semantics=sem, vmem_limit_bytes=VMEM_LIMIT)


def _const_spec(shape):
    nd = len(shape)
    return pl.BlockSpec(shape, lambda *_: (0,) * nd, pipeline_mode=pl.Buffered(1))


def _dot(a, b):
    return jnp.dot(a, b, preferred_element_type=F32)


def _dot_nt(a, b):
    return lax.dot_general(a, b, (((1,), (1,)), ((), ())), preferred_element_type=F32)


def _dot_hi(a, b):
    return jnp.dot(a, b, precision=HIGHEST, preferred_element_type=F32)


def _pack_bf16_pairs(t):
    half = t.shape[-1] // 2
    hi = lax.bitcast_convert_type(t[:, :half].astype(BF16).astype(F32), jnp.uint32)
    lo = lax.bitcast_convert_type(t[:, half:].astype(BF16).astype(F32), jnp.uint32)
    return hi | (lo >> 16)


def _unpack_bf16_pairs(p):
    hi = lax.bitcast_convert_type(p & jnp.uint32(0xFFFF0000), F32)
    lo = lax.bitcast_convert_type(p << 16, F32)
    return jnp.concatenate([hi, lo], axis=-1)


def _ada_kernel(cc_ref, w_ref, b_ref, o_ref):
    cc = cc_ref[...]
    act = cc * jax.nn.sigmoid(cc)
    o_ref[0] = _dot_hi(act, w_ref[0]) + b_ref[0]


def _ada_table(cc, ada_w, ada_b):
    depth = ada_w.shape[0]
    tn = 1536
    nmod = ada_w.shape[2]
    out = pl.pallas_call(
        _ada_kernel,
        out_shape=jax.ShapeDtypeStruct((depth, MOD_ROWS, nmod), F32),
        grid=(depth, nmod // tn),
        in_specs=[
            pl.BlockSpec((MOD_ROWS, D_MODEL), lambda l, j: (0, 0)),
            pl.BlockSpec((1, D_MODEL, tn), lambda l, j: (l, 0, j)),
            pl.BlockSpec((1, 1, tn), lambda l, j: (l, 0, j)),
        ],
        out_specs=pl.BlockSpec((1, MOD_ROWS, tn), lambda l, j: (l, 0, j)),
        compiler_params=_cparams(("arbitrary", "arbitrary")),
        name="ada_table",
    )(cc, ada_w, ada_b.reshape(depth, 1, nmod))
    return out.reshape(depth, MOD_ROWS, N_MOD, D_MODEL)


def _in_proj_kernel(*refs, rope):
    if rope:
        (x_ref, mod_ref, g1_ref, w_ref, gqa_ref, gka_ref, gqw_ref, gkw_ref, hs_ref, cos_ref, sin_ref,
         qa_ref, ka_ref, va_ref, uh_ref, qw_ref, kw_ref, vw_ref, gt_ref) = refs
    else:
        (x_ref, mod_ref, g1_ref, w_ref, gqa_ref, gka_ref, gqw_ref, gkw_ref, hs_ref,
         qa_ref, ka_ref, va_ref, uh_ref, qw_ref, kw_ref, vw_ref, gt_ref) = refs
        cos_ref = sin_ref = None
    x = x_ref[0]
    ms = jnp.mean(x * x, axis=-1, keepdims=True)
    h = x * lax.rsqrt(ms + EPS) * g1_ref[...]
    h = (h * (1.0 + mod_ref[0, 1:2, :]) + mod_ref[0, 0:1, :]).astype(BF16)

    def proj(lo, width):
        return _dot(h, w_ref[:, lo:lo + width])

    def head_norm_rope(t, g_ref):
        width = t.shape[-1]
        ssum = _dot((t * t).astype(BF16), hs_ref[0:width, 0:width])
        tn = t * lax.rsqrt(ssum * (1.0 / HEAD_DIM) + EPS) * g_ref[...]
        if rope:
            reps = width // cos_ref.shape[-1]
            cos = cos_ref[...]
            sin = sin_ref[...]
            if reps > 1:
                cos = jnp.concatenate([cos] * reps, axis=-1)
                sin = jnp.concatenate([sin] * reps, axis=-1)
            lane = lax.broadcasted_iota(jnp.int32, tn.shape, 1)
            first = (lane & (2 * ROPE_FREQS - 1)) < ROPE_FREQS
            partner = jnp.where(first, pltpu.roll(tn, width - ROPE_FREQS, axis=1), pltpu.roll(tn, ROPE_FREQS, axis=1))
            tn = tn * cos + partner * sin
        return tn

    def store_heads(o_ref, t):
        for hh in range(t.shape[-1] // HEAD_DIM):
            o_ref[0, hh] = t[:, hh * HEAD_DIM:(hh + 1) * HEAD_DIM].astype(o_ref.dtype)

    def store_values(o_ref, t):
        rows = t.shape[0]
        tail = (lax.broadcasted_iota(jnp.int32, (rows, V_AUG - HEAD_DIM), 1) == 0).astype(F32)
        for hh in range(t.shape[-1] // HEAD_DIM):
            o_ref[0, hh] = jnp.concatenate([t[:, hh * HEAD_DIM:(hh + 1) * HEAD_DIM], tail], axis=-1).astype(o_ref.dtype)

    gt_ref[0] = jax.nn.sigmoid(proj(O_GT, N_BRANCH * D_MODEL)).astype(gt_ref.dtype)
    store_heads(qa_ref, head_norm_rope(proj(O_QA, QA_W), gqa_ref) * Q_SCALE)
    store_heads(qw_ref, head_norm_rope(proj(O_QW, QW_W), gqw_ref) * Q_SCALE)
    store_heads(ka_ref, head_norm_rope(proj(O_KA, KA_W), gka_ref))
    store_heads(kw_ref, head_norm_rope(proj(O_KW, KW_W), gkw_ref))
    store_values(va_ref, proj(O_VA, KA_W))
    store_values(vw_ref, proj(O_VW, KW_W))
    uh_ref[0] = proj(O_UH, 3 * HY_WIDTH)


def _in_proj(x, mod, mod_row, g1, w_in, gq, hs, rope_tabs):
    bsz, s, d = x.shape
    tm = min(512, s)
    rope = rope_tabs is not None
    in_specs = [
        pl.BlockSpec((1, tm, d), lambda b, t: (b, t, 0)),
        pl.BlockSpec((1, N_MOD, d), lambda b, t: (mod_row(b), 0, 0)),
        _const_spec((1, d)),
        _const_spec((d, D_IN)),
        _const_spec((1, QA_W)), _const_spec((1, KA_W)), _const_spec((1, QW_W)), _const_spec((1, KW_W)),
        _const_spec((QA_W, QA_W)),
    ]
    args = [x, mod, g1, w_in, gq[0], gq[1], gq[2], gq[3], hs]
    if rope:
        in_specs += [pl.BlockSpec((tm, KA_W), lambda b, t: (t, 0))] * 2
        args += list(rope_tabs)

    def head_out(n_heads, width=HEAD_DIM):
        return (jax.ShapeDtypeStruct((bsz, n_heads, s, width), BF16),
                pl.BlockSpec((1, n_heads, tm, width), lambda b, t: (b, 0, t, 0)))

    def flat_out(width, dtype):
        return (jax.ShapeDtypeStruct((bsz, s, width), dtype), pl.BlockSpec((1, tm, width), lambda b, t: (b, t, 0)))

    outs = [head_out(A_HEADS), head_out(A_KV), head_out(A_KV, V_AUG), flat_out(3 * HY_WIDTH, F32),
            head_out(W_HEADS), head_out(W_KV), head_out(W_KV, V_AUG), flat_out(N_BRANCH * D_MODEL, BF16)]
    return pl.pallas_call(
        functools.partial(_in_proj_kernel, rope=rope),
        out_shape=[o[0] for o in outs],
        grid=(bsz, s // tm),
        in_specs=in_specs,
        out_specs=[o[1] for o in outs],
        compiler_params=_cparams(("parallel", "parallel")),
        name="in_proj",
    )(*args)


def _attn_kernel(*refs, n_g, tq, mode, s_len):
    refs = list(refs)
    sink_ref = refs.pop(0) if mode in ("window", "ctx_sink") else None
    q_ref = refs.pop(0)
    if mode == "global":
        kl_ref, vl_ref = refs.pop(0), refs.pop(0)
    elif mode == "window":
        n_win = tq // WINDOW + 2
        kwin_refs = [refs.pop(0) for _ in range(n_win)]
        vwin_refs = [refs.pop(0) for _ in range(n_win)]
    kc_ref, vc_ref, o_ref = refs
    n_kv = kc_ref.shape[1]
    if mode == "window":
        qt = pl.program_id(1)
        qpos = lax.broadcasted_iota(jnp.int32, (tq, n_win * WINDOW), 0) + qt * tq
        kpos = lax.broadcasted_iota(jnp.int32, (tq, n_win * WINDOW), 1) + (qt * tq - WINDOW)
        valid = (kpos - qpos <= WINDOW) & (qpos - kpos <= WINDOW) & (kpos >= 0) & (kpos < s_len)
        band_bias = jnp.where(valid, 0.0, NEG_INF)

    def attend(q, kv, sink):
        sc = _dot_nt(q, kc_ref[0, kv])
        m = jnp.max(sc, axis=-1, keepdims=True)
        if mode == "global":
            sl = _dot_nt(q, kl_ref[0, kv])
        elif mode == "window":
            kl = jnp.concatenate([r[0, kv] for r in kwin_refs], axis=0)
            sl = _dot_nt(q, kl) + band_bias
        if mode in ("global", "window"):
            m = jnp.maximum(m, jnp.max(sl, axis=-1, keepdims=True))
        if sink is not None:
            m = jnp.maximum(m, sink)
        acc = _dot(jnp.exp2(sc - m).astype(BF16), vc_ref[0, kv])
        if mode in ("global", "window"):
            vl = vl_ref[0, kv] if mode == "global" else jnp.concatenate([r[0, kv] for r in vwin_refs], axis=0)
            acc = acc + _dot(jnp.exp2(sl - m).astype(BF16), vl)
        den = acc[:, HEAD_DIM:HEAD_DIM + 1]
        if sink is not None:
            den = den + jnp.exp2(sink - m)
        return acc[:, :HEAD_DIM] / den

    outs = []
    for h in range(n_kv * n_g):
        sink = None if sink_ref is None else sink_ref[h] * LOG2E
        outs.append(attend(q_ref[0, h], h // n_g, sink))
    o_ref[0] = jnp.concatenate(outs, axis=-1).astype(o_ref.dtype)


def _attention(q, kc, vc, *, mode, kl=None, vl=None, sinks=None, tq):
    bsz, n_h, sq, hd = q.shape
    n_kv = kc.shape[1]
    n_g = n_h // n_kv
    sc_len = kc.shape[2]
    tq = min(tq, sq)
    nq = sq // tq
    in_specs, args = [], []
    if mode in ("window", "ctx_sink"):
        in_specs.append(pl.BlockSpec(memory_space=pltpu.SMEM))
        args.append(sinks)
    in_specs.append(pl.BlockSpec((1, n_h, tq, hd), lambda b, t: (b, 0, t, 0)))
    args.append(q)
    whole = lambda a: pl.BlockSpec((1,) + a.shape[1:], lambda b, t: (b, 0, 0, 0))
    if mode == "global":
        in_specs += [whole(kl), whole(vl)]
        args += [kl, vl]
    elif mode == "window":
        per_q = tq // WINDOW
        n_blk = sq // WINDOW
        win = lambda a: [pl.BlockSpec((1, n_kv, WINDOW, a.shape[-1]),
                                      lambda b, t, i=i: (b, 0, jnp.clip(t * per_q - 1 + i, 0, n_blk - 1), 0))
                         for i in range(per_q + 2)]
        in_specs += win(kl) + win(vl)
        args += [kl] * (per_q + 2) + [vl] * (per_q + 2)
    in_specs += [whole(kc), whole(vc)]
    args += [kc, vc]
    return pl.pallas_call(
        functools.partial(_attn_kernel, n_g=n_g, tq=tq, mode=mode, s_len=sq),
        out_shape=jax.ShapeDtypeStruct((bsz, sq, n_h * hd), BF16),
        grid=(bsz, nq),
        in_specs=in_specs,
        out_specs=pl.BlockSpec((1, tq, n_h * hd), lambda b, t: (b, t, 0)),
        compiler_params=_cparams(("parallel", "parallel")),
        name="attn_" + mode,
    )(*args)


def _dft_tables(n):
    k = jnp.arange(n, dtype=jnp.int32)[:, None]
    s = jnp.arange(n, dtype=jnp.int32)[None, :]
    ang = ((k * s) % (2 * n)).astype(F32) * (math.pi / n)
    cos = jnp.cos(ang)
    sin = jnp.where(k == 0, jnp.where(s % 2 == 0, 1.0, -1.0), jnp.sin(ang))
    fwd = jnp.concatenate([cos, sin], axis=0).astype(BF16)
    return fwd, fwd.T


def _filter_kernel(band_ref, w1_ref, b1_ref, fr_ref, w2_ref, b2_ref, w3_ref, dl_ref, fs_ref, fd_ref, nrm_ref, *, n):
    pos_i = lax.broadcasted_iota(jnp.int32, (n, LANES), 0)
    lane = lax.broadcasted_iota(jnp.int32, (n, LANES), 1)
    pos = pos_i.astype(F32)
    t = pos / (n - 1)
    ang = band_ref[...] * pos / n
    z = jnp.where(lane == 0, t,
                  jnp.where(lane <= HY_BANDS, jnp.cos(ang),
                            jnp.where(lane <= 2 * HY_BANDS, -jnp.sin(ang), 0.0)))
    freq = fr_ref[...]
    h = jnp.sin(freq * (_dot_hi(z, w1_ref[...]) + b1_ref[...]))
    h = jnp.sin(freq * (_dot_hi(h, w2_ref[...]) + b2_ref[...]))
    h = _dot_hi(h, w3_ref[...])
    tc = t[:, 0:1]
    win = jnp.exp(-tc * dl_ref[...])
    fwd = h[:, :HY_WIDTH] * win
    bwd = jnp.where(pos_i[:, 0:1] == 0, 0.0, h[:, HY_WIDTH:] * win)
    fs_ref[...] = fwd + bwd
    fd_ref[...] = fwd - bwd
    nrm_ref[...] = jnp.sum(jnp.abs(fwd) + jnp.abs(bwd), axis=0, keepdims=True)


def _hyena_filter(n, band_row, w1p, b1, freq, w2, b2, w3, deltas):
    return pl.pallas_call(
        functools.partial(_filter_kernel, n=n),
        out_shape=[jax.ShapeDtypeStruct((n, HY_WIDTH), F32), jax.ShapeDtypeStruct((n, HY_WIDTH), F32),
                   jax.ShapeDtypeStruct((1, HY_WIDTH), F32)],
        compiler_params=pltpu.CompilerParams(vmem_limit_bytes=VMEM_LIMIT),
        name="hyena_filter",
    )(band_row, w1p, b1, freq, w2, b2, w3, deltas)


def _spectrum_kernel(f_ref, x_ref, o_ref):
    o_ref[...] = _dot(f_ref[...], x_ref[...].astype(BF16))


def _filter_spectrum(fwd_tab, x):
    n2, n = fwd_tab.shape
    w = x.shape[1]
    tc = 256
    return pl.pallas_call(
        _spectrum_kernel,
        out_shape=jax.ShapeDtypeStruct((n2, w), F32),
        grid=(w // tc,),
        in_specs=[_const_spec((n2, n)), pl.BlockSpec((n, tc), lambda j: (0, j))],
        out_specs=pl.BlockSpec((n2, tc), lambda j: (0, j)),
        compiler_params=_cparams(("arbitrary",)),
        name="hyena_filter_spectrum",
    )(fwd_tab, x)


def _short_conv(x, w_ref, b_ref):
    n = x.shape[0]
    row = lax.broadcasted_iota(jnp.int32, x.shape, 0)
    prev = jnp.where(row == 0, 0.0, pltpu.roll(x, 1, axis=0))
    nxt = jnp.where(row == n - 1, 0.0, pltpu.roll(x, n - 1, axis=0))
    return w_ref[0:1, :] * prev + w_ref[1:2, :] * x + w_ref[2:3, :] * nxt + b_ref[...]


def _hyena_fwd_kernel(f_ref, x1_ref, v_ref, w1_ref, b1_ref, wv_ref, bv_ref, kr_ref, ka_ref, kn_ref, nrm_ref,
                      y_ref, u_ref, *, n):
    x1 = _short_conv(x1_ref[0], w1_ref, b1_ref)
    v = _short_conv(v_ref[0], wv_ref, bv_ref)
    ub = (v * x1).astype(BF16)
    u_ref[0] = ub
    spec = _dot(f_ref[...], ub)
    ur = spec[0:n]
    ua = spec[n:2 * n]
    row0 = lax.broadcasted_iota(jnp.int32, ur.shape, 0) == 0
    scale = jnp.where(row0, 1.0 / (2 * n), 1.0 / n) / nrm_ref[...]
    kr = kr_ref[...]
    ka = jnp.where(row0, 0.0, ka_ref[...])
    k4 = jnp.where(row0, kn_ref[...], kr)
    y_ref[0, 0:n] = ((ur * kr - ua * ka) * scale).astype(y_ref.dtype)
    y_ref[0, n:2 * n] = ((ur * ka + ua * k4) * scale).astype(y_ref.dtype)


def _hyena_inv_kernel(ft_ref, y_ref, u_ref, x0_ref, w0_ref, b0_ref, skip_ref, o_ref):
    conv = _dot(ft_ref[...], y_ref[0])
    x0 = _short_conv(x0_ref[0], w0_ref, b0_ref)
    o_ref[0] = (x0 * (conv + skip_ref[...] * u_ref[0].astype(F32))).astype(o_ref.dtype)


def _hyena(uh, tabs, spec, knyq, nrm, short_w, short_b, skip):
    fwd_tab, inv_tab = tabs
    bsz, n, _ = uh.shape
    tc = 256
    nct = HY_WIDTH // tc
    chan = lambda off: (lambda c, b: (b, 0, off + c))
    par = lambda off: (lambda c, b: (0, off + c))
    y, u = pl.pallas_call(
        functools.partial(_hyena_fwd_kernel, n=n),
        out_shape=[jax.ShapeDtypeStruct((bsz, 2 * n, HY_WIDTH), BF16), jax.ShapeDtypeStruct((bsz, n, HY_WIDTH), BF16)],
        grid=(nct, bsz),
        in_specs=[
            _const_spec((2 * n, n)),
            pl.BlockSpec((1, n, tc), chan(nct)), pl.BlockSpec((1, n, tc), chan(2 * nct)),
            pl.BlockSpec((3, tc), par(nct)), pl.BlockSpec((1, tc), par(nct)),
            pl.BlockSpec((3, tc), par(2 * nct)), pl.BlockSpec((1, tc), par(2 * nct)),
            pl.BlockSpec((n, tc), lambda c, b: (0, c)), pl.BlockSpec((n, tc), lambda c, b: (1, nct + c)),
            pl.BlockSpec((1, tc), par(0)), pl.BlockSpec((1, tc), par(0)),
        ],
        out_specs=[pl.BlockSpec((1, 2 * n, tc), lambda c, b: (b, 0, c)), pl.BlockSpec((1, n, tc), lambda c, b: (b, 0, c))],
        compiler_params=_cparams(("arbitrary", "arbitrary")),
        name="hyena_fwd",
    )(fwd_tab, uh, uh, short_w, short_b, short_w, short_b, spec, spec, knyq, nrm)
    return pl.pallas_call(
        _hyena_inv_kernel,
        out_shape=jax.ShapeDtypeStruct((bsz, n, HY_WIDTH), BF16),
        grid=(nct, bsz),
        in_specs=[
            _const_spec((n, 2 * n)),
            pl.BlockSpec((1, 2 * n, tc), lambda c, b: (b, 0, c)), pl.BlockSpec((1, n, tc), lambda c, b: (b, 0, c)),
            pl.BlockSpec((1, n, tc), chan(0)), pl.BlockSpec((3, tc), par(0)), pl.BlockSpec((1, tc), par(0)),
            pl.BlockSpec((1, tc), par(0)),
        ],
        out_specs=pl.BlockSpec((1, n, tc), lambda c, b: (b, 0, c)),
        compiler_params=_cparams(("arbitrary", "arbitrary")),
        name="hyena_inv",
    )(inv_tab, y, u, uh, short_w, short_b, skip)


def _merge_kernel(x_ref, mod_ref, ya_ref, yh_ref, yw_ref, gt_ref, wb_ref, wo_ref, g2_ref, rw_ref, rb_ref,
                  *rest):
    xo_ref, tok_ref, route_ref, rt_ref = rest[-4:]
    tm = x_ref.shape[1]
    n_part = 2 if tm % 512 == 0 else 1
    for part in range(n_part):
        rows = pl.ds(part * (tm // n_part), tm // n_part)
        _merge_rows(rows, x_ref, mod_ref, ya_ref, yh_ref, yw_ref, gt_ref, wb_ref, wo_ref, g2_ref, rw_ref, rb_ref,
                    xo_ref, tok_ref, route_ref, rt_ref)


def _merge_rows(rows, x_ref, mod_ref, ya_ref, yh_ref, yw_ref, gt_ref, wb_ref, wo_ref, g2_ref, rw_ref, rb_ref,
                xo_ref, tok_ref, route_ref, rt_ref):
    gt = gt_ref[0, rows, :]
    m = gt[:, 0:D_MODEL].astype(F32) * _dot(ya_ref[0, rows, :], wb_ref[0])
    m = m + gt[:, D_MODEL:2 * D_MODEL].astype(F32) * _dot(yh_ref[0, rows, :], wb_ref[1])
    m = m + gt[:, 2 * D_MODEL:3 * D_MODEL].astype(F32) * _dot(yw_ref[0, rows, :], wb_ref[2])
    x = x_ref[0, rows, :] + mod_ref[0, 2:3, :] * _dot(m.astype(BF16), wo_ref[...])
    xo_ref[0, rows, :] = x
    ms = jnp.mean(x * x, axis=-1, keepdims=True)
    tok = x * lax.rsqrt(ms + EPS) * g2_ref[...]
    tok = tok * (1.0 + mod_ref[0, 4:5, :]) + mod_ref[0, 3:4, :]
    tok_ref[rows, :] = _pack_bf16_pairs(tok)

    tok_hi = tok.astype(BF16)
    tok_lo = (tok - tok_hi.astype(F32)).astype(BF16)
    lg2 = _dot(tok_hi, rw_ref[...]) + _dot(tok_lo, rw_ref[...])
    lg = lg2[:, :ROUTER_W] + lg2[:, ROUTER_W:] + rb_ref[...]
    lane = lax.broadcasted_iota(jnp.int32, lg.shape, 1).astype(F32)
    big = float(ROUTER_W)

    def first_lane(mask):
        return jnp.min(jnp.where(mask, lane, big), axis=-1, keepdims=True)

    is_g = lane < N_GROUPS
    gmax = jnp.max(jnp.where(is_g, lg, NEG_INF), axis=-1, keepdims=True)
    g_sel = first_lane(jnp.where(is_g, lg, NEG_INF) == gmax)
    p_g = 1.0 / jnp.sum(jnp.where(is_g, jnp.exp(lg - gmax), 0.0), axis=-1, keepdims=True)
    lo = N_GROUPS + g_sel * EXPERTS_PER_GROUP
    in_grp = (lane >= lo) & (lane < lo + EXPERTS_PER_GROUP)
    le = jnp.where(in_grp, lg, NEG_INF)
    v1 = jnp.max(le, axis=-1, keepdims=True)
    i1 = first_lane(le == v1)
    le = jnp.where(lane == i1, NEG_INF, le)
    v2 = jnp.max(le, axis=-1, keepdims=True)
    i2 = first_lane(le == v2)
    e2 = jnp.exp(v2 - v1)
    p1 = p_g / (1.0 + e2)
    p2 = p_g * e2 / (1.0 + e2)
    route = jnp.where(lane == 0, i1 - N_GROUPS, jnp.where(lane == 1, i2 - N_GROUPS,
                      jnp.where(lane == 2, p1, jnp.where(lane == 3, p2, 0.0))))
    route_ref[0, rows, :] = route
    rt_ref[0, :, rows] = route.T[0:ROUTE_T_ROWS, :]


def _merge(x, mod, mod_row, ya, yh, yw, gates, w_branch, w_out, g2, rw, rb, tok_buf, row_off, n_all):
    bsz, s, d = x.shape
    tm = min(512, s)
    nt = s // tm
    assert row_off % tm == 0
    row = lambda b, t: (b, t, 0)
    in_specs = [
        pl.BlockSpec((1, tm, d), row),
        pl.BlockSpec((1, N_MOD, d), lambda b, t: (mod_row(b), 0, 0)),
        pl.BlockSpec((1, tm, QA_W), row), pl.BlockSpec((1, tm, HY_WIDTH), row), pl.BlockSpec((1, tm, QW_W), row),
        pl.BlockSpec((1, tm, N_BRANCH * d), row),
        _const_spec((N_BRANCH, HY_WIDTH, d)), _const_spec((d, d)), _const_spec((1, d)),
        _const_spec((d, 2 * ROUTER_W)), _const_spec((1, ROUTER_W)),
    ]
    args = [x, mod, ya, yh, yw, gates, w_branch, w_out, g2, rw, rb]
    aliases = {}
    if tok_buf is not None:
        aliases = {len(args): 1}
        in_specs.append(pl.BlockSpec(memory_space=pl.ANY))
        args.append(tok_buf)
    return pl.pallas_call(
        _merge_kernel,
        out_shape=[jax.ShapeDtypeStruct((bsz, s, d), F32), jax.ShapeDtypeStruct((n_all, d // 2), jnp.uint32),
                   jax.ShapeDtypeStruct((bsz, s, ROUTER_W), F32), jax.ShapeDtypeStruct((bsz, ROUTE_T_ROWS, s), F32)],
        grid=(bsz, nt),
        in_specs=in_specs,
        out_specs=[pl.BlockSpec((1, tm, d), row),
                   pl.BlockSpec((tm, d // 2), lambda b, t: (row_off // tm + b * nt + t, 0)),
                   pl.BlockSpec((1, tm, ROUTER_W), row),
                   pl.BlockSpec((1, ROUTE_T_ROWS, tm), lambda b, t: (b, 0, t))],
        input_output_aliases=aliases,
        compiler_params=_cparams(("parallel", "parallel")),
        name="merge_router",
    )(*args)


def _route_plan(flat):
    n_tiles = -(-flat.shape[0] // MOE_TM) + N_EXPERTS
    onehot = (flat[:, None] == jnp.arange(N_EXPERTS, dtype=jnp.int32)[None, :]).astype(jnp.int32)
    csum = jnp.cumsum(onehot, axis=0)
    count = csum[-1]
    rank = jnp.sum((csum - onehot) * onehot, axis=1)
    size = ((count + MOE_TM - 1) // MOE_TM) * MOE_TM
    end = jnp.cumsum(size)
    start = end - size
    slot = jnp.sum(onehot * start[None, :], axis=1) + rank
    tile_row = jnp.arange(n_tiles, dtype=jnp.int32) * MOE_TM
    tile_e = jnp.minimum(jnp.sum((tile_row[:, None] >= end[None, :]).astype(jnp.int32), axis=1), N_EXPERTS - 1)
    tile_oh = (tile_e[:, None] == jnp.arange(N_EXPERTS, dtype=jnp.int32)[None, :]).astype(jnp.int32)
    filled = jnp.sum(tile_oh * (start + count)[None, :], axis=1)
    tile_rows = jnp.clip(filled - tile_row, 0, MOE_TM)
    return slot, tile_e, tile_rows, n_tiles


def _sc_mesh():
    info = plsc.get_sparse_core_info()
    mesh = plsc.VectorSubcoreMesh(core_axis_name="c", subcore_axis_name="s")
    return mesh, info.num_cores, info.num_cores * info.num_subcores


def _sc_scatter_rows(rows, slot, n_out):
    mesh, n_cores, n_workers = _sc_mesh()
    n, w = rows.shape
    n_k = slot.shape[0] // n
    per_worker = n // n_workers
    n_chunks = per_worker // SC_CHUNK
    assert per_worker * n_workers == n and n_chunks * SC_CHUNK == per_worker

    @functools.partial(
        pl.kernel, mesh=mesh,
        out_type=jax.ShapeDtypeStruct((n_out, w), rows.dtype),
        scratch_types=[pltpu.VMEM((SC_CHUNK,), jnp.int32), pltpu.VMEM((SC_CHUNK, w), rows.dtype),
                       pltpu.SemaphoreType.DMA],
    )
    def scatter(rows_hbm, slot_hbm, out_hbm, idx_v, rows_v, sem):
        base = (lax.axis_index("s") * n_cores + lax.axis_index("c")) * per_worker

        @pl.loop(0, n_chunks)
        def _(j):
            off = base + j * SC_CHUNK
            pltpu.sync_copy(rows_hbm.at[pl.ds(off, SC_CHUNK)], rows_v)
            for q in range(n_k):
                pltpu.sync_copy(slot_hbm.at[pl.ds(q * n + off, SC_CHUNK)], idx_v)
                pltpu.async_copy(rows_v, out_hbm.at[idx_v], sem).wait()

    return scatter(rows, slot)


def _sc_gather_rows(table, idx):
    mesh, n_cores, n_workers = _sc_mesh()
    m = idx.shape[0]
    w = table.shape[1]
    per_worker = m // n_workers
    n_chunks = per_worker // SC_CHUNK
    assert per_worker * n_workers == m and n_chunks * SC_CHUNK == per_worker

    @functools.partial(
        pl.kernel, mesh=mesh,
        out_type=jax.ShapeDtypeStruct((m, w), table.dtype),
        scratch_types=[pltpu.VMEM((SC_CHUNK,), jnp.int32), pltpu.VMEM((SC_CHUNK, w), table.dtype),
                       pltpu.SemaphoreType.DMA],
    )
    def gather(table_hbm, idx_hbm, out_hbm, idx_v, rows_v, sem):
        base = (lax.axis_index("s") * n_cores + lax.axis_index("c")) * per_worker

        @pl.loop(0, n_chunks)
        def _(j):
            off = base + j * SC_CHUNK
            pltpu.sync_copy(idx_hbm.at[pl.ds(off, SC_CHUNK)], idx_v)
            pltpu.async_copy(table_hbm.at[idx_v], rows_v, sem).wait()
            pltpu.sync_copy(rows_v, out_hbm.at[pl.ds(off, SC_CHUNK)])

    return gather(table, idx)


def _experts_kernel(te_ref, tr_ref, x_ref, w1_ref, w3_ref, w2_ref, o_ref, w1_b, w3_b, w2_b):
    t = pl.program_id(0)
    n_rows = tr_ref[t]

    @pl.when((t == 0) | (te_ref[t] != te_ref[jnp.maximum(t - 1, 0)]))
    def _():
        w1_b[...] = w1_ref[0, 0].astype(BF16)
        w3_b[...] = w3_ref[0, 0].astype(BF16)
        w2_b[...] = w2_ref[0, 0].astype(BF16)

    @pl.when(n_rows > 0)
    def _():
        xp = x_ref[...]
        row = lax.broadcasted_iota(jnp.int32, xp.shape, 0)
        x = _unpack_bf16_pairs(jnp.where(row < n_rows, xp, jnp.uint32(0))).astype(BF16)
        h1 = _dot(x, w1_b[...])
        h = (h1 * jax.nn.sigmoid(h1)) * _dot(x, w3_b[...])
        o_ref[...] = _pack_bf16_pairs(_dot(h.astype(BF16), w2_b[...]))

    @pl.when(n_rows == 0)
    def _():
        o_ref[...] = jnp.zeros_like(o_ref)


def _experts(x_sorted, tile_e, tile_rows, layer, w1, w3, w2):
    n_slots, wp = x_sorted.shape
    d = 2 * wp
    wspec = lambda shape: pl.BlockSpec(shape, lambda t, te, tr: (layer, te[t], 0, 0))
    return pl.pallas_call(
        _experts_kernel,
        out_shape=jax.ShapeDtypeStruct((n_slots, wp), jnp.uint32),
        grid_spec=pltpu.PrefetchScalarGridSpec(
            num_scalar_prefetch=2,
            grid=(n_slots // MOE_TM,),
            in_specs=[pl.BlockSpec((MOE_TM, wp), lambda t, te, tr: (t, 0)),
                      wspec((1, 1, d, D_EXPERT)), wspec((1, 1, d, D_EXPERT)), wspec((1, 1, D_EXPERT, d))],
            out_specs=pl.BlockSpec((MOE_TM, wp), lambda t, te, tr: (t, 0)),
            scratch_shapes=[pltpu.VMEM((d, D_EXPERT), BF16), pltpu.VMEM((d, D_EXPERT), BF16),
                            pltpu.VMEM((D_EXPERT, d), BF16)],
        ),
        compiler_params=_cparams(("arbitrary",)),
        name="moe_experts",
    )(tile_e, tile_rows, x_sorted, w1, w3, w2)


def _combine_kernel(x_ref, mod_ref, ya_ref, yb_ref, route_ref, o_ref):
    route = route_ref[0]
    y = route[:, 2:3] * _unpack_bf16_pairs(ya_ref[...]) + route[:, 3:4] * _unpack_bf16_pairs(yb_ref[...])
    o_ref[0] = x_ref[0] + mod_ref[0, 5:6, :] * y


def _combine(x, mod, mod_row, y_tok, route, row_off, n_all):
    bsz, s, d = x.shape
    tm = min(512, s)
    nt = s // tm
    assert row_off % tm == 0 and n_all % tm == 0
    off0 = row_off // tm
    off1 = (n_all + row_off) // tm
    row = lambda b, t: (b, t, 0)
    return pl.pallas_call(
        _combine_kernel,
        out_shape=jax.ShapeDtypeStruct((bsz, s, d), F32),
        grid=(bsz, nt),
        in_specs=[
            pl.BlockSpec((1, tm, d), row),
            pl.BlockSpec((1, N_MOD, d), lambda b, t: (mod_row(b), 0, 0)),
            pl.BlockSpec((tm, d // 2), lambda b, t: (off0 + b * nt + t, 0)),
            pl.BlockSpec((tm, d // 2), lambda b, t: (off1 + b * nt + t, 0)),
            pl.BlockSpec((1, tm, ROUTER_W), row),
        ],
        out_specs=pl.BlockSpec((1, tm, d), row),
        compiler_params=_cparams(("parallel", "parallel")),
        name="moe_combine",
    )(x, mod, y_tok, y_tok, route)


def _moe(streams, toks, mod, layer, w1, w3, w2):
    n_all = toks.shape[0]
    flat = jnp.concatenate([st[3][:, k, :].reshape(-1) for k in range(2) for st in streams]).astype(jnp.int32)
    slot, tile_e, tile_rows, n_tiles = _route_plan(flat)
    x_sorted = _sc_scatter_rows(toks, slot, n_tiles * MOE_TM)
    y_sorted = _experts(x_sorted, tile_e, tile_rows, layer, w1, w3, w2)
    y_tok = _sc_gather_rows(y_sorted, slot)
    outs, row_off = [], 0
    for x1, mod_row, route, _ in streams:
        outs.append(_combine(x1, mod, mod_row, y_tok, route, row_off, n_all))
        row_off += x1.shape[0] * x1.shape[1]
    return outs


def _rope_tables(n_tok):
    rows = n_tok // GRID_W
    row_id = jnp.repeat(jnp.arange(rows), GRID_W)
    col_id = jnp.tile(jnp.arange(GRID_W), rows)
    inv_freq = ROPE_THETA ** (-jnp.arange(ROPE_FREQS, dtype=F32) / ROPE_FREQS)
    ang_r = row_id[:, None] * inv_freq
    ang_c = col_id[:, None] * inv_freq
    cos_h = jnp.concatenate([jnp.cos(ang_r), jnp.cos(ang_r), jnp.cos(ang_c), jnp.cos(ang_c)], axis=-1)
    sin_h = jnp.concatenate([-jnp.sin(ang_r), jnp.sin(ang_r), -jnp.sin(ang_c), jnp.sin(ang_c)], axis=-1)
    reps = KA_W // HEAD_DIM
    return jnp.tile(cos_h, (1, reps)), jnp.tile(sin_h, (1, reps))


def kernel(x, c, ctx, c_ctx, ada_w, ada_b, norm1_g, norm2_g, w_in, qk_g, hy_short_w, hy_short_b, hy_f_w1, hy_f_b1,
           hy_f_freq, hy_f_w2, hy_f_b2, hy_f_w3, hy_bias, sinks, w_branch, w_out, router_g_w, router_g_b,
           router_e_w, router_e_b, exp_w1, exp_w3, exp_w2):
    bsz, n_tok, d = x.shape
    n_ctx = ctx.shape[1]
    depth = ada_w.shape[0]
    assert d == D_MODEL and bsz < MOD_ROWS and n_tok % 256 == 0 and n_ctx % 256 == 0

    cc = jnp.zeros((MOD_ROWS, d), F32).at[:bsz].set(c).at[bsz].set(c_ctx)
    mods = _ada_table(cc, ada_w, ada_b)
    lat_row = lambda b: b
    ctx_row = lambda b: bsz

    rope_tabs = _rope_tables(n_tok)
    head_ones = (jnp.arange(QA_W)[:, None] // HEAD_DIM == jnp.arange(QA_W)[None, :] // HEAD_DIM).astype(BF16)
    tabs_lat = _dft_tables(n_tok)
    tabs_ctx = _dft_tables(n_ctx)
    bands = jnp.linspace(1e-4, HY_BANDS - 1, HY_BANDS, dtype=F32)
    band_row = jnp.zeros((1, LANES), F32).at[0, 1:1 + HY_BANDS].set(bands).at[0, 1 + HY_BANDS:1 + 2 * HY_BANDS].set(bands)
    band_row = 2 * math.pi * band_row
    deltas = jnp.abs(jnp.linspace(math.log(HY_TARGET) / HY_SLOW_DECAY, math.log(HY_TARGET) / HY_FAST_DECAY,
                                  HY_WIDTH, dtype=F32))[None, :]

    xc = ctx
    for l in range(depth):
        update_ctx = l < depth - 1
        mod = mods[l]
        w_in_b = w_in[l].astype(BF16)
        gq = [jnp.tile(qk_g[l, 0], A_HEADS)[None], jnp.tile(qk_g[l, 1], A_KV)[None],
              jnp.tile(qk_g[l, 2], W_HEADS)[None], jnp.tile(qk_g[l, 3], W_KV)[None]]
        g1 = norm1_g[l][None]
        qa, ka, va, uh, qw, kw, vw, gts = _in_proj(x, mod, lat_row, g1, w_in_b, gq, head_ones, rope_tabs)
        qa_c, ka_c, va_c, uh_c, qw_c, kw_c, vw_c, gts_c = _in_proj(xc, mod, ctx_row, g1, w_in_b, gq, head_ones, None)

        ya = _attention(qa, ka_c, va_c, mode="global", kl=ka, vl=va, tq=1024)
        yw = _attention(qw, kw_c, vw_c, mode="window", kl=kw, vl=vw, sinks=sinks[l], tq=512)

        w1p = jnp.zeros((LANES, HY_HIDDEN), F32).at[:hy_f_w1.shape[1]].set(hy_f_w1[l])
        fargs = (band_row, w1p, hy_f_b1[l][None], hy_f_freq[l][None], hy_f_w2[l], hy_f_b2[l][None], hy_f_w3[l], deltas)
        skip = hy_bias[l][None]
        sb = hy_short_b[l][None]

        def hyena(u_in, n, tabs):
            fsum, fdiff, nrm = _hyena_filter(n, *fargs)
            spec = _filter_spectrum(tabs[0], jnp.concatenate([fsum, fdiff], axis=1))
            return _hyena(u_in, tabs, spec, spec[n:n + 1, :HY_WIDTH], nrm, hy_short_w[l], sb, skip)

        yh = hyena(uh, n_tok, tabs_lat)

        wb = w_branch[l].astype(BF16)
        wo = w_out[l].astype(BF16)
        g2 = norm2_g[l][None]
        rw = jnp.zeros((d, ROUTER_W), F32).at[:, :N_GROUPS].set(router_g_w[l]).at[:, N_GROUPS:N_GROUPS + N_EXPERTS].set(router_e_w[l])
        rb = jnp.zeros((1, ROUTER_W), F32).at[0, :N_GROUPS].set(router_g_b[l]).at[0, N_GROUPS:N_GROUPS + N_EXPERTS].set(router_e_b[l])
        rw_hi = rw.astype(BF16)
        rw = jnp.concatenate([rw_hi, (rw - rw_hi.astype(F32)).astype(BF16)], axis=1)
        w1, w3, w2 = exp_w1, exp_w3, exp_w2

        n_all = bsz * (n_tok + n_ctx) if update_ctx else bsz * n_tok
        x1, toks, route, route_t = _merge(x, mod, lat_row, ya, yh, yw, gts, wb, wo, g2, rw, rb, None, 0, n_all)
        streams = [(x1, lat_row, route, route_t)]

        if update_ctx:
            ya_c = _attention(qa_c, ka_c, va_c, mode="ctx", tq=256)
            yw_c = _attention(qw_c, kw_c, vw_c, mode="ctx_sink", sinks=sinks[l], tq=256)
            yh_c = hyena(uh_c, n_ctx, tabs_ctx)
            xc1, toks, route_c, route_ct = _merge(xc, mod, ctx_row, ya_c, yh_c, yw_c, gts_c, wb, wo, g2, rw, rb,
                                                  toks, bsz * n_tok, n_all)
            streams.append((xc1, ctx_row, route_c, route_ct))
        outs = _moe(streams, toks, mod, l, w1, w3, w2)
        x = outs[0]
        if update_ctx:
            xc = outs[1]
    return x
```

```python
import functools
import math

import jax
import jax.numpy as jnp
from jax import lax
from jax.experimental import pallas as pl
from jax.experimental.pallas import tpu as pltpu
from jax.experimental.pallas import tpu_sc as plsc

F32 = jnp.float32
BF16 = jnp.bfloat16
HIGHEST = lax.Precision.HIGHEST

D_MODEL = 1024
GRID_W = 64
HEAD_DIM = 64
ROPE_FREQS = HEAD_DIM // 4
ROPE_THETA = 10000.0
ATTN_SCALE = HEAD_DIM ** -0.5
LOG2E = math.log2(math.e)
Q_SCALE = ATTN_SCALE * LOG2E
V_AUG = 128
NEG_INF = -1e30
WINDOW = 128
A_HEADS = 8
A_KV = 2
W_HEADS = 8
W_KV = 2
HY_WIDTH = 512
HY_BANDS = 16
HY_HIDDEN = 64
HY_TARGET = 1e-2
HY_FAST_DECAY = 0.3
HY_SLOW_DECAY = 1.5
N_BRANCH = 3
N_GROUPS = 4
EXPERTS_PER_GROUP = 4
N_EXPERTS = N_GROUPS * EXPERTS_PER_GROUP
D_EXPERT = 512
N_MOD = 6
EPS = 1e-6

QA_W = A_HEADS * HEAD_DIM
KA_W = A_KV * HEAD_DIM
QW_W = W_HEADS * HEAD_DIM
KW_W = W_KV * HEAD_DIM
O_QA = 0
O_KA = O_QA + QA_W
O_VA = O_KA + KA_W
O_UH = O_VA + KA_W
O_QW = O_UH + 3 * HY_WIDTH
O_KW = O_QW + QW_W
O_VW = O_KW + KW_W
O_GT = O_VW + KW_W
D_IN = O_GT + N_BRANCH * D_MODEL

MOD_ROWS = 24
LANES = 128
ROUTER_W = LANES
ROUTE_T_ROWS = 8
VMEM_LIMIT = 56 * 1024 * 1024
MOE_TM = 1024
SC_CHUNK = 64


def _cparams(sem):
    return pltpu.CompilerParams(dimension_semantics=sem, vmem_limit_bytes=VMEM_LIMIT)


def _const_spec(shape):
    nd = len(shape)
    return pl.BlockSpec(shape, lambda *_: (0,) * nd, pipeline_mode=pl.Buffered(1))


def _dot(a, b):
    return jnp.dot(a, b, preferred_element_type=F32)


def _dot_nt(a, b):
    return lax.dot_general(a, b, (((1,), (1,)), ((), ())), preferred_element_type=F32)


def _dot_hi(a, b):
    return jnp.dot(a, b, precision=HIGHEST, preferred_element_type=F32)


def _pack_bf16_pairs(t):
    half = t.shape[-1] // 2
    hi = lax.bitcast_convert_type(t[:, :half].astype(BF16).astype(F32), jnp.uint32)
    lo = lax.bitcast_convert_type(t[:, half:].astype(BF16).astype(F32), jnp.uint32)
    return hi | (lo >> 16)


def _unpack_bf16_pairs(p):
    hi = lax.bitcast_convert_type(p & jnp.uint32(0xFFFF0000), F32)
    lo = lax.bitcast_convert_type(p << 16, F32)
    return jnp.concatenate([hi, lo], axis=-1)


def _ada_kernel(cc_ref, w_ref, b_ref, o_ref):
    cc = cc_ref[...]
    act = cc * jax.nn.sigmoid(cc)
    o_ref[0] = _dot_hi(act, w_ref[0]) + b_ref[0]


def _ada_table(cc, ada_w, ada_b):
    depth = ada_w.shape[0]
    tn = 1536
    nmod = ada_w.shape[2]
    out = pl.pallas_call(
        _ada_kernel,
        out_shape=jax.ShapeDtypeStruct((depth, MOD_ROWS, nmod), F32),
        grid=(depth, nmod // tn),
        in_specs=[
            pl.BlockSpec((MOD_ROWS, D_MODEL), lambda l, j: (0, 0)),
            pl.BlockSpec((1, D_MODEL, tn), lambda l, j: (l, 0, j)),
            pl.BlockSpec((1, 1, tn), lambda l, j: (l, 0, j)),
        ],
        out_specs=pl.BlockSpec((1, MOD_ROWS, tn), lambda l, j: (l, 0, j)),
        compiler_params=_cparams(("arbitrary", "arbitrary")),
        name="ada_table",
    )(cc, ada_w, ada_b.reshape(depth, 1, nmod))
    return out.reshape(depth, MOD_ROWS, N_MOD, D_MODEL)


def _in_proj_kernel(*refs, rope):
    if rope:
        (x_ref, mod_ref, g1_ref, w_ref, gqa_ref, gka_ref, gqw_ref, gkw_ref, hs_ref, cos_ref, sin_ref,
         qa_ref, ka_ref, va_ref, uh_ref, qw_ref, kw_ref, vw_ref, gt_ref) = refs
    else:
        (x_ref, mod_ref, g1_ref, w_ref, gqa_ref, gka_ref, gqw_ref, gkw_ref, hs_ref,
         qa_ref, ka_ref, va_ref, uh_ref, qw_ref, kw_ref, vw_ref, gt_ref) = refs
        cos_ref = sin_ref = None
    x = x_ref[0]
    ms = jnp.mean(x * x, axis=-1, keepdims=True)
    h = x * lax.rsqrt(ms + EPS) * g1_ref[...]
    h = (h * (1.0 + mod_ref[0, 1:2, :]) + mod_ref[0, 0:1, :]).astype(BF16)

    def proj(lo, width):
        return _dot(h, w_ref[:, lo:lo + width])

    def head_norm_rope(t, g_ref):
        width = t.shape[-1]
        ssum = _dot((t * t).astype(BF16), hs_ref[0:width, 0:width])
        tn = t * lax.rsqrt(ssum * (1.0 / HEAD_DIM) + EPS) * g_ref[...]
        if rope:
            reps = width // cos_ref.shape[-1]
            cos = cos_ref[...]
            sin = sin_ref[...]
            if reps > 1:
                cos = jnp.concatenate([cos] * reps, axis=-1)
                sin = jnp.concatenate([sin] * reps, axis=-1)
            lane = lax.broadcasted_iota(jnp.int32, tn.shape, 1)
            first = (lane & (2 * ROPE_FREQS - 1)) < ROPE_FREQS
            partner = jnp.where(first, pltpu.roll(tn, width - ROPE_FREQS, axis=1), pltpu.roll(tn, ROPE_FREQS, axis=1))
            tn = tn * cos + partner * sin
        return tn

    def store_heads(o_ref, t):
        for hh in range(t.shape[-1] // HEAD_DIM):
            o_ref[0, hh] = t[:, hh * HEAD_DIM:(hh + 1) * HEAD_DIM].astype(o_ref.dtype)

    def store_values(o_ref, t):
        rows = t.shape[0]
        tail = (lax.broadcasted_iota(jnp.int32, (rows, V_AUG - HEAD_DIM), 1) == 0).astype(F32)
        for hh in range(t.shape[-1] // HEAD_DIM):
            o_ref[0, hh] = jnp.concatenate([t[:, hh * HEAD_DIM:(hh + 1) * HEAD_DIM], tail], axis=-1).astype(o_ref.dtype)

    gt_ref[0] = jax.nn.sigmoid(proj(O_GT, N_BRANCH * D_MODEL)).astype(gt_ref.dtype)
    store_heads(qa_ref, head_norm_rope(proj(O_QA, QA_W), gqa_ref) * Q_SCALE)
    store_heads(qw_ref, head_norm_rope(proj(O_QW, QW_W), gqw_ref) * Q_SCALE)
    store_heads(ka_ref, head_norm_rope(proj(O_KA, KA_W), gka_ref))
    store_heads(kw_ref, head_norm_rope(proj(O_KW, KW_W), gkw_ref))
    store_values(va_ref, proj(O_VA, KA_W))
    store_values(vw_ref, proj(O_VW, KW_W))
    uh_ref[0] = proj(O_UH, 3 * HY_WIDTH)


def _in_proj(x, mod, mod_row, g1, w_in, gq, hs, rope_tabs):
    bsz, s, d = x.shape
    tm = min(512, s)
    rope = rope_tabs is not None
    in_specs = [
        pl.BlockSpec((1, tm, d), lambda b, t: (b, t, 0)),
        pl.BlockSpec((1, N_MOD, d), lambda b, t: (mod_row(b), 0, 0)),
        _const_spec((1, d)),
        _const_spec((d, D_IN)),
        _const_spec((1, QA_W)), _const_spec((1, KA_W)), _const_spec((1, QW_W)), _const_spec((1, KW_W)),
        _const_spec((QA_W, QA_W)),
    ]
    args = [x, mod, g1, w_in, gq[0], gq[1], gq[2], gq[3], hs]
    if rope:
        in_specs += [pl.BlockSpec((tm, KA_W), lambda b, t: (t, 0))] * 2
        args += list(rope_tabs)

    def head_out(n_heads, width=HEAD_DIM):
        return (jax.ShapeDtypeStruct((bsz, n_heads, s, width), BF16),
                pl.BlockSpec((1, n_heads, tm, width), lambda b, t: (b, 0, t, 0)))

    def flat_out(width, dtype):
        return (jax.ShapeDtypeStruct((bsz, s, width), dtype), pl.BlockSpec((1, tm, width), lambda b, t: (b, t, 0)))

    outs = [head_out(A_HEADS), head_out(A_KV), head_out(A_KV, V_AUG), flat_out(3 * HY_WIDTH, F32),
            head_out(W_HEADS), head_out(W_KV), head_out(W_KV, V_AUG), flat_out(N_BRANCH * D_MODEL, BF16)]
    return pl.pallas_call(
        functools.partial(_in_proj_kernel, rope=rope),
        out_shape=[o[0] for o in outs],
        grid=(bsz, s // tm),
        in_specs=in_specs,
        out_specs=[o[1] for o in outs],
        compiler_params=_cparams(("parallel", "parallel")),
        name="in_proj",
    )(*args)


def _attn_kernel(*refs, n_g, tq, mode, s_len):
    refs = list(refs)
    sink_ref = refs.pop(0) if mode in ("window", "ctx_sink") else None
    q_ref = refs.pop(0)
    if mode == "global":
        kl_ref, vl_ref = refs.pop(0), refs.pop(0)
    elif mode == "window":
        n_win = tq // WINDOW + 2
        kwin_refs = [refs.pop(0) for _ in range(n_win)]
        vwin_refs = [refs.pop(0) for _ in range(n_win)]
    kc_ref, vc_ref, o_ref = refs
    n_kv = kc_ref.shape[1]
    if mode == "window":
        qt = pl.program_id(1)
        qpos = lax.broadcasted_iota(jnp.int32, (tq, n_win * WINDOW), 0) + qt * tq
        kpos = lax.broadcasted_iota(jnp.int32, (tq, n_win * WINDOW), 1) + (qt * tq - WINDOW)
        valid = (kpos - qpos <= WINDOW) & (qpos - kpos <= WINDOW) & (kpos >= 0) & (kpos < s_len)
        band_bias = jnp.where(valid, 0.0, NEG_INF)

    def attend(q, kv, sink):
        sc = _dot_nt(q, kc_ref[0, kv])
        m = jnp.max(sc, axis=-1, keepdims=True)
        if mode == "global":
            sl = _dot_nt(q, kl_ref[0, kv])
        elif mode == "window":
            kl = jnp.concatenate([r[0, kv] for r in kwin_refs], axis=0)
            sl = _dot_nt(q, kl) + band_bias
        if mode in ("global", "window"):
            m = jnp.maximum(m, jnp.max(sl, axis=-1, keepdims=True))
        if sink is not None:
            m = jnp.maximum(m, sink)
        acc = _dot(jnp.exp2(sc - m).astype(BF16), vc_ref[0, kv])
        if mode in ("global", "window"):
            vl = vl_ref[0, kv] if mode == "global" else jnp.concatenate([r[0, kv] for r in vwin_refs], axis=0)
            acc = acc + _dot(jnp.exp2(sl - m).astype(BF16), vl)
        den = acc[:, HEAD_DIM:HEAD_DIM + 1]
        if sink is not None:
            den = den + jnp.exp2(sink - m)
        return acc[:, :HEAD_DIM] / den

    outs = []
    for h in range(n_kv * n_g):
        sink = None if sink_ref is None else sink_ref[h] * LOG2E
        outs.append(attend(q_ref[0, h], h // n_g, sink))
    o_ref[0] = jnp.concatenate(outs, axis=-1).astype(o_ref.dtype)


def _attention(q, kc, vc, *, mode, kl=None, vl=None, sinks=None, tq):
    bsz, n_h, sq, hd = q.shape
    n_kv = kc.shape[1]
    n_g = n_h // n_kv
    sc_len = kc.shape[2]
    tq = min(tq, sq)
    nq = sq // tq
    in_specs, args = [], []
    if mode in ("window", "ctx_sink"):
        in_specs.append(pl.BlockSpec(memory_space=pltpu.SMEM))
        args.append(sinks)
    in_specs.append(pl.BlockSpec((1, n_h, tq, hd), lambda b, t: (b, 0, t, 0)))
    args.append(q)
    whole = lambda a: pl.BlockSpec((1,) + a.shape[1:], lambda b, t: (b, 0, 0, 0))
    if mode == "global":
        in_specs += [whole(kl), whole(vl)]
        args += [kl, vl]
    elif mode == "window":
        per_q = tq // WINDOW
        n_blk = sq // WINDOW
        win = lambda a: [pl.BlockSpec((1, n_kv, WINDOW, a.shape[-1]),
                                      lambda b, t, i=i: (b, 0, jnp.clip(t * per_q - 1 + i, 0, n_blk - 1), 0))
                         for i in range(per_q + 2)]
        in_specs += win(kl) + win(vl)
        args += [kl] * (per_q + 2) + [vl] * (per_q + 2)
    in_specs += [whole(kc), whole(vc)]
    args += [kc, vc]
    return pl.pallas_call(
        functools.partial(_attn_kernel, n_g=n_g, tq=tq, mode=mode, s_len=sq),
        out_shape=jax.ShapeDtypeStruct((bsz, sq, n_h * hd), BF16),
        grid=(bsz, nq),
        in_specs=in_specs,
        out_specs=pl.BlockSpec((1, tq, n_h * hd), lambda b, t: (b, t, 0)),
        compiler_params=_cparams(("parallel", "parallel")),
        name="attn_" + mode,
    )(*args)


def _dft_tables(n):
    k = jnp.arange(n, dtype=jnp.int32)[:, None]
    s = jnp.arange(n, dtype=jnp.int32)[None, :]
    ang = ((k * s) % (2 * n)).astype(F32) * (math.pi / n)
    cos = jnp.cos(ang)
    sin = jnp.where(k == 0, jnp.where(s % 2 == 0, 1.0, -1.0), jnp.sin(ang))
    fwd = jnp.concatenate([cos, sin], axis=0).astype(BF16)
    return fwd, fwd.T


def _filter_kernel(band_ref, w1_ref, b1_ref, fr_ref, w2_ref, b2_ref, w3_ref, dl_ref, fs_ref, fd_ref, nrm_ref, *, n):
    pos_i = lax.broadcasted_iota(jnp.int32, (n, LANES), 0)
    lane = lax.broadcasted_iota(jnp.int32, (n, LANES), 1)
    pos = pos_i.astype(F32)
    t = pos / (n - 1)
    ang = band_ref[...] * pos / n
    z = jnp.where(lane == 0, t,
                  jnp.where(lane <= HY_BANDS, jnp.cos(ang),
                            jnp.where(lane <= 2 * HY_BANDS, -jnp.sin(ang), 0.0)))
    freq = fr_ref[...]
    h = jnp.sin(freq * (_dot_hi(z, w1_ref[...]) + b1_ref[...]))
    h = jnp.sin(freq * (_dot_hi(h, w2_ref[...]) + b2_ref[...]))
    h = _dot_hi(h, w3_ref[...])
    tc = t[:, 0:1]
    win = jnp.exp(-tc * dl_ref[...])
    fwd = h[:, :HY_WIDTH] * win
    bwd = jnp.where(pos_i[:, 0:1] == 0, 0.0, h[:, HY_WIDTH:] * win)
    fs_ref[...] = fwd + bwd
    fd_ref[...] = fwd - bwd
    nrm_ref[...] = jnp.sum(jnp.abs(fwd) + jnp.abs(bwd), axis=0, keepdims=True)


def _hyena_filter(n, band_row, w1p, b1, freq, w2, b2, w3, deltas):
    return pl.pallas_call(
        functools.partial(_filter_kernel, n=n),
        out_shape=[jax.ShapeDtypeStruct((n, HY_WIDTH), F32), jax.ShapeDtypeStruct((n, HY_WIDTH), F32),
                   jax.ShapeDtypeStruct((1, HY_WIDTH), F32)],
        compiler_params=pltpu.CompilerParams(vmem_limit_bytes=VMEM_LIMIT),
        name="hyena_filter",
    )(band_row, w1p, b1, freq, w2, b2, w3, deltas)


def _spectrum_kernel(f_ref, x_ref, o_ref):
    o_ref[...] = _dot(f_ref[...], x_ref[...].astype(BF16))


def _filter_spectrum(fwd_tab, x):
    n2, n = fwd_tab.shape
    w = x.shape[1]
    tc = 256
    return pl.pallas_call(
        _spectrum_kernel,
        out_shape=jax.ShapeDtypeStruct((n2, w), F32),
        grid=(w // tc,),
        in_specs=[_const_spec((n2, n)), pl.BlockSpec((n, tc), lambda j: (0, j))],
        out_specs=pl.BlockSpec((n2, tc), lambda j: (0, j)),
        compiler_params=_cparams(("arbitrary",)),
        name="hyena_filter_spectrum",
    )(fwd_tab, x)


def _short_conv(x, w_ref, b_ref):
    n = x.shape[0]
    row = lax.broadcasted_iota(jnp.int32, x.shape, 0)
    prev = jnp.where(row == 0, 0.0, pltpu.roll(x, 1, axis=0))
    nxt = jnp.where(row == n - 1, 0.0, pltpu.roll(x, n - 1, axis=0))
    return w_ref[0:1, :] * prev + w_ref[1:2, :] * x + w_ref[2:3, :] * nxt + b_ref[...]


def _hyena_fwd_kernel(f_ref, x1_ref, v_ref, w1_ref, b1_ref, wv_ref, bv_ref, kr_ref, ka_ref, kn_ref, nrm_ref,
                      y_ref, u_ref, *, n):
    x1 = _short_conv(x1_ref[0], w1_ref, b1_ref)
    v = _short_conv(v_ref[0], wv_ref, bv_ref)
    ub = (v * x1).astype(BF16)
    u_ref[0] = ub
    spec = _dot(f_ref[...], ub)
    ur = spec[0:n]
    ua = spec[n:2 * n]
    row0 = lax.broadcasted_iota(jnp.int32, ur.shape, 0) == 0
    scale = jnp.where(row0, 1.0 / (2 * n), 1.0 / n) / nrm_ref[...]
    kr = kr_ref[...]
    ka = jnp.where(row0, 0.0, ka_ref[...])
    k4 = jnp.where(row0, kn_ref[...], kr)
    y_ref[0, 0:n] = ((ur * kr - ua * ka) * scale).astype(y_ref.dtype)
    y_ref[0, n:2 * n] = ((ur * ka + ua * k4) * scale).astype(y_ref.dtype)


def _hyena_inv_kernel(ft_ref, y_ref, u_ref, x0_ref, w0_ref, b0_ref, skip_ref, o_ref):
    conv = _dot(ft_ref[...], y_ref[0])
    x0 = _short_conv(x0_ref[0], w0_ref, b0_ref)
    o_ref[0] = (x0 * (conv + skip_ref[...] * u_ref[0].astype(F32))).astype(o_ref.dtype)


def _hyena(uh, tabs, spec, knyq, nrm, short_w, short_b, skip):
    fwd_tab, inv_tab = tabs
    bsz, n, _ = uh.shape
    tc = 256
    nct = HY_WIDTH // tc
    chan = lambda off: (lambda c, b: (b, 0, off + c))
    par = lambda off: (lambda c, b: (0, off + c))
    y, u = pl.pallas_call(
        functools.partial(_hyena_fwd_kernel, n=n),
        out_shape=[jax.ShapeDtypeStruct((bsz, 2 * n, HY_WIDTH), BF16), jax.ShapeDtypeStruct((bsz, n, HY_WIDTH), BF16)],
        grid=(nct, bsz),
        in_specs=[
            _const_spec((2 * n, n)),
            pl.BlockSpec((1, n, tc), chan(nct)), pl.BlockSpec((1, n, tc), chan(2 * nct)),
            pl.BlockSpec((3, tc), par(nct)), pl.BlockSpec((1, tc), par(nct)),
            pl.BlockSpec((3, tc), par(2 * nct)), pl.BlockSpec((1, tc), par(2 * nct)),
            pl.BlockSpec((n, tc), lambda c, b: (0, c)), pl.BlockSpec((n, tc), lambda c, b: (1, nct + c)),
            pl.BlockSpec((1, tc), par(0)), pl.BlockSpec((1, tc), par(0)),
        ],
        out_specs=[pl.BlockSpec((1, 2 * n, tc), lambda c, b: (b, 0, c)), pl.BlockSpec((1, n, tc), lambda c, b: (b, 0, c))],
        compiler_params=_cparams(("arbitrary", "arbitrary")),
        name="hyena_fwd",
    )(fwd_tab, uh, uh, short_w, short_b, short_w, short_b, spec, spec, knyq, nrm)
    return pl.pallas_call(
        _hyena_inv_kernel,
        out_shape=jax.ShapeDtypeStruct((bsz, n, HY_WIDTH), BF16),
        grid=(nct, bsz),
        in_specs=[
            _const_spec((n, 2 * n)),
            pl.BlockSpec((1, 2 * n, tc), lambda c, b: (b, 0, c)), pl.BlockSpec((1, n, tc), lambda c, b: (b, 0, c)),
            pl.BlockSpec((1, n, tc), chan(0)), pl.BlockSpec((3, tc), par(0)), pl.BlockSpec((1, tc), par(0)),
            pl.BlockSpec((1, tc), par(0)),
        ],
        out_specs=pl.BlockSpec((1, n, tc), lambda c, b: (b, 0, c)),
        compiler_params=_cparams(("arbitrary", "arbitrary")),
        name="hyena_inv",
    )(inv_tab, y, u, uh, short_w, short_b, skip)


def _merge_kernel(x_ref, mod_ref, ya_ref, yh_ref, yw_ref, gt_ref, wb_ref, wo_ref, g2_ref, rw_ref, rb_ref,
                  xo_ref, tok_ref, route_ref, rt_ref):
    tm = x_ref.shape[1]
    n_part = 2 if tm % 512 == 0 else 1
    for part in range(n_part):
        rows = pl.ds(part * (tm // n_part), tm // n_part)
        _merge_rows(rows, x_ref, mod_ref, ya_ref, yh_ref, yw_ref, gt_ref, wb_ref, wo_ref, g2_ref, rw_ref, rb_ref,
                    xo_ref, tok_ref, route_ref, rt_ref)


def _merge_rows(rows, x_ref, mod_ref, ya_ref, yh_ref, yw_ref, gt_ref, wb_ref, wo_ref, g2_ref, rw_ref, rb_ref,
                xo_ref, tok_ref, route_ref, rt_ref):
    gt = gt_ref[0, rows, :]
    m = gt[:, 0:D_MODEL].astype(F32) * _dot(ya_ref[0, rows, :], wb_ref[0])
    m = m + gt[:, D_MODEL:2 * D_MODEL].astype(F32) * _dot(yh_ref[0, rows, :], wb_ref[1])
    m = m + gt[:, 2 * D_MODEL:3 * D_MODEL].astype(F32) * _dot(yw_ref[0, rows, :], wb_ref[2])
    x = x_ref[0, rows, :] + mod_ref[0, 2:3, :] * _dot(m.astype(BF16), wo_ref[...])
    xo_ref[0, rows, :] = x
    ms = jnp.mean(x * x, axis=-1, keepdims=True)
    tok = x * lax.rsqrt(ms + EPS) * g2_ref[...]
    tok = tok * (1.0 + mod_ref[0, 4:5, :]) + mod_ref[0, 3:4, :]
    tok_ref[0, rows, :] = _pack_bf16_pairs(tok)

    tok_hi = tok.astype(BF16)
    tok_lo = (tok - tok_hi.astype(F32)).astype(BF16)
    lg2 = _dot(tok_hi, rw_ref[...]) + _dot(tok_lo, rw_ref[...])
    lg = lg2[:, :ROUTER_W] + lg2[:, ROUTER_W:] + rb_ref[...]
    lane = lax.broadcasted_iota(jnp.int32, lg.shape, 1).astype(F32)
    big = float(ROUTER_W)

    def first_lane(mask):
        return jnp.min(jnp.where(mask, lane, big), axis=-1, keepdims=True)

    is_g = lane < N_GROUPS
    gmax = jnp.max(jnp.where(is_g, lg, NEG_INF), axis=-1, keepdims=True)
    g_sel = first_lane(jnp.where(is_g, lg, NEG_INF) == gmax)
    p_g = 1.0 / jnp.sum(jnp.where(is_g, jnp.exp(lg - gmax), 0.0), axis=-1, keepdims=True)
    lo = N_GROUPS + g_sel * EXPERTS_PER_GROUP
    in_grp = (lane >= lo) & (lane < lo + EXPERTS_PER_GROUP)
    le = jnp.where(in_grp, lg, NEG_INF)
    v1 = jnp.max(le, axis=-1, keepdims=True)
    i1 = first_lane(le == v1)
    le = jnp.where(lane == i1, NEG_INF, le)
    v2 = jnp.max(le, axis=-1, keepdims=True)
    i2 = first_lane(le == v2)
    e2 = jnp.exp(v2 - v1)
    p1 = p_g / (1.0 + e2)
    p2 = p_g * e2 / (1.0 + e2)
    route = jnp.where(lane == 0, i1 - N_GROUPS, jnp.where(lane == 1, i2 - N_GROUPS,
                      jnp.where(lane == 2, p1, jnp.where(lane == 3, p2, 0.0))))
    route_ref[0, rows, :] = route
    rt_ref[0, :, rows] = route.T[0:ROUTE_T_ROWS, :]


def _merge(x, mod, mod_row, ya, yh, yw, gates, w_branch, w_out, g2, rw, rb):
    bsz, s, d = x.shape
    tm = min(512, s)
    row = lambda b, t: (b, t, 0)
    return pl.pallas_call(
        _merge_kernel,
        out_shape=[jax.ShapeDtypeStruct((bsz, s, d), F32), jax.ShapeDtypeStruct((bsz, s, d // 2), jnp.uint32),
                   jax.ShapeDtypeStruct((bsz, s, ROUTER_W), F32), jax.ShapeDtypeStruct((bsz, ROUTE_T_ROWS, s), F32)],
        grid=(bsz, s // tm),
        in_specs=[
            pl.BlockSpec((1, tm, d), row),
            pl.BlockSpec((1, N_MOD, d), lambda b, t: (mod_row(b), 0, 0)),
            pl.BlockSpec((1, tm, QA_W), row), pl.BlockSpec((1, tm, HY_WIDTH), row), pl.BlockSpec((1, tm, QW_W), row),
            pl.BlockSpec((1, tm, N_BRANCH * d), row),
            _const_spec((N_BRANCH, HY_WIDTH, d)), _const_spec((d, d)), _const_spec((1, d)),
            _const_spec((d, 2 * ROUTER_W)), _const_spec((1, ROUTER_W)),
        ],
        out_specs=[pl.BlockSpec((1, tm, d), row), pl.BlockSpec((1, tm, d // 2), row),
                   pl.BlockSpec((1, tm, ROUTER_W), row), pl.BlockSpec((1, ROUTE_T_ROWS, tm), lambda b, t: (b, 0, t))],
        compiler_params=_cparams(("parallel", "parallel")),
        name="merge_router",
    )(x, mod, ya, yh, yw, gates, w_branch, w_out, g2, rw, rb)


def _route_plan(flat):
    n_tiles = -(-flat.shape[0] // MOE_TM) + N_EXPERTS
    onehot = (flat[:, None] == jnp.arange(N_EXPERTS, dtype=jnp.int32)[None, :]).astype(jnp.int32)
    csum = jnp.cumsum(onehot, axis=0)
    count = csum[-1]
    rank = jnp.sum((csum - onehot) * onehot, axis=1)
    size = ((count + MOE_TM - 1) // MOE_TM) * MOE_TM
    end = jnp.cumsum(size)
    start = end - size
    slot = jnp.sum(onehot * start[None, :], axis=1) + rank
    tile_row = jnp.arange(n_tiles, dtype=jnp.int32) * MOE_TM
    tile_e = jnp.minimum(jnp.sum((tile_row[:, None] >= end[None, :]).astype(jnp.int32), axis=1), N_EXPERTS - 1)
    tile_oh = (tile_e[:, None] == jnp.arange(N_EXPERTS, dtype=jnp.int32)[None, :]).astype(jnp.int32)
    filled = jnp.sum(tile_oh * (start + count)[None, :], axis=1)
    tile_rows = jnp.clip(filled - tile_row, 0, MOE_TM)
    return slot, tile_e, tile_rows, n_tiles


def _sc_mesh():
    info = plsc.get_sparse_core_info()
    mesh = plsc.VectorSubcoreMesh(core_axis_name="c", subcore_axis_name="s")
    return mesh, info.num_cores, info.num_cores * info.num_subcores


def _sc_scatter_rows(sources, slot, n_out):
    mesh, n_cores, n_workers = _sc_mesh()
    n = sum(src.shape[0] for src in sources)
    w = sources[0].shape[1]
    dtype = sources[0].dtype
    n_k = slot.shape[0] // n
    per_worker = n // n_workers
    n_chunks = per_worker // SC_CHUNK
    assert per_worker * n_workers == n and n_chunks * SC_CHUNK == per_worker
    assert all(src.shape[0] % SC_CHUNK == 0 for src in sources)
    n_src = len(sources)

    @functools.partial(
        pl.kernel, mesh=mesh,
        out_type=jax.ShapeDtypeStruct((n_out, w), dtype),
        scratch_types=[pltpu.VMEM((SC_CHUNK,), jnp.int32), pltpu.VMEM((SC_CHUNK, w), dtype), pltpu.SemaphoreType.DMA],
    )
    def scatter(*refs):
        src_hbm = refs[:n_src]
        slot_hbm, out_hbm, idx_v, rows_v, sem = refs[n_src:]
        base = (lax.axis_index("s") * n_cores + lax.axis_index("c")) * per_worker

        @pl.loop(0, n_chunks)
        def _(j):
            off = base + j * SC_CHUNK
            lo = 0
            for src in src_hbm:
                hi = lo + src.shape[0]

                @pl.when((off >= lo) & (off < hi))
                def _(src=src, lo=lo):
                    pltpu.sync_copy(src.at[pl.ds(off - lo, SC_CHUNK)], rows_v)

                lo = hi
            for q in range(n_k):
                pltpu.sync_copy(slot_hbm.at[pl.ds(q * n + off, SC_CHUNK)], idx_v)
                pltpu.async_copy(rows_v, out_hbm.at[idx_v], sem).wait()

    return scatter(*sources, slot)


def _sc_gather_rows(table, idx):
    mesh, n_cores, n_workers = _sc_mesh()
    m = idx.shape[0]
    w = table.shape[1]
    per_worker = m // n_workers
    n_chunks = per_worker // SC_CHUNK
    assert per_worker * n_workers == m and n_chunks * SC_CHUNK == per_worker

    @functools.partial(
        pl.kernel, mesh=mesh,
        out_type=jax.ShapeDtypeStruct((m, w), table.dtype),
        scratch_types=[pltpu.VMEM((SC_CHUNK,), jnp.int32), pltpu.VMEM((SC_CHUNK, w), table.dtype),
                       pltpu.SemaphoreType.DMA],
    )
    def gather(table_hbm, idx_hbm, out_hbm, idx_v, rows_v, sem):
        base = (lax.axis_index("s") * n_cores + lax.axis_index("c")) * per_worker

        @pl.loop(0, n_chunks)
        def _(j):
            off = base + j * SC_CHUNK
            pltpu.sync_copy(idx_hbm.at[pl.ds(off, SC_CHUNK)], idx_v)
            pltpu.async_copy(table_hbm.at[idx_v], rows_v, sem).wait()
            pltpu.sync_copy(rows_v, out_hbm.at[pl.ds(off, SC_CHUNK)])

    return gather(table, idx)


def _experts_kernel(te_ref, tr_ref, x_ref, w1_ref, w3_ref, w2_ref, o_ref, w1_b, w3_b, w2_b):
    t = pl.program_id(0)
    n_rows = tr_ref[t]

    @pl.when((t == 0) | (te_ref[t] != te_ref[jnp.maximum(t - 1, 0)]))
    def _():
        w1_b[...] = w1_ref[0, 0].astype(BF16)
        w3_b[...] = w3_ref[0, 0].astype(BF16)
        w2_b[...] = w2_ref[0, 0].astype(BF16)

    @pl.when(n_rows > 0)
    def _():
        xp = x_ref[...]
        row = lax.broadcasted_iota(jnp.int32, xp.shape, 0)
        x = _unpack_bf16_pairs(jnp.where(row < n_rows, xp, jnp.uint32(0))).astype(BF16)
        h1 = _dot(x, w1_b[...])
        h = (h1 * jax.nn.sigmoid(h1)) * _dot(x, w3_b[...])
        o_ref[...] = _pack_bf16_pairs(_dot(h.astype(BF16), w2_b[...]))

    @pl.when(n_rows == 0)
    def _():
        o_ref[...] = jnp.zeros_like(o_ref)


def _experts(x_sorted, tile_e, tile_rows, layer, w1, w3, w2):
    n_slots, wp = x_sorted.shape
    d = 2 * wp
    wspec = lambda shape: pl.BlockSpec(shape, lambda t, te, tr: (layer, te[t], 0, 0))
    return pl.pallas_call(
        _experts_kernel,
        out_shape=jax.ShapeDtypeStruct((n_slots, wp), jnp.uint32),
        grid_spec=pltpu.PrefetchScalarGridSpec(
            num_scalar_prefetch=2,
            grid=(n_slots // MOE_TM,),
            in_specs=[pl.BlockSpec((MOE_TM, wp), lambda t, te, tr: (t, 0)),
                      wspec((1, 1, d, D_EXPERT)), wspec((1, 1, d, D_EXPERT)), wspec((1, 1, D_EXPERT, d))],
            out_specs=pl.BlockSpec((MOE_TM, wp), lambda t, te, tr: (t, 0)),
            scratch_shapes=[pltpu.VMEM((d, D_EXPERT), BF16), pltpu.VMEM((d, D_EXPERT), BF16),
                            pltpu.VMEM((D_EXPERT, d), BF16)],
        ),
        compiler_params=_cparams(("arbitrary",)),
        name="moe_experts",
    )(tile_e, tile_rows, x_sorted, w1, w3, w2)


def _combine_kernel(x_ref, mod_ref, ya_ref, yb_ref, route_ref, o_ref):
    route = route_ref[0]
    y = route[:, 2:3] * _unpack_bf16_pairs(ya_ref[...]) + route[:, 3:4] * _unpack_bf16_pairs(yb_ref[...])
    o_ref[0] = x_ref[0] + mod_ref[0, 5:6, :] * y


def _combine(x, mod, mod_row, y_tok, route, row_off, n_all):
    bsz, s, d = x.shape
    tm = min(512, s)
    nt = s // tm
    assert row_off % tm == 0 and n_all % tm == 0
    off0 = row_off // tm
    off1 = (n_all + row_off) // tm
    row = lambda b, t: (b, t, 0)
    return pl.pallas_call(
        _combine_kernel,
        out_shape=jax.ShapeDtypeStruct((bsz, s, d), F32),
        grid=(bsz, nt),
        in_specs=[
            pl.BlockSpec((1, tm, d), row),
            pl.BlockSpec((1, N_MOD, d), lambda b, t: (mod_row(b), 0, 0)),
            pl.BlockSpec((tm, d // 2), lambda b, t: (off0 + b * nt + t, 0)),
            pl.BlockSpec((tm, d // 2), lambda b, t: (off1 + b * nt + t, 0)),
            pl.BlockSpec((1, tm, ROUTER_W), row),
        ],
        out_specs=pl.BlockSpec((1, tm, d), row),
        compiler_params=_cparams(("parallel", "parallel")),
        name="moe_combine",
    )(x, mod, y_tok, y_tok, route)


def _moe(streams, mod, layer, w1, w3, w2):
    toks = [st[2].reshape(-1, st[2].shape[-1]) for st in streams]
    n_all = sum(t.shape[0] for t in toks)
    flat = jnp.concatenate([st[4][:, k, :].reshape(-1) for k in range(2) for st in streams]).astype(jnp.int32)
    slot, tile_e, tile_rows, n_tiles = _route_plan(flat)
    x_sorted = _sc_scatter_rows(toks, slot, n_tiles * MOE_TM)
    y_sorted = _experts(x_sorted, tile_e, tile_rows, layer, w1, w3, w2)
    y_tok = _sc_gather_rows(y_sorted, slot)
    outs, row_off = [], 0
    for x1, mod_row, _, route, _ in streams:
        outs.append(_combine(x1, mod, mod_row, y_tok, route, row_off, n_all))
        row_off += x1.shape[0] * x1.shape[1]
    return outs


def _rope_tables(n_tok):
    rows = n_tok // GRID_W
    row_id = jnp.repeat(jnp.arange(rows), GRID_W)
    col_id = jnp.tile(jnp.arange(GRID_W), rows)
    inv_freq = ROPE_THETA ** (-jnp.arange(ROPE_FREQS, dtype=F32) / ROPE_FREQS)
    ang_r = row_id[:, None] * inv_freq
    ang_c = col_id[:, None] * inv_freq
    cos_h = jnp.concatenate([jnp.cos(ang_r), jnp.cos(ang_r), jnp.cos(ang_c), jnp.cos(ang_c)], axis=-1)
    sin_h = jnp.concatenate([-jnp.sin(ang_r), jnp.sin(ang_r), -jnp.sin(ang_c), jnp.sin(ang_c)], axis=-1)
    reps = KA_W // HEAD_DIM
    return jnp.tile(cos_h, (1, reps)), jnp.tile(sin_h, (1, reps))


def kernel(x, c, ctx, c_ctx, ada_w, ada_b, norm1_g, norm2_g, w_in, qk_g, hy_short_w, hy_short_b, hy_f_w1, hy_f_b1,
           hy_f_freq, hy_f_w2, hy_f_b2, hy_f_w3, hy_bias, sinks, w_branch, w_out, router_g_w, router_g_b,
           router_e_w, router_e_b, exp_w1, exp_w3, exp_w2):
    bsz, n_tok, d = x.shape
    n_ctx = ctx.shape[1]
    depth = ada_w.shape[0]
    assert d == D_MODEL and bsz < MOD_ROWS and n_tok % 256 == 0 and n_ctx % 256 == 0

    cc = jnp.zeros((MOD_ROWS, d), F32).at[:bsz].set(c).at[bsz].set(c_ctx)
    mods = _ada_table(cc, ada_w, ada_b)
    ctx_row = lambda b: bsz

    rope_tabs = _rope_tables(n_tok)
    head_ones = (jnp.arange(QA_W)[:, None] // HEAD_DIM == jnp.arange(QA_W)[None, :] // HEAD_DIM).astype(BF16)
    tabs_lat = _dft_tables(n_tok)
    tabs_ctx = _dft_tables(n_ctx)
    bands = jnp.linspace(1e-4, HY_BANDS - 1, HY_BANDS, dtype=F32)
    band_row = jnp.zeros((1, LANES), F32).at[0, 1:1 + HY_BANDS].set(bands).at[0, 1 + HY_BANDS:1 + 2 * HY_BANDS].set(bands)
    band_row = 2 * math.pi * band_row
    deltas = jnp.abs(jnp.linspace(math.log(HY_TARGET) / HY_SLOW_DECAY, math.log(HY_TARGET) / HY_FAST_DECAY,
                                  HY_WIDTH, dtype=F32))[None, :]

    n_split = 2 if bsz % 2 == 0 else 1
    per = bsz // n_split
    xs = [x[g * per:(g + 1) * per] for g in range(n_split)]
    xcs = [ctx[g * per:(g + 1) * per] for g in range(n_split)]
    for l in range(depth):
        update_ctx = l < depth - 1
        mod = mods[l]
        w_in_b = w_in[l].astype(BF16)
        gq = [jnp.tile(qk_g[l, 0], A_HEADS)[None], jnp.tile(qk_g[l, 1], A_KV)[None],
              jnp.tile(qk_g[l, 2], W_HEADS)[None], jnp.tile(qk_g[l, 3], W_KV)[None]]
        g1 = norm1_g[l][None]

        w1p = jnp.zeros((LANES, HY_HIDDEN), F32).at[:hy_f_w1.shape[1]].set(hy_f_w1[l])
        fargs = (band_row, w1p, hy_f_b1[l][None], hy_f_freq[l][None], hy_f_w2[l], hy_f_b2[l][None], hy_f_w3[l], deltas)
        skip = hy_bias[l][None]
        sb = hy_short_b[l][None]

        def hyena_filter(n, tabs):
            fsum, fdiff, nrm = _hyena_filter(n, *fargs)
            spec = _filter_spectrum(tabs[0], jnp.concatenate([fsum, fdiff], axis=1))
            return spec, spec[n:n + 1, :HY_WIDTH], nrm

        filt_lat = hyena_filter(n_tok, tabs_lat)
        filt_ctx = hyena_filter(n_ctx, tabs_ctx) if update_ctx else None

        wb = w_branch[l].astype(BF16)
        wo = w_out[l].astype(BF16)
        g2 = norm2_g[l][None]
        rw = jnp.zeros((d, ROUTER_W), F32).at[:, :N_GROUPS].set(router_g_w[l]).at[:, N_GROUPS:N_GROUPS + N_EXPERTS].set(router_e_w[l])
        rb = jnp.zeros((1, ROUTER_W), F32).at[0, :N_GROUPS].set(router_g_b[l]).at[0, N_GROUPS:N_GROUPS + N_EXPERTS].set(router_e_b[l])
        rw_hi = rw.astype(BF16)
        rw = jnp.concatenate([rw_hi, (rw - rw_hi.astype(F32)).astype(BF16)], axis=1)

        for g in range(n_split):
            x, xc = xs[g], xcs[g]
            lat_row = lambda b, g=g: g * per + b
            qa, ka, va, uh, qw, kw, vw, gts = _in_proj(x, mod, lat_row, g1, w_in_b, gq, head_ones, rope_tabs)
            qa_c, ka_c, va_c, uh_c, qw_c, kw_c, vw_c, gts_c = _in_proj(xc, mod, ctx_row, g1, w_in_b, gq, head_ones, None)

            ya = _attention(qa, ka_c, va_c, mode="global", kl=ka, vl=va, tq=1024)
            yw = _attention(qw, kw_c, vw_c, mode="window", kl=kw, vl=vw, sinks=sinks[l], tq=512)
            yh = _hyena(uh, tabs_lat, *filt_lat, hy_short_w[l], sb, skip)
            x1, tok, route, route_t = _merge(x, mod, lat_row, ya, yh, yw, gts, wb, wo, g2, rw, rb)
            streams = [(x1, lat_row, tok, route, route_t)]
            if update_ctx:
                ya_c = _attention(qa_c, ka_c, va_c, mode="ctx", tq=256)
                yw_c = _attention(qw_c, kw_c, vw_c, mode="ctx_sink", sinks=sinks[l], tq=256)
                yh_c = _hyena(uh_c, tabs_ctx, *filt_ctx, hy_short_w[l], sb, skip)
                xc1, tok_c, route_c, route_ct = _merge(xc, mod, ctx_row, ya_c, yh_c, yw_c, gts_c, wb, wo, g2, rw, rb)
                streams.append((xc1, ctx_row, tok_c, route_c, route_ct))
            outs = _moe(streams, mod, l, exp_w1, exp_w3, exp_w2)
            xs[g] = outs[0]
            if update_ctx:
                xcs[g] = outs[1]
    return jnp.concatenate(xs, axis=0) if n_split > 1 else xs[0]
```

```python
import functools
import math

import jax
import jax.numpy as jnp
from jax import lax
from jax.experimental import pallas as pl
from jax.experimental.pallas import tpu as pltpu
from jax.experimental.pallas import tpu_sc as plsc

F32 = jnp.float32
BF16 = jnp.bfloat16
HIGHEST = lax.Precision.HIGHEST

D_MODEL = 1024
GRID_W = 64
HEAD_DIM = 64
ROPE_FREQS = HEAD_DIM // 4
ROPE_THETA = 10000.0
ATTN_SCALE = HEAD_DIM ** -0.5
LOG2E = math.log2(math.e)
Q_SCALE = ATTN_SCALE * LOG2E
V_AUG = 128
NEG_INF = -1e30
WINDOW = 128
A_HEADS = 8
A_KV = 2
W_HEADS = 8
W_KV = 2
HY_WIDTH = 512
HY_BANDS = 16
HY_HIDDEN = 64
HY_TARGET = 1e-2
HY_FAST_DECAY = 0.3
HY_SLOW_DECAY = 1.5
N_BRANCH = 3
N_GROUPS = 4
EXPERTS_PER_GROUP = 4
N_EXPERTS = N_GROUPS * EXPERTS_PER_GROUP
D_EXPERT = 512
N_MOD = 6
EPS = 1e-6

QA_W = A_HEADS * HEAD_DIM
KA_W = A_KV * HEAD_DIM
QW_W = W_HEADS * HEAD_DIM
KW_W = W_KV * HEAD_DIM
O_QA = 0
O_KA = O_QA + QA_W
O_VA = O_KA + KA_W
O_UH = O_VA + KA_W
O_QW = O_UH + 3 * HY_WIDTH
O_KW = O_QW + QW_W
O_VW = O_KW + KW_W
O_GT = O_VW + KW_W
D_IN = O_GT + N_BRANCH * D_MODEL

MOD_ROWS = 24
LANES = 128
ROUTER_W = LANES
ROUTE_T_ROWS = 8
VMEM_LIMIT = 56 * 1024 * 1024
MOE_TM = 1024
SC_CHUNK = 64


def _cparams(sem):
    return pltpu.CompilerParams(dimension_semantics=sem, vmem_limit_bytes=VMEM_LIMIT)


def _const_spec(shape):
    nd = len(shape)
    return pl.BlockSpec(shape, lambda *_: (0,) * nd, pipeline_mode=pl.Buffered(1))


def _dot(a, b):
    return jnp.dot(a, b, preferred_element_type=F32)


def _dot_nt(a, b):
    return lax.dot_general(a, b, (((1,), (1,)), ((), ())), preferred_element_type=F32)


def _dot_hi(a, b):
    return jnp.dot(a, b, precision=HIGHEST, preferred_element_type=F32)


def _pack_bf16_pairs(t):
    half = t.shape[-1] // 2
    hi = lax.bitcast_convert_type(t[:, :half].astype(BF16).astype(F32), jnp.uint32)
    lo = lax.bitcast_convert_type(t[:, half:].astype(BF16).astype(F32), jnp.uint32)
    return hi | (lo >> 16)


def _unpack_bf16_pairs(p):
    hi = lax.bitcast_convert_type(p & jnp.uint32(0xFFFF0000), F32)
    lo = lax.bitcast_convert_type(p << 16, F32)
    return jnp.concatenate([hi, lo], axis=-1)


def _ada_kernel(cc_ref, w_ref, b_ref, o_ref):
    cc = cc_ref[...]
    act = cc * jax.nn.sigmoid(cc)
    o_ref[0] = _dot_hi(act, w_ref[0]) + b_ref[0]


def _ada_table(cc, ada_w, ada_b):
    depth = ada_w.shape[0]
    tn = 1536
    nmod = ada_w.shape[2]
    out = pl.pallas_call(
        _ada_kernel,
        out_shape=jax.ShapeDtypeStruct((depth, MOD_ROWS, nmod), F32),
        grid=(depth, nmod // tn),
        in_specs=[
            pl.BlockSpec((MOD_ROWS, D_MODEL), lambda l, j: (0, 0)),
            pl.BlockSpec((1, D_MODEL, tn), lambda l, j: (l, 0, j)),
            pl.BlockSpec((1, 1, tn), lambda l, j: (l, 0, j)),
        ],
        out_specs=pl.BlockSpec((1, MOD_ROWS, tn), lambda l, j: (l, 0, j)),
        compiler_params=_cparams(("arbitrary", "arbitrary")),
        name="ada_table",
    )(cc, ada_w, ada_b.reshape(depth, 1, nmod))
    return out.reshape(depth, MOD_ROWS, N_MOD, D_MODEL)


def _in_proj_kernel(*refs, rope):
    if rope:
        (x_ref, mod_ref, g1_ref, w_ref, gqa_ref, gka_ref, gqw_ref, gkw_ref, hs_ref, cos_ref, sin_ref,
         qa_ref, ka_ref, va_ref, uh_ref, qw_ref, kw_ref, vw_ref, gt_ref) = refs
    else:
        (x_ref, mod_ref, g1_ref, w_ref, gqa_ref, gka_ref, gqw_ref, gkw_ref, hs_ref,
         qa_ref, ka_ref, va_ref, uh_ref, qw_ref, kw_ref, vw_ref, gt_ref) = refs
        cos_ref = sin_ref = None
    tm = x_ref.shape[1]
    n_part = 2 if tm % 512 == 0 else 1
    for part in range(n_part):
        rows = pl.ds(part * (tm // n_part), tm // n_part)
        _in_proj_rows(rows, rope, x_ref, mod_ref, g1_ref, w_ref, gqa_ref, gka_ref, gqw_ref, gkw_ref, hs_ref,
                      cos_ref, sin_ref, qa_ref, ka_ref, va_ref, uh_ref, qw_ref, kw_ref, vw_ref, gt_ref)


def _in_proj_rows(rows, rope, x_ref, mod_ref, g1_ref, w_ref, gqa_ref, gka_ref, gqw_ref, gkw_ref, hs_ref,
                  cos_ref, sin_ref, qa_ref, ka_ref, va_ref, uh_ref, qw_ref, kw_ref, vw_ref, gt_ref):
    x = x_ref[0, rows, :]
    ms = jnp.mean(x * x, axis=-1, keepdims=True)
    h = x * lax.rsqrt(ms + EPS) * g1_ref[...]
    h = (h * (1.0 + mod_ref[0, 1:2, :]) + mod_ref[0, 0:1, :]).astype(BF16)

    def proj(lo, width):
        return _dot(h, w_ref[:, lo:lo + width])

    def head_norm_rope(t, g_ref):
        width = t.shape[-1]
        ssum = _dot((t * t).astype(BF16), hs_ref[0:width, 0:width])
        tn = t * lax.rsqrt(ssum * (1.0 / HEAD_DIM) + EPS) * g_ref[...]
        if rope:
            reps = width // cos_ref.shape[-1]
            cos = cos_ref[rows, :]
            sin = sin_ref[rows, :]
            if reps > 1:
                cos = jnp.concatenate([cos] * reps, axis=-1)
                sin = jnp.concatenate([sin] * reps, axis=-1)
            lane = lax.broadcasted_iota(jnp.int32, tn.shape, 1)
            first = (lane & (2 * ROPE_FREQS - 1)) < ROPE_FREQS
            partner = jnp.where(first, pltpu.roll(tn, width - ROPE_FREQS, axis=1), pltpu.roll(tn, ROPE_FREQS, axis=1))
            tn = tn * cos + partner * sin
        return tn

    def store_heads(o_ref, t):
        for hh in range(t.shape[-1] // HEAD_DIM):
            o_ref[0, hh, rows, :] = t[:, hh * HEAD_DIM:(hh + 1) * HEAD_DIM].astype(o_ref.dtype)

    def store_values(o_ref, t):
        tail = (lax.broadcasted_iota(jnp.int32, (t.shape[0], V_AUG - HEAD_DIM), 1) == 0).astype(F32)
        for hh in range(t.shape[-1] // HEAD_DIM):
            o_ref[0, hh, rows, :] = jnp.concatenate([t[:, hh * HEAD_DIM:(hh + 1) * HEAD_DIM], tail],
                                                    axis=-1).astype(o_ref.dtype)

    gt_ref[0, rows, :] = jax.nn.sigmoid(proj(O_GT, N_BRANCH * D_MODEL)).astype(gt_ref.dtype)
    store_heads(qa_ref, head_norm_rope(proj(O_QA, QA_W), gqa_ref) * Q_SCALE)
    store_heads(qw_ref, head_norm_rope(proj(O_QW, QW_W), gqw_ref) * Q_SCALE)
    store_heads(ka_ref, head_norm_rope(proj(O_KA, KA_W), gka_ref))
    store_heads(kw_ref, head_norm_rope(proj(O_KW, KW_W), gkw_ref))
    store_values(va_ref, proj(O_VA, KA_W))
    store_values(vw_ref, proj(O_VW, KW_W))
    uh_ref[0, rows, :] = proj(O_UH, 3 * HY_WIDTH)


def _in_proj(x, mod, mod_row, g1, w_in, gq, hs, rope_tabs):
    bsz, s, d = x.shape
    tm = min(512, s)
    rope = rope_tabs is not None
    in_specs = [
        pl.BlockSpec((1, tm, d), lambda b, t: (b, t, 0)),
        pl.BlockSpec((1, N_MOD, d), lambda b, t: (mod_row(b), 0, 0)),
        _const_spec((1, d)),
        _const_spec((d, D_IN)),
        _const_spec((1, QA_W)), _const_spec((1, KA_W)), _const_spec((1, QW_W)), _const_spec((1, KW_W)),
        _const_spec((QA_W, QA_W)),
    ]
    args = [x, mod, g1, w_in, gq[0], gq[1], gq[2], gq[3], hs]
    if rope:
        in_specs += [pl.BlockSpec((tm, KA_W), lambda b, t: (t, 0))] * 2
        args += list(rope_tabs)

    def head_out(n_heads, width=HEAD_DIM):
        return (jax.ShapeDtypeStruct((bsz, n_heads, s, width), BF16),
                pl.BlockSpec((1, n_heads, tm, width), lambda b, t: (b, 0, t, 0)))

    def flat_out(width, dtype):
        return (jax.ShapeDtypeStruct((bsz, s, width), dtype), pl.BlockSpec((1, tm, width), lambda b, t: (b, t, 0)))

    outs = [head_out(A_HEADS), head_out(A_KV), head_out(A_KV, V_AUG), flat_out(3 * HY_WIDTH, F32),
            head_out(W_HEADS), head_out(W_KV), head_out(W_KV, V_AUG), flat_out(N_BRANCH * D_MODEL, BF16)]
    return pl.pallas_call(
        functools.partial(_in_proj_kernel, rope=rope),
        out_shape=[o[0] for o in outs],
        grid=(bsz, s // tm),
        in_specs=in_specs,
        out_specs=[o[1] for o in outs],
        compiler_params=_cparams(("parallel", "parallel")),
        name="in_proj",
    )(*args)


def _attn_kernel(*refs, n_g, tq, mode, s_len):
    refs = list(refs)
    sink_ref = refs.pop(0) if mode in ("window", "ctx_sink") else None
    q_ref = refs.pop(0)
    if mode == "global":
        kl_ref, vl_ref = refs.pop(0), refs.pop(0)
    elif mode == "window":
        n_win = tq // WINDOW + 2
        kwin_refs = [refs.pop(0) for _ in range(n_win)]
        vwin_refs = [refs.pop(0) for _ in range(n_win)]
    kc_ref, vc_ref, o_ref = refs
    n_kv = kc_ref.shape[1]
    if mode == "window":
        qt = pl.program_id(1)
        qpos = lax.broadcasted_iota(jnp.int32, (tq, n_win * WINDOW), 0) + qt * tq
        kpos = lax.broadcasted_iota(jnp.int32, (tq, n_win * WINDOW), 1) + (qt * tq - WINDOW)
        valid = (kpos - qpos <= WINDOW) & (qpos - kpos <= WINDOW) & (kpos >= 0) & (kpos < s_len)
        band_bias = jnp.where(valid, 0.0, NEG_INF)

    def attend(q, kv, sink):
        sc = _dot_nt(q, kc_ref[0, kv])
        m = jnp.max(sc, axis=-1, keepdims=True)
        if mode == "global":
            sl = _dot_nt(q, kl_ref[0, kv])
        elif mode == "window":
            kl = jnp.concatenate([r[0, kv] for r in kwin_refs], axis=0)
            sl = _dot_nt(q, kl) + band_bias
        if mode in ("global", "window"):
            m = jnp.maximum(m, jnp.max(sl, axis=-1, keepdims=True))
        if sink is not None:
            m = jnp.maximum(m, sink)
        acc = _dot(jnp.exp2(sc - m).astype(BF16), vc_ref[0, kv])
        if mode in ("global", "window"):
            vl = vl_ref[0, kv] if mode == "global" else jnp.concatenate([r[0, kv] for r in vwin_refs], axis=0)
            acc = acc + _dot(jnp.exp2(sl - m).astype(BF16), vl)
        den = acc[:, HEAD_DIM:HEAD_DIM + 1]
        if sink is not None:
            den = den + jnp.exp2(sink - m)
        return acc[:, :HEAD_DIM] / den

    outs = []
    for h in range(n_kv * n_g):
        sink = None if sink_ref is None else sink_ref[h] * LOG2E
        outs.append(attend(q_ref[0, h], h // n_g, sink))
    o_ref[0] = jnp.concatenate(outs, axis=-1).astype(o_ref.dtype)


def _attention(q, kc, vc, *, mode, kl=None, vl=None, sinks=None, tq):
    bsz, n_h, sq, hd = q.shape
    n_kv = kc.shape[1]
    n_g = n_h // n_kv
    sc_len = kc.shape[2]
    tq = min(tq, sq)
    nq = sq // tq
    in_specs, args = [], []
    if mode in ("window", "ctx_sink"):
        in_specs.append(pl.BlockSpec(memory_space=pltpu.SMEM))
        args.append(sinks)
    in_specs.append(pl.BlockSpec((1, n_h, tq, hd), lambda b, t: (b, 0, t, 0)))
    args.append(q)
    whole = lambda a: pl.BlockSpec((1,) + a.shape[1:], lambda b, t: (b, 0, 0, 0))
    if mode == "global":
        in_specs += [whole(kl), whole(vl)]
        args += [kl, vl]
    elif mode == "window":
        per_q = tq // WINDOW
        n_blk = sq // WINDOW
        win = lambda a: [pl.BlockSpec((1, n_kv, WINDOW, a.shape[-1]),
                                      lambda b, t, i=i: (b, 0, jnp.clip(t * per_q - 1 + i, 0, n_blk - 1), 0))
                         for i in range(per_q + 2)]
        in_specs += win(kl) + win(vl)
        args += [kl] * (per_q + 2) + [vl] * (per_q + 2)
    in_specs += [whole(kc), whole(vc)]
    args += [kc, vc]
    return pl.pallas_call(
        functools.partial(_attn_kernel, n_g=n_g, tq=tq, mode=mode, s_len=sq),
        out_shape=jax.ShapeDtypeStruct((bsz, sq, n_h * hd), BF16),
        grid=(bsz, nq),
        in_specs=in_specs,
        out_specs=pl.BlockSpec((1, tq, n_h * hd), lambda b, t: (b, t, 0)),
        compiler_params=_cparams(("parallel", "parallel")),
        name="attn_" + mode,
    )(*args)


def _dft_tables(n):
    k = jnp.arange(n, dtype=jnp.int32)[:, None]
    s = jnp.arange(n, dtype=jnp.int32)[None, :]
    ang = ((k * s) % (2 * n)).astype(F32) * (math.pi / n)
    cos = jnp.cos(ang)
    sin = jnp.where(k == 0, jnp.where(s % 2 == 0, 1.0, -1.0), jnp.sin(ang))
    fwd = jnp.concatenate([cos, sin], axis=0).astype(BF16)
    return fwd, fwd.T


def _filter_kernel(band_ref, w1_ref, b1_ref, fr_ref, w2_ref, b2_ref, w3_ref, dl_ref, fs_ref, fd_ref, nrm_ref, *, n):
    pos_i = lax.broadcasted_iota(jnp.int32, (n, LANES), 0)
    lane = lax.broadcasted_iota(jnp.int32, (n, LANES), 1)
    pos = pos_i.astype(F32)
    t = pos / (n - 1)
    ang = band_ref[...] * pos / n
    z = jnp.where(lane == 0, t,
                  jnp.where(lane <= HY_BANDS, jnp.cos(ang),
                            jnp.where(lane <= 2 * HY_BANDS, -jnp.sin(ang), 0.0)))
    freq = fr_ref[...]
    h = jnp.sin(freq * (_dot_hi(z, w1_ref[...]) + b1_ref[...]))
    h = jnp.sin(freq * (_dot_hi(h, w2_ref[...]) + b2_ref[...]))
    h = _dot_hi(h, w3_ref[...])
    tc = t[:, 0:1]
    win = jnp.exp(-tc * dl_ref[...])
    fwd = h[:, :HY_WIDTH] * win
    bwd = jnp.where(pos_i[:, 0:1] == 0, 0.0, h[:, HY_WIDTH:] * win)
    fs_ref[...] = fwd + bwd
    fd_ref[...] = fwd - bwd
    nrm_ref[...] = jnp.sum(jnp.abs(fwd) + jnp.abs(bwd), axis=0, keepdims=True)


def _hyena_filter(n, band_row, w1p, b1, freq, w2, b2, w3, deltas):
    return pl.pallas_call(
        functools.partial(_filter_kernel, n=n),
        out_shape=[jax.ShapeDtypeStruct((n, HY_WIDTH), F32), jax.ShapeDtypeStruct((n, HY_WIDTH), F32),
                   jax.ShapeDtypeStruct((1, HY_WIDTH), F32)],
        compiler_params=pltpu.CompilerParams(vmem_limit_bytes=VMEM_LIMIT),
        name="hyena_filter",
    )(band_row, w1p, b1, freq, w2, b2, w3, deltas)


def _spectrum_kernel(f_ref, x_ref, o_ref):
    o_ref[...] = _dot(f_ref[...], x_ref[...].astype(BF16))


def _filter_spectrum(fwd_tab, x):
    n2, n = fwd_tab.shape
    w = x.shape[1]
    tc = 256
    return pl.pallas_call(
        _spectrum_kernel,
        out_shape=jax.ShapeDtypeStruct((n2, w), F32),
        grid=(w // tc,),
        in_specs=[_const_spec((n2, n)), pl.BlockSpec((n, tc), lambda j: (0, j))],
        out_specs=pl.BlockSpec((n2, tc), lambda j: (0, j)),
        compiler_params=_cparams(("arbitrary",)),
        name="hyena_filter_spectrum",
    )(fwd_tab, x)


def _short_conv(x, w_ref, b_ref):
    n = x.shape[0]
    row = lax.broadcasted_iota(jnp.int32, x.shape, 0)
    prev = jnp.where(row == 0, 0.0, pltpu.roll(x, 1, axis=0))
    nxt = jnp.where(row == n - 1, 0.0, pltpu.roll(x, n - 1, axis=0))
    return w_ref[0:1, :] * prev + w_ref[1:2, :] * x + w_ref[2:3, :] * nxt + b_ref[...]


def _hyena_fwd_kernel(f_ref, x1_ref, v_ref, w1_ref, b1_ref, wv_ref, bv_ref, kr_ref, ka_ref, kn_ref, nrm_ref,
                      y_ref, u_ref, *, n):
    x1 = _short_conv(x1_ref[0], w1_ref, b1_ref)
    v = _short_conv(v_ref[0], wv_ref, bv_ref)
    ub = (v * x1).astype(BF16)
    u_ref[0] = ub
    spec = _dot(f_ref[...], ub)
    ur = spec[0:n]
    ua = spec[n:2 * n]
    row0 = lax.broadcasted_iota(jnp.int32, ur.shape, 0) == 0
    scale = jnp.where(row0, 1.0 / (2 * n), 1.0 / n) / nrm_ref[...]
    kr = kr_ref[...]
    ka = jnp.where(row0, 0.0, ka_ref[...])
    k4 = jnp.where(row0, kn_ref[...], kr)
    y_ref[0, 0:n] = ((ur * kr - ua * ka) * scale).astype(y_ref.dtype)
    y_ref[0, n:2 * n] = ((ur * ka + ua * k4) * scale).astype(y_ref.dtype)


def _hyena_inv_kernel(ft_ref, y_ref, u_ref, x0_ref, w0_ref, b0_ref, skip_ref, o_ref):
    conv = _dot(ft_ref[...], y_ref[0])
    x0 = _short_conv(x0_ref[0], w0_ref, b0_ref)
    o_ref[0] = (x0 * (conv + skip_ref[...] * u_ref[0].astype(F32))).astype(o_ref.dtype)


def _hyena(uh, tabs, spec, knyq, nrm, short_w, short_b, skip):
    fwd_tab, inv_tab = tabs
    bsz, n, _ = uh.shape
    tc = 256
    nct = HY_WIDTH // tc
    chan = lambda off: (lambda c, b: (b, 0, off + c))
    par = lambda off: (lambda c, b: (0, off + c))
    y, u = pl.pallas_call(
        functools.partial(_hyena_fwd_kernel, n=n),
        out_shape=[jax.ShapeDtypeStruct((bsz, 2 * n, HY_WIDTH), BF16), jax.ShapeDtypeStruct((bsz, n, HY_WIDTH), BF16)],
        grid=(nct, bsz),
        in_specs=[
            _const_spec((2 * n, n)),
            pl.BlockSpec((1, n, tc), chan(nct)), pl.BlockSpec((1, n, tc), chan(2 * nct)),
            pl.BlockSpec((3, tc), par(nct)), pl.BlockSpec((1, tc), par(nct)),
            pl.BlockSpec((3, tc), par(2 * nct)), pl.BlockSpec((1, tc), par(2 * nct)),
            pl.BlockSpec((n, tc), lambda c, b: (0, c)), pl.BlockSpec((n, tc), lambda c, b: (1, nct + c)),
            pl.BlockSpec((1, tc), par(0)), pl.BlockSpec((1, tc), par(0)),
        ],
        out_specs=[pl.BlockSpec((1, 2 * n, tc), lambda c, b: (b, 0, c)), pl.BlockSpec((1, n, tc), lambda c, b: (b, 0, c))],
        compiler_params=_cparams(("arbitrary", "arbitrary")),
        name="hyena_fwd",
    )(fwd_tab, uh, uh, short_w, short_b, short_w, short_b, spec, spec, knyq, nrm)
    return pl.pallas_call(
        _hyena_inv_kernel,
        out_shape=jax.ShapeDtypeStruct((bsz, n, HY_WIDTH), BF16),
        grid=(nct, bsz),
        in_specs=[
            _const_spec((n, 2 * n)),
            pl.BlockSpec((1, 2 * n, tc), lambda c, b: (b, 0, c)), pl.BlockSpec((1, n, tc), lambda c, b: (b, 0, c)),
            pl.BlockSpec((1, n, tc), chan(0)), pl.BlockSpec((3, tc), par(0)), pl.BlockSpec((1, tc), par(0)),
            pl.BlockSpec((1, tc), par(0)),
        ],
        out_specs=pl.BlockSpec((1, n, tc), lambda c, b: (b, 0, c)),
        compiler_params=_cparams(("arbitrary", "arbitrary")),
        name="hyena_inv",
    )(inv_tab, y, u, uh, short_w, short_b, skip)


def _merge_kernel(x_ref, mod_ref, ya_ref, yh_ref, yw_ref, gt_ref, wb_ref, wo_ref, g2_ref, rw_ref, rb_ref,
                  xo_ref, tok_ref, route_ref, rt_ref):
    tm = x_ref.shape[1]
    n_part = 2 if tm % 512 == 0 else 1
    for part in range(n_part):
        rows = pl.ds(part * (tm // n_part), tm // n_part)
        _merge_rows(rows, x_ref, mod_ref, ya_ref, yh_ref, yw_ref, gt_ref, wb_ref, wo_ref, g2_ref, rw_ref, rb_ref,
                    xo_ref, tok_ref, route_ref, rt_ref)


def _merge_rows(rows, x_ref, mod_ref, ya_ref, yh_ref, yw_ref, gt_ref, wb_ref, wo_ref, g2_ref, rw_ref, rb_ref,
                xo_ref, tok_ref, route_ref, rt_ref):
    gt = gt_ref[0, rows, :]
    m = gt[:, 0:D_MODEL].astype(F32) * _dot(ya_ref[0, rows, :], wb_ref[0])
    m = m + gt[:, D_MODEL:2 * D_MODEL].astype(F32) * _dot(yh_ref[0, rows, :], wb_ref[1])
    m = m + gt[:, 2 * D_MODEL:3 * D_MODEL].astype(F32) * _dot(yw_ref[0, rows, :], wb_ref[2])
    x = x_ref[0, rows, :] + mod_ref[0, 2:3, :] * _dot(m.astype(BF16), wo_ref[...])
    xo_ref[0, rows, :] = x
    ms = jnp.mean(x * x, axis=-1, keepdims=True)
    tok = x * lax.rsqrt(ms + EPS) * g2_ref[...]
    tok = tok * (1.0 + mod_ref[0, 4:5, :]) + mod_ref[0, 3:4, :]
    tok_ref[0, rows, :] = _pack_bf16_pairs(tok)

    tok_hi = tok.astype(BF16)
    tok_lo = (tok - tok_hi.astype(F32)).astype(BF16)
    lg2 = _dot(tok_hi, rw_ref[...]) + _dot(tok_lo, rw_ref[...])
    lg = lg2[:, :ROUTER_W] + lg2[:, ROUTER_W:] + rb_ref[...]
    lane = lax.broadcasted_iota(jnp.int32, lg.shape, 1).astype(F32)
    big = float(ROUTER_W)

    def first_lane(mask):
        return jnp.min(jnp.where(mask, lane, big), axis=-1, keepdims=True)

    is_g = lane < N_GROUPS
    gmax = jnp.max(jnp.where(is_g, lg, NEG_INF), axis=-1, keepdims=True)
    g_sel = first_lane(jnp.where(is_g, lg, NEG_INF) == gmax)
    p_g = 1.0 / jnp.sum(jnp.where(is_g, jnp.exp(lg - gmax), 0.0), axis=-1, keepdims=True)
    lo = N_GROUPS + g_sel * EXPERTS_PER_GROUP
    in_grp = (lane >= lo) & (lane < lo + EXPERTS_PER_GROUP)
    le = jnp.where(in_grp, lg, NEG_INF)
    v1 = jnp.max(le, axis=-1, keepdims=True)
    i1 = first_lane(le == v1)
    le = jnp.where(lane == i1, NEG_INF, le)
    v2 = jnp.max(le, axis=-1, keepdims=True)
    i2 = first_lane(le == v2)
    e2 = jnp.exp(v2 - v1)
    p1 = p_g / (1.0 + e2)
    p2 = p_g * e2 / (1.0 + e2)
    route = jnp.where(lane == 0, i1 - N_GROUPS, jnp.where(lane == 1, i2 - N_GROUPS,
                      jnp.where(lane == 2, p1, jnp.where(lane == 3, p2, 0.0))))
    route_ref[0, rows, :] = route
    rt_ref[0, :, rows] = route.T[0:ROUTE_T_ROWS, :]


def _merge(x, mod, mod_row, ya, yh, yw, gates, w_branch, w_out, g2, rw, rb):
    bsz, s, d = x.shape
    tm = min(512, s)
    row = lambda b, t: (b, t, 0)
    return pl.pallas_call(
        _merge_kernel,
        out_shape=[jax.ShapeDtypeStruct((bsz, s, d), F32), jax.ShapeDtypeStruct((bsz, s, d // 2), jnp.uint32),
                   jax.ShapeDtypeStruct((bsz, s, ROUTER_W), F32), jax.ShapeDtypeStruct((bsz, ROUTE_T_ROWS, s), F32)],
        grid=(bsz, s // tm),
        in_specs=[
            pl.BlockSpec((1, tm, d), row),
            pl.BlockSpec((1, N_MOD, d), lambda b, t: (mod_row(b), 0, 0)),
            pl.BlockSpec((1, tm, QA_W), row), pl.BlockSpec((1, tm, HY_WIDTH), row), pl.BlockSpec((1, tm, QW_W), row),
            pl.BlockSpec((1, tm, N_BRANCH * d), row),
            _const_spec((N_BRANCH, HY_WIDTH, d)), _const_spec((d, d)), _const_spec((1, d)),
            _const_spec((d, 2 * ROUTER_W)), _const_spec((1, ROUTER_W)),
        ],
        out_specs=[pl.BlockSpec((1, tm, d), row), pl.BlockSpec((1, tm, d // 2), row),
                   pl.BlockSpec((1, tm, ROUTER_W), row), pl.BlockSpec((1, ROUTE_T_ROWS, tm), lambda b, t: (b, 0, t))],
        compiler_params=_cparams(("parallel", "parallel")),
        name="merge_router",
    )(x, mod, ya, yh, yw, gates, w_branch, w_out, g2, rw, rb)


def _route_plan(flat):
    n_tiles = -(-flat.shape[0] // MOE_TM) + N_EXPERTS
    onehot = (flat[:, None] == jnp.arange(N_EXPERTS, dtype=jnp.int32)[None, :]).astype(jnp.int32)
    csum = jnp.cumsum(onehot, axis=0)
    count = csum[-1]
    rank = jnp.sum((csum - onehot) * onehot, axis=1)
    size = ((count + MOE_TM - 1) // MOE_TM) * MOE_TM
    end = jnp.cumsum(size)
    start = end - size
    slot = jnp.sum(onehot * start[None, :], axis=1) + rank
    tile_row = jnp.arange(n_tiles, dtype=jnp.int32) * MOE_TM
    tile_e = jnp.minimum(jnp.sum((tile_row[:, None] >= end[None, :]).astype(jnp.int32), axis=1), N_EXPERTS - 1)
    tile_oh = (tile_e[:, None] == jnp.arange(N_EXPERTS, dtype=jnp.int32)[None, :]).astype(jnp.int32)
    filled = jnp.sum(tile_oh * (start + count)[None, :], axis=1)
    tile_rows = jnp.clip(filled - tile_row, 0, MOE_TM)
    return slot, tile_e, tile_rows, n_tiles


def _sc_mesh():
    info = plsc.get_sparse_core_info()
    mesh = plsc.VectorSubcoreMesh(core_axis_name="c", subcore_axis_name="s")
    return mesh, info.num_cores, info.num_cores * info.num_subcores


def _sc_scatter_rows(sources, slot, n_out):
    mesh, n_cores, n_workers = _sc_mesh()
    n = sum(src.shape[0] for src in sources)
    w = sources[0].shape[1]
    dtype = sources[0].dtype
    n_k = slot.shape[0] // n
    per_worker = n // n_workers
    n_chunks = per_worker // SC_CHUNK
    assert per_worker * n_workers == n and n_chunks * SC_CHUNK == per_worker
    assert all(src.shape[0] % SC_CHUNK == 0 for src in sources)
    n_src = len(sources)

    @functools.partial(
        pl.kernel, mesh=mesh,
        out_type=jax.ShapeDtypeStruct((n_out, w), dtype),
        scratch_types=[pltpu.VMEM((SC_CHUNK,), jnp.int32), pltpu.VMEM((SC_CHUNK, w), dtype), pltpu.SemaphoreType.DMA],
    )
    def scatter(*refs):
        src_hbm = refs[:n_src]
        slot_hbm, out_hbm, idx_v, rows_v, sem = refs[n_src:]
        base = (lax.axis_index("s") * n_cores + lax.axis_index("c")) * per_worker

        @pl.loop(0, n_chunks)
        def _(j):
            off = base + j * SC_CHUNK
            lo = 0
            for src in src_hbm:
                hi = lo + src.shape[0]

                @pl.when((off >= lo) & (off < hi))
                def _(src=src, lo=lo):
                    pltpu.sync_copy(src.at[pl.ds(off - lo, SC_CHUNK)], rows_v)

                lo = hi
            for q in range(n_k):
                pltpu.sync_copy(slot_hbm.at[pl.ds(q * n + off, SC_CHUNK)], idx_v)
                pltpu.async_copy(rows_v, out_hbm.at[idx_v], sem).wait()

    return scatter(*sources, slot)


def _sc_gather_rows(table, idx):
    mesh, n_cores, n_workers = _sc_mesh()
    m = idx.shape[0]
    w = table.shape[1]
    per_worker = m // n_workers
    n_chunks = per_worker // SC_CHUNK
    assert per_worker * n_workers == m and n_chunks * SC_CHUNK == per_worker

    @functools.partial(
        pl.kernel, mesh=mesh,
        out_type=jax.ShapeDtypeStruct((m, w), table.dtype),
        scratch_types=[pltpu.VMEM((SC_CHUNK,), jnp.int32), pltpu.VMEM((SC_CHUNK, w), table.dtype),
                       pltpu.SemaphoreType.DMA],
    )
    def gather(table_hbm, idx_hbm, out_hbm, idx_v, rows_v, sem):
        base = (lax.axis_index("s") * n_cores + lax.axis_index("c")) * per_worker

        @pl.loop(0, n_chunks)
        def _(j):
            off = base + j * SC_CHUNK
            pltpu.sync_copy(idx_hbm.at[pl.ds(off, SC_CHUNK)], idx_v)
            pltpu.async_copy(table_hbm.at[idx_v], rows_v, sem).wait()
            pltpu.sync_copy(rows_v, out_hbm.at[pl.ds(off, SC_CHUNK)])

    return gather(table, idx)


def _experts_kernel(te_ref, tr_ref, x_ref, w1_ref, w3_ref, w2_ref, o_ref, w1_b, w3_b, w2_b):
    t = pl.program_id(0)
    n_rows = tr_ref[t]

    @pl.when((t == 0) | (te_ref[t] != te_ref[jnp.maximum(t - 1, 0)]))
    def _():
        w1_b[...] = w1_ref[0, 0].astype(BF16)
        w3_b[...] = w3_ref[0, 0].astype(BF16)
        w2_b[...] = w2_ref[0, 0].astype(BF16)

    @pl.when(n_rows > 0)
    def _():
        xp = x_ref[...]
        row = lax.broadcasted_iota(jnp.int32, xp.shape, 0)
        x = _unpack_bf16_pairs(jnp.where(row < n_rows, xp, jnp.uint32(0))).astype(BF16)
        h1 = _dot(x, w1_b[...])
        h = (h1 * jax.nn.sigmoid(h1)) * _dot(x, w3_b[...])
        o_ref[...] = _pack_bf16_pairs(_dot(h.astype(BF16), w2_b[...]))

    @pl.when(n_rows == 0)
    def _():
        o_ref[...] = jnp.zeros_like(o_ref)


def _experts(x_sorted, tile_e, tile_rows, layer, w1, w3, w2):
    n_slots, wp = x_sorted.shape
    d = 2 * wp
    wspec = lambda shape: pl.BlockSpec(shape, lambda t, te, tr: (layer, te[t], 0, 0))
    return pl.pallas_call(
        _experts_kernel,
        out_shape=jax.ShapeDtypeStruct((n_slots, wp), jnp.uint32),
        grid_spec=pltpu.PrefetchScalarGridSpec(
            num_scalar_prefetch=2,
            grid=(n_slots // MOE_TM,),
            in_specs=[pl.BlockSpec((MOE_TM, wp), lambda t, te, tr: (t, 0)),
                      wspec((1, 1, d, D_EXPERT)), wspec((1, 1, d, D_EXPERT)), wspec((1, 1, D_EXPERT, d))],
            out_specs=pl.BlockSpec((MOE_TM, wp), lambda t, te, tr: (t, 0)),
            scratch_shapes=[pltpu.VMEM((d, D_EXPERT), BF16), pltpu.VMEM((d, D_EXPERT), BF16),
                            pltpu.VMEM((D_EXPERT, d), BF16)],
        ),
        compiler_params=_cparams(("arbitrary",)),
        name="moe_experts",
    )(tile_e, tile_rows, x_sorted, w1, w3, w2)


def _combine_kernel(x_ref, mod_ref, ya_ref, yb_ref, route_ref, o_ref):
    route = route_ref[0]
    y = route[:, 2:3] * _unpack_bf16_pairs(ya_ref[...]) + route[:, 3:4] * _unpack_bf16_pairs(yb_ref[...])
    o_ref[0] = x_ref[0] + mod_ref[0, 5:6, :] * y


def _combine(x, mod, mod_row, y_tok, route, row_off, n_all):
    bsz, s, d = x.shape
    tm = min(512, s)
    nt = s // tm
    assert row_off % tm == 0 and n_all % tm == 0
    off0 = row_off // tm
    off1 = (n_all + row_off) // tm
    row = lambda b, t: (b, t, 0)
    return pl.pallas_call(
        _combine_kernel,
        out_shape=jax.ShapeDtypeStruct((bsz, s, d), F32),
        grid=(bsz, nt),
        in_specs=[
            pl.BlockSpec((1, tm, d), row),
            pl.BlockSpec((1, N_MOD, d), lambda b, t: (mod_row(b), 0, 0)),
            pl.BlockSpec((tm, d // 2), lambda b, t: (off0 + b * nt + t, 0)),
            pl.BlockSpec((tm, d // 2), lambda b, t: (off1 + b * nt + t, 0)),
            pl.BlockSpec((1, tm, ROUTER_W), row),
        ],
        out_specs=pl.BlockSpec((1, tm, d), row),
        compiler_params=_cparams(("parallel", "parallel")),
        name="moe_combine",
    )(x, mod, y_tok, y_tok, route)


def _moe(streams, mod, layer, w1, w3, w2):
    toks = [st[2].reshape(-1, st[2].shape[-1]) for st in streams]
    n_all = sum(t.shape[0] for t in toks)
    flat = jnp.concatenate([st[4][:, k, :].reshape(-1) for k in range(2) for st in streams]).astype(jnp.int32)
    slot, tile_e, tile_rows, n_tiles = _route_plan(flat)
    x_sorted = _sc_scatter_rows(toks, slot, n_tiles * MOE_TM)
    y_sorted = _experts(x_sorted, tile_e, tile_rows, layer, w1, w3, w2)
    y_tok = _sc_gather_rows(y_sorted, slot)
    outs, row_off = [], 0
    for x1, mod_row, _, route, _ in streams:
        outs.append(_combine(x1, mod, mod_row, y_tok, route, row_off, n_all))
        row_off += x1.shape[0] * x1.shape[1]
    return outs


def _rope_tables(n_tok):
    rows = n_tok // GRID_W
    row_id = jnp.repeat(jnp.arange(rows), GRID_W)
    col_id = jnp.tile(jnp.arange(GRID_W), rows)
    inv_freq = ROPE_THETA ** (-jnp.arange(ROPE_FREQS, dtype=F32) / ROPE_FREQS)
    ang_r = row_id[:, None] * inv_freq
    ang_c = col_id[:, None] * inv_freq
    cos_h = jnp.concatenate([jnp.cos(ang_r), jnp.cos(ang_r), jnp.cos(ang_c), jnp.cos(ang_c)], axis=-1)
    sin_h = jnp.concatenate([-jnp.sin(ang_r), jnp.sin(ang_r), -jnp.sin(ang_c), jnp.sin(ang_c)], axis=-1)
    reps = KA_W // HEAD_DIM
    return jnp.tile(cos_h, (1, reps)), jnp.tile(sin_h, (1, reps))


def kernel(x, c, ctx, c_ctx, ada_w, ada_b, norm1_g, norm2_g, w_in, qk_g, hy_short_w, hy_short_b, hy_f_w1, hy_f_b1,
           hy_f_freq, hy_f_w2, hy_f_b2, hy_f_w3, hy_bias, sinks, w_branch, w_out, router_g_w, router_g_b,
           router_e_w, router_e_b, exp_w1, exp_w3, exp_w2):
    bsz, n_tok, d = x.shape
    n_ctx = ctx.shape[1]
    depth = ada_w.shape[0]
    assert d == D_MODEL and bsz < MOD_ROWS and n_tok % 256 == 0 and n_ctx % 256 == 0

    cc = jnp.zeros((MOD_ROWS, d), F32).at[:bsz].set(c).at[bsz].set(c_ctx)
    mods = _ada_table(cc, ada_w, ada_b)
    ctx_row = lambda b: bsz

    rope_tabs = _rope_tables(n_tok)
    head_ones = (jnp.arange(QA_W)[:, None] // HEAD_DIM == jnp.arange(QA_W)[None, :] // HEAD_DIM).astype(BF16)
    tabs_lat = _dft_tables(n_tok)
    tabs_ctx = _dft_tables(n_ctx)
    bands = jnp.linspace(1e-4, HY_BANDS - 1, HY_BANDS, dtype=F32)
    band_row = jnp.zeros((1, LANES), F32).at[0, 1:1 + HY_BANDS].set(bands).at[0, 1 + HY_BANDS:1 + 2 * HY_BANDS].set(bands)
    band_row = 2 * math.pi * band_row
    deltas = jnp.abs(jnp.linspace(math.log(HY_TARGET) / HY_SLOW_DECAY, math.log(HY_TARGET) / HY_FAST_DECAY,
                                  HY_WIDTH, dtype=F32))[None, :]

    n_split = 1
    per = bsz // n_split
    xs = [x[g * per:(g + 1) * per] for g in range(n_split)]
    xcs = [ctx[g * per:(g + 1) * per] for g in range(n_split)]
    for l in range(depth):
        update_ctx = l < depth - 1
        mod = mods[l]
        w_in_b = w_in[l].astype(BF16)
        gq = [jnp.tile(qk_g[l, 0], A_HEADS)[None], jnp.tile(qk_g[l, 1], A_KV)[None],
              jnp.tile(qk_g[l, 2], W_HEADS)[None], jnp.tile(qk_g[l, 3], W_KV)[None]]
        g1 = norm1_g[l][None]

        w1p = jnp.zeros((LANES, HY_HIDDEN), F32).at[:hy_f_w1.shape[1]].set(hy_f_w1[l])
        fargs = (band_row, w1p, hy_f_b1[l][None], hy_f_freq[l][None], hy_f_w2[l], hy_f_b2[l][None], hy_f_w3[l], deltas)
        skip = hy_bias[l][None]
        sb = hy_short_b[l][None]

        def hyena_filter(n, tabs):
            fsum, fdiff, nrm = _hyena_filter(n, *fargs)
            spec = _filter_spectrum(tabs[0], jnp.concatenate([fsum, fdiff], axis=1))
            return spec, spec[n:n + 1, :HY_WIDTH], nrm

        filt_lat = hyena_filter(n_tok, tabs_lat)
        filt_ctx = hyena_filter(n_ctx, tabs_ctx) if update_ctx else None

        wb = w_branch[l].astype(BF16)
        wo = w_out[l].astype(BF16)
        g2 = norm2_g[l][None]
        rw = jnp.zeros((d, ROUTER_W), F32).at[:, :N_GROUPS].set(router_g_w[l]).at[:, N_GROUPS:N_GROUPS + N_EXPERTS].set(router_e_w[l])
        rb = jnp.zeros((1, ROUTER_W), F32).at[0, :N_GROUPS].set(router_g_b[l]).at[0, N_GROUPS:N_GROUPS + N_EXPERTS].set(router_e_b[l])
        rw_hi = rw.astype(BF16)
        rw = jnp.concatenate([rw_hi, (rw - rw_hi.astype(F32)).astype(BF16)], axis=1)

        for g in range(n_split):
            x, xc = xs[g], xcs[g]
            lat_row = lambda b, g=g: g * per + b
            qa, ka, va, uh, qw, kw, vw, gts = _in_proj(x, mod, lat_row, g1, w_in_b, gq, head_ones, rope_tabs)
            qa_c, ka_c, va_c, uh_c, qw_c, kw_c, vw_c, gts_c = _in_proj(xc, mod, ctx_row, g1, w_in_b, gq, head_ones, None)

            ya = _attention(qa, ka_c, va_c, mode="global", kl=ka, vl=va, tq=1024)
            yw = _attention(qw, kw_c, vw_c, mode="window", kl=kw, vl=vw, sinks=sinks[l], tq=512)
            yh = _hyena(uh, tabs_lat, *filt_lat, hy_short_w[l], sb, skip)
            x1, tok, route, route_t = _merge(x, mod, lat_row, ya, yh, yw, gts, wb, wo, g2, rw, rb)
            streams = [(x1, lat_row, tok, route, route_t)]
            if update_ctx:
                ya_c = _attention(qa_c, ka_c, va_c, mode="ctx", tq=256)
                yw_c = _attention(qw_c, kw_c, vw_c, mode="ctx_sink", sinks=sinks[l], tq=256)
                yh_c = _hyena(uh_c, tabs_ctx, *filt_ctx, hy_short_w[l], sb, skip)
                xc1, tok_c, route_c, route_ct = _merge(xc, mod, ctx_row, ya_c, yh_c, yw_c, gts_c, wb, wo, g2, rw, rb)
                streams.append((xc1, ctx_row, tok_c, route_c, route_ct))
            outs = _moe(streams, mod, l, exp_w1, exp_w3, exp_w2)
            xs[g] = outs[0]
            if update_ctx:
                xcs[g] = outs[1]
    return jnp.concatenate(xs, axis=0) if n_split > 1 else xs[0]
```

```python
import functools
import math

import jax
import jax.numpy as jnp
from jax import lax
from jax.experimental import pallas as pl
from jax.experimental.pallas import tpu as pltpu
from jax.experimental.pallas import tpu_sc as plsc

F32 = jnp.float32
BF16 = jnp.bfloat16
HIGHEST = lax.Precision.HIGHEST

D_MODEL = 1024
GRID_W = 64
HEAD_DIM = 64
ROPE_FREQS = HEAD_DIM // 4
ROPE_THETA = 10000.0
ATTN_SCALE = HEAD_DIM ** -0.5
LOG2E = math.log2(math.e)
Q_SCALE = ATTN_SCALE * LOG2E
V_AUG = 128
NEG_INF = -1e30
WINDOW = 128
A_HEADS = 8
A_KV = 2
W_HEADS = 8
W_KV = 2
HY_WIDTH = 512
HY_BANDS = 16
HY_HIDDEN = 64
HY_TARGET = 1e-2
HY_FAST_DECAY = 0.3
HY_SLOW_DECAY = 1.5
N_BRANCH = 3
N_GROUPS = 4
EXPERTS_PER_GROUP = 4
N_EXPERTS = N_GROUPS * EXPERTS_PER_GROUP
D_EXPERT = 512
N_MOD = 6
EPS = 1e-6

QA_W = A_HEADS * HEAD_DIM
KA_W = A_KV * HEAD_DIM
QW_W = W_HEADS * HEAD_DIM
KW_W = W_KV * HEAD_DIM
O_QA = 0
O_KA = O_QA + QA_W
O_VA = O_KA + KA_W
O_UH = O_VA + KA_W
O_QW = O_UH + 3 * HY_WIDTH
O_KW = O_QW + QW_W
O_VW = O_KW + KW_W
O_GT = O_VW + KW_W
D_IN = O_GT + N_BRANCH * D_MODEL

MOD_ROWS = 24
LANES = 128
ROUTER_W = LANES
ROUTE_T_ROWS = 8
VMEM_LIMIT = 56 * 1024 * 1024
MOE_TM = 1024
SC_CHUNK = 64


def _cparams(sem):
    return pltpu.CompilerParams(dimension_semantics=sem, vmem_limit_bytes=VMEM_LIMIT)


def _const_spec(shape):
    nd = len(shape)
    return pl.BlockSpec(shape, lambda *_: (0,) * nd, pipeline_mode=pl.Buffered(1))


def _dot(a, b):
    return jnp.dot(a, b, preferred_element_type=F32)


def _dot_nt(a, b):
    return lax.dot_general(a, b, (((1,), (1,)), ((), ())), preferred_element_type=F32)


def _dot_hi(a, b):
    return jnp.dot(a, b, precision=HIGHEST, preferred_element_type=F32)


def _pack_bf16_pairs(t):
    half = t.shape[-1] // 2
    hi = lax.bitcast_convert_type(t[:, :half].astype(BF16).astype(F32), jnp.uint32)
    lo = lax.bitcast_convert_type(t[:, half:].astype(BF16).astype(F32), jnp.uint32)
    return hi | (lo >> 16)


def _unpack_bf16_pairs(p):
    hi = lax.bitcast_convert_type(p & jnp.uint32(0xFFFF0000), F32)
    lo = lax.bitcast_convert_type(p << 16, F32)
    return jnp.concatenate([hi, lo], axis=-1)


def _ada_kernel(cc_ref, w_ref, b_ref, o_ref):
    cc = cc_ref[...]
    act = cc * jax.nn.sigmoid(cc)
    o_ref[0] = _dot_hi(act, w_ref[0]) + b_ref[0]


def _ada_table(cc, ada_w, ada_b):
    depth = ada_w.shape[0]
    tn = 1536
    nmod = ada_w.shape[2]
    out = pl.pallas_call(
        _ada_kernel,
        out_shape=jax.ShapeDtypeStruct((depth, MOD_ROWS, nmod), F32),
        grid=(depth, nmod // tn),
        in_specs=[
            pl.BlockSpec((MOD_ROWS, D_MODEL), lambda l, j: (0, 0)),
            pl.BlockSpec((1, D_MODEL, tn), lambda l, j: (l, 0, j)),
            pl.BlockSpec((1, 1, tn), lambda l, j: (l, 0, j)),
        ],
        out_specs=pl.BlockSpec((1, MOD_ROWS, tn), lambda l, j: (l, 0, j)),
        compiler_params=_cparams(("arbitrary", "arbitrary")),
        name="ada_table",
    )(cc, ada_w, ada_b.reshape(depth, 1, nmod))
    return out.reshape(depth, MOD_ROWS, N_MOD, D_MODEL)


def _in_proj_kernel(*refs, rope, fuse):
    refs = list(refs)
    x_ref, mod_ref, g1_ref, w_ref, gqa_ref, gka_ref, gqw_ref, gkw_ref, hs_ref = refs[:9]
    del refs[:9]
    cos_ref = sin_ref = None
    if rope:
        cos_ref, sin_ref = refs[:2]
        del refs[:2]
    moe_refs = None
    if fuse:
        moe_refs = tuple(refs[:4]) + (refs[-1],)
        del refs[:4]
        del refs[-1]
    qa_ref, ka_ref, va_ref, uh_ref, qw_ref, kw_ref, vw_ref, gt_ref = refs
    tm = x_ref.shape[1]
    n_part = 2 if tm % 512 == 0 else 1
    for part in range(n_part):
        rows = pl.ds(part * (tm // n_part), tm // n_part)
        _in_proj_rows(rows, rope, moe_refs, x_ref, mod_ref, g1_ref, w_ref, gqa_ref, gka_ref, gqw_ref, gkw_ref, hs_ref,
                      cos_ref, sin_ref, qa_ref, ka_ref, va_ref, uh_ref, qw_ref, kw_ref, vw_ref, gt_ref)


def _in_proj_rows(rows, rope, moe_refs, x_ref, mod_ref, g1_ref, w_ref, gqa_ref, gka_ref, gqw_ref, gkw_ref, hs_ref,
                  cos_ref, sin_ref, qa_ref, ka_ref, va_ref, uh_ref, qw_ref, kw_ref, vw_ref, gt_ref):
    x = x_ref[0, rows, :]
    if moe_refs is not None:
        modp_ref, ya_ref, yb_ref, route_ref, xo_ref = moe_refs
        route = route_ref[0, rows, :]
        y = route[:, 2:3] * _unpack_bf16_pairs(ya_ref[rows, :]) + route[:, 3:4] * _unpack_bf16_pairs(yb_ref[rows, :])
        x = x + modp_ref[0, 5:6, :] * y
        xo_ref[0, rows, :] = x
    ms = jnp.mean(x * x, axis=-1, keepdims=True)
    h = x * lax.rsqrt(ms + EPS) * g1_ref[...]
    h = (h * (1.0 + mod_ref[0, 1:2, :]) + mod_ref[0, 0:1, :]).astype(BF16)

    def proj(lo, width):
        return _dot(h, w_ref[:, lo:lo + width])

    def head_norm_rope(t, g_ref):
        width = t.shape[-1]
        ssum = _dot((t * t).astype(BF16), hs_ref[0:width, 0:width])
        tn = t * lax.rsqrt(ssum * (1.0 / HEAD_DIM) + EPS) * g_ref[...]
        if rope:
            reps = width // cos_ref.shape[-1]
            cos = cos_ref[rows, :]
            sin = sin_ref[rows, :]
            if reps > 1:
                cos = jnp.concatenate([cos] * reps, axis=-1)
                sin = jnp.concatenate([sin] * reps, axis=-1)
            lane = lax.broadcasted_iota(jnp.int32, tn.shape, 1)
            first = (lane & (2 * ROPE_FREQS - 1)) < ROPE_FREQS
            partner = jnp.where(first, pltpu.roll(tn, width - ROPE_FREQS, axis=1), pltpu.roll(tn, ROPE_FREQS, axis=1))
            tn = tn * cos + partner * sin
        return tn

    def store_heads(o_ref, t):
        for hh in range(t.shape[-1] // HEAD_DIM):
            o_ref[0, hh, rows, :] = t[:, hh * HEAD_DIM:(hh + 1) * HEAD_DIM].astype(o_ref.dtype)

    def store_values(o_ref, t):
        tail = (lax.broadcasted_iota(jnp.int32, (t.shape[0], V_AUG - HEAD_DIM), 1) == 0).astype(F32)
        for hh in range(t.shape[-1] // HEAD_DIM):
            o_ref[0, hh, rows, :] = jnp.concatenate([t[:, hh * HEAD_DIM:(hh + 1) * HEAD_DIM], tail],
                                                    axis=-1).astype(o_ref.dtype)

    gt_ref[0, rows, :] = jax.nn.sigmoid(proj(O_GT, N_BRANCH * D_MODEL)).astype(gt_ref.dtype)
    store_heads(qa_ref, head_norm_rope(proj(O_QA, QA_W), gqa_ref) * Q_SCALE)
    store_heads(qw_ref, head_norm_rope(proj(O_QW, QW_W), gqw_ref) * Q_SCALE)
    store_heads(ka_ref, head_norm_rope(proj(O_KA, KA_W), gka_ref))
    store_heads(kw_ref, head_norm_rope(proj(O_KW, KW_W), gkw_ref))
    store_values(va_ref, proj(O_VA, KA_W))
    store_values(vw_ref, proj(O_VW, KW_W))
    uh_ref[0, rows, :] = proj(O_UH, 3 * HY_WIDTH)


def _in_proj(x, mod, mod_row, g1, w_in, gq, hs, rope_tabs, pending=None):
    bsz, s, d = x.shape
    tm = min(512, s)
    nt = s // tm
    rope = rope_tabs is not None
    in_specs = [
        pl.BlockSpec((1, tm, d), lambda b, t: (b, t, 0)),
        pl.BlockSpec((1, N_MOD, d), lambda b, t: (mod_row(b), 0, 0)),
        _const_spec((1, d)),
        _const_spec((d, D_IN)),
        _const_spec((1, QA_W)), _const_spec((1, KA_W)), _const_spec((1, QW_W)), _const_spec((1, KW_W)),
        _const_spec((QA_W, QA_W)),
    ]
    args = [x, mod, g1, w_in, gq[0], gq[1], gq[2], gq[3], hs]
    if rope:
        in_specs += [pl.BlockSpec((tm, KA_W), lambda b, t: (t, 0))] * 2
        args += list(rope_tabs)

    def head_out(n_heads, width=HEAD_DIM):
        return (jax.ShapeDtypeStruct((bsz, n_heads, s, width), BF16),
                pl.BlockSpec((1, n_heads, tm, width), lambda b, t: (b, 0, t, 0)))

    def flat_out(width, dtype):
        return (jax.ShapeDtypeStruct((bsz, s, width), dtype), pl.BlockSpec((1, tm, width), lambda b, t: (b, t, 0)))

    outs = [head_out(A_HEADS), head_out(A_KV), head_out(A_KV, V_AUG), flat_out(3 * HY_WIDTH, F32),
            head_out(W_HEADS), head_out(W_KV), head_out(W_KV, V_AUG), flat_out(N_BRANCH * D_MODEL, BF16)]
    if pending is not None:
        mod_prev, y_tok, route, row_off, n_all = pending
        assert row_off % tm == 0 and n_all % tm == 0
        off0, off1 = row_off // tm, (n_all + row_off) // tm
        in_specs += [pl.BlockSpec((1, N_MOD, d), lambda b, t: (mod_row(b), 0, 0)),
                     pl.BlockSpec((tm, d // 2), lambda b, t: (off0 + b * nt + t, 0)),
                     pl.BlockSpec((tm, d // 2), lambda b, t: (off1 + b * nt + t, 0)),
                     pl.BlockSpec((1, tm, ROUTER_W), lambda b, t: (b, t, 0))]
        args += [mod_prev, y_tok, y_tok, route]
        outs.append(flat_out(d, F32))
    return pl.pallas_call(
        functools.partial(_in_proj_kernel, rope=rope, fuse=pending is not None),
        out_shape=[o[0] for o in outs],
        grid=(bsz, s // tm),
        in_specs=in_specs,
        out_specs=[o[1] for o in outs],
        compiler_params=_cparams(("parallel", "parallel")),
        name="in_proj",
    )(*args)


def _attn_kernel(*refs, n_g, tq, mode, s_len):
    refs = list(refs)
    sink_ref = refs.pop(0) if mode in ("window", "ctx_sink") else None
    q_ref = refs.pop(0)
    if mode == "global":
        kl_ref, vl_ref = refs.pop(0), refs.pop(0)
    elif mode == "window":
        n_win = tq // WINDOW + 2
        kwin_refs = [refs.pop(0) for _ in range(n_win)]
        vwin_refs = [refs.pop(0) for _ in range(n_win)]
    kc_ref, vc_ref, o_ref = refs
    n_kv = kc_ref.shape[1]
    if mode == "window":
        qt = pl.program_id(1)
        qpos = lax.broadcasted_iota(jnp.int32, (tq, n_win * WINDOW), 0) + qt * tq
        kpos = lax.broadcasted_iota(jnp.int32, (tq, n_win * WINDOW), 1) + (qt * tq - WINDOW)
        valid = (kpos - qpos <= WINDOW) & (qpos - kpos <= WINDOW) & (kpos >= 0) & (kpos < s_len)
        band_bias = jnp.where(valid, 0.0, NEG_INF)

    def attend(q, kv, sink):
        sc = _dot_nt(q, kc_ref[0, kv])
        m = jnp.max(sc, axis=-1, keepdims=True)
        if mode == "global":
            sl = _dot_nt(q, kl_ref[0, kv])
        elif mode == "window":
            kl = jnp.concatenate([r[0, kv] for r in kwin_refs], axis=0)
            sl = _dot_nt(q, kl) + band_bias
        if mode in ("global", "window"):
            m = jnp.maximum(m, jnp.max(sl, axis=-1, keepdims=True))
        if sink is not None:
            m = jnp.maximum(m, sink)
        acc = _dot(jnp.exp2(sc - m).astype(BF16), vc_ref[0, kv])
        if mode in ("global", "window"):
            vl = vl_ref[0, kv] if mode == "global" else jnp.concatenate([r[0, kv] for r in vwin_refs], axis=0)
            acc = acc + _dot(jnp.exp2(sl - m).astype(BF16), vl)
        den = acc[:, HEAD_DIM:HEAD_DIM + 1]
        if sink is not None:
            den = den + jnp.exp2(sink - m)
        return acc[:, :HEAD_DIM] / den

    outs = []
    for h in range(n_kv * n_g):
        sink = None if sink_ref is None else sink_ref[h] * LOG2E
        outs.append(attend(q_ref[0, h], h // n_g, sink))
    o_ref[0] = jnp.concatenate(outs, axis=-1).astype(o_ref.dtype)


def _attention(q, kc, vc, *, mode, kl=None, vl=None, sinks=None, tq):
    bsz, n_h, sq, hd = q.shape
    n_kv = kc.shape[1]
    n_g = n_h // n_kv
    sc_len = kc.shape[2]
    tq = min(tq, sq)
    nq = sq // tq
    in_specs, args = [], []
    if mode in ("window", "ctx_sink"):
        in_specs.append(pl.BlockSpec(memory_space=pltpu.SMEM))
        args.append(sinks)
    in_specs.append(pl.BlockSpec((1, n_h, tq, hd), lambda b, t: (b, 0, t, 0)))
    args.append(q)
    whole = lambda a: pl.BlockSpec((1,) + a.shape[1:], lambda b, t: (b, 0, 0, 0))
    if mode == "global":
        in_specs += [whole(kl), whole(vl)]
        args += [kl, vl]
    elif mode == "window":
        per_q = tq // WINDOW
        n_blk = sq // WINDOW
        win = lambda a: [pl.BlockSpec((1, n_kv, WINDOW, a.shape[-1]),
                                      lambda b, t, i=i: (b, 0, jnp.clip(t * per_q - 1 + i, 0, n_blk - 1), 0))
                         for i in range(per_q + 2)]
        in_specs += win(kl) + win(vl)
        args += [kl] * (per_q + 2) + [vl] * (per_q + 2)
    in_specs += [whole(kc), whole(vc)]
    args += [kc, vc]
    return pl.pallas_call(
        functools.partial(_attn_kernel, n_g=n_g, tq=tq, mode=mode, s_len=sq),
        out_shape=jax.ShapeDtypeStruct((bsz, sq, n_h * hd), BF16),
        grid=(bsz, nq),
        in_specs=in_specs,
        out_specs=pl.BlockSpec((1, tq, n_h * hd), lambda b, t: (b, t, 0)),
        compiler_params=_cparams(("parallel", "parallel")),
        name="attn_" + mode,
    )(*args)


def _dft_tables(n):
    kb = 64
    s = jnp.arange(n, dtype=jnp.int32)[None, :]

    def table(mult, count):
        kk = jnp.arange(count, dtype=jnp.int32)[:, None] * mult
        ang = ((kk * s) % (2 * n)).astype(F32) * (math.pi / n)
        return jnp.cos(ang), jnp.sin(ang)

    ca, sa = table(kb, n // kb)
    cb, sb = table(1, kb)
    cos = (ca[:, None, :] * cb[None] - sa[:, None, :] * sb[None]).reshape(n, n)
    sin = (sa[:, None, :] * cb[None] + ca[:, None, :] * sb[None]).reshape(n, n)
    k = jnp.arange(n, dtype=jnp.int32)[:, None]
    sin = jnp.where(k == 0, jnp.where(s % 2 == 0, 1.0, -1.0), sin)
    fwd = jnp.concatenate([cos, sin], axis=0).astype(BF16)
    return fwd, fwd.T


def _filter_kernel(band_ref, w1_ref, b1_ref, fr_ref, w2_ref, b2_ref, w3_ref, dl_ref, fs_ref, fd_ref, nrm_ref, *, n):
    pos_i = lax.broadcasted_iota(jnp.int32, (n, LANES), 0)
    lane = lax.broadcasted_iota(jnp.int32, (n, LANES), 1)
    pos = pos_i.astype(F32)
    t = pos / (n - 1)
    ang = band_ref[...] * pos / n
    z = jnp.where(lane == 0, t,
                  jnp.where(lane <= HY_BANDS, jnp.cos(ang),
                            jnp.where(lane <= 2 * HY_BANDS, -jnp.sin(ang), 0.0)))
    freq = fr_ref[...]
    h = jnp.sin(freq * (_dot_hi(z, w1_ref[...]) + b1_ref[...]))
    h = jnp.sin(freq * (_dot_hi(h, w2_ref[...]) + b2_ref[...]))
    h = _dot_hi(h, w3_ref[...])
    tc = t[:, 0:1]
    win = jnp.exp(-tc * dl_ref[...])
    fwd = h[:, :HY_WIDTH] * win
    bwd = jnp.where(pos_i[:, 0:1] == 0, 0.0, h[:, HY_WIDTH:] * win)
    fs_ref[...] = fwd + bwd
    fd_ref[...] = fwd - bwd
    nrm_ref[...] = jnp.sum(jnp.abs(fwd) + jnp.abs(bwd), axis=0, keepdims=True)


def _hyena_filter(n, band_row, w1p, b1, freq, w2, b2, w3, deltas):
    return pl.pallas_call(
        functools.partial(_filter_kernel, n=n),
        out_shape=[jax.ShapeDtypeStruct((n, HY_WIDTH), F32), jax.ShapeDtypeStruct((n, HY_WIDTH), F32),
                   jax.ShapeDtypeStruct((1, HY_WIDTH), F32)],
        compiler_params=pltpu.CompilerParams(vmem_limit_bytes=VMEM_LIMIT),
        name="hyena_filter",
    )(band_row, w1p, b1, freq, w2, b2, w3, deltas)


def _spectrum_kernel(f_ref, x_ref, o_ref):
    o_ref[...] = _dot(f_ref[...], x_ref[...].astype(BF16))


def _filter_spectrum(fwd_tab, x):
    n2, n = fwd_tab.shape
    w = x.shape[1]
    tc = 256
    return pl.pallas_call(
        _spectrum_kernel,
        out_shape=jax.ShapeDtypeStruct((n2, w), F32),
        grid=(w // tc,),
        in_specs=[_const_spec((n2, n)), pl.BlockSpec((n, tc), lambda j: (0, j))],
        out_specs=pl.BlockSpec((n2, tc), lambda j: (0, j)),
        compiler_params=_cparams(("arbitrary",)),
        name="hyena_filter_spectrum",
    )(fwd_tab, x)


def _short_conv(x, w_ref, b_ref):
    n = x.shape[0]
    row = lax.broadcasted_iota(jnp.int32, x.shape, 0)
    prev = jnp.where(row == 0, 0.0, pltpu.roll(x, 1, axis=0))
    nxt = jnp.where(row == n - 1, 0.0, pltpu.roll(x, n - 1, axis=0))
    return w_ref[0:1, :] * prev + w_ref[1:2, :] * x + w_ref[2:3, :] * nxt + b_ref[...]


def _hyena_fwd_kernel(f_ref, x1_ref, v_ref, w1_ref, b1_ref, wv_ref, bv_ref, kr_ref, ka_ref, kn_ref, nrm_ref,
                      y_ref, u_ref, *, n):
    x1 = _short_conv(x1_ref[0], w1_ref, b1_ref)
    v = _short_conv(v_ref[0], wv_ref, bv_ref)
    ub = (v * x1).astype(BF16)
    u_ref[0] = ub
    spec = _dot(f_ref[...], ub)
    ur = spec[0:n]
    ua = spec[n:2 * n]
    row0 = lax.broadcasted_iota(jnp.int32, ur.shape, 0) == 0
    scale = jnp.where(row0, 1.0 / (2 * n), 1.0 / n) / nrm_ref[...]
    kr = kr_ref[...]
    ka = jnp.where(row0, 0.0, ka_ref[...])
    k4 = jnp.where(row0, kn_ref[...], kr)
    y_ref[0, 0:n] = ((ur * kr - ua * ka) * scale).astype(y_ref.dtype)
    y_ref[0, n:2 * n] = ((ur * ka + ua * k4) * scale).astype(y_ref.dtype)


def _hyena_inv_kernel(ft_ref, y_ref, u_ref, x0_ref, w0_ref, b0_ref, skip_ref, o_ref):
    conv = _dot(ft_ref[...], y_ref[0])
    x0 = _short_conv(x0_ref[0], w0_ref, b0_ref)
    o_ref[0] = (x0 * (conv + skip_ref[...] * u_ref[0].astype(F32))).astype(o_ref.dtype)


def _hyena(uh, tabs, spec, knyq, nrm, short_w, short_b, skip):
    fwd_tab, inv_tab = tabs
    bsz, n, _ = uh.shape
    tc = 256
    nct = HY_WIDTH // tc
    chan = lambda off: (lambda c, b: (b, 0, off + c))
    par = lambda off: (lambda c, b: (0, off + c))
    y, u = pl.pallas_call(
        functools.partial(_hyena_fwd_kernel, n=n),
        out_shape=[jax.ShapeDtypeStruct((bsz, 2 * n, HY_WIDTH), BF16), jax.ShapeDtypeStruct((bsz, n, HY_WIDTH), BF16)],
        grid=(nct, bsz),
        in_specs=[
            _const_spec((2 * n, n)),
            pl.BlockSpec((1, n, tc), chan(nct)), pl.BlockSpec((1, n, tc), chan(2 * nct)),
            pl.BlockSpec((3, tc), par(nct)), pl.BlockSpec((1, tc), par(nct)),
            pl.BlockSpec((3, tc), par(2 * nct)), pl.BlockSpec((1, tc), par(2 * nct)),
            pl.BlockSpec((n, tc), lambda c, b: (0, c)), pl.BlockSpec((n, tc), lambda c, b: (1, nct + c)),
            pl.BlockSpec((1, tc), par(0)), pl.BlockSpec((1, tc), par(0)),
        ],
        out_specs=[pl.BlockSpec((1, 2 * n, tc), lambda c, b: (b, 0, c)), pl.BlockSpec((1, n, tc), lambda c, b: (b, 0, c))],
        compiler_params=_cparams(("arbitrary", "arbitrary")),
        name="hyena_fwd",
    )(fwd_tab, uh, uh, short_w, short_b, short_w, short_b, spec, spec, knyq, nrm)
    return pl.pallas_call(
        _hyena_inv_kernel,
        out_shape=jax.ShapeDtypeStruct((bsz, n, HY_WIDTH), BF16),
        grid=(nct, bsz),
        in_specs=[
            _const_spec((n, 2 * n)),
            pl.BlockSpec((1, 2 * n, tc), lambda c, b: (b, 0, c)), pl.BlockSpec((1, n, tc), lambda c, b: (b, 0, c)),
            pl.BlockSpec((1, n, tc), chan(0)), pl.BlockSpec((3, tc), par(0)), pl.BlockSpec((1, tc), par(0)),
            pl.BlockSpec((1, tc), par(0)),
        ],
        out_specs=pl.BlockSpec((1, n, tc), lambda c, b: (b, 0, c)),
        compiler_params=_cparams(("arbitrary", "arbitrary")),
        name="hyena_inv",
    )(inv_tab, y, u, uh, short_w, short_b, skip)


def _merge_kernel(x_ref, mod_ref, ya_ref, yh_ref, yw_ref, gt_ref, wb_ref, wo_ref, g2_ref, rw_ref, rb_ref,
                  xo_ref, tok_ref, route_ref, rt_ref):
    tm = x_ref.shape[1]
    n_part = 2 if tm % 512 == 0 else 1
    for part in range(n_part):
        rows = pl.ds(part * (tm // n_part), tm // n_part)
        _merge_rows(rows, x_ref, mod_ref, ya_ref, yh_ref, yw_ref, gt_ref, wb_ref, wo_ref, g2_ref, rw_ref, rb_ref,
                    xo_ref, tok_ref, route_ref, rt_ref)


def _merge_rows(rows, x_ref, mod_ref, ya_ref, yh_ref, yw_ref, gt_ref, wb_ref, wo_ref, g2_ref, rw_ref, rb_ref,
                xo_ref, tok_ref, route_ref, rt_ref):
    gt = gt_ref[0, rows, :]
    m = gt[:, 0:D_MODEL].astype(F32) * _dot(ya_ref[0, rows, :], wb_ref[0])
    m = m + gt[:, D_MODEL:2 * D_MODEL].astype(F32) * _dot(yh_ref[0, rows, :], wb_ref[1])
    m = m + gt[:, 2 * D_MODEL:3 * D_MODEL].astype(F32) * _dot(yw_ref[0, rows, :], wb_ref[2])
    x = x_ref[0, rows, :] + mod_ref[0, 2:3, :] * _dot(m.astype(BF16), wo_ref[...])
    xo_ref[0, rows, :] = x
    ms = jnp.mean(x * x, axis=-1, keepdims=True)
    tok = x * lax.rsqrt(ms + EPS) * g2_ref[...]
    tok = tok * (1.0 + mod_ref[0, 4:5, :]) + mod_ref[0, 3:4, :]
    tok_ref[0, rows, :] = _pack_bf16_pairs(tok)

    tok_hi = tok.astype(BF16)
    tok_lo = (tok - tok_hi.astype(F32)).astype(BF16)
    lg2 = _dot(tok_hi, rw_ref[...]) + _dot(tok_lo, rw_ref[...])
    lg = lg2[:, :ROUTER_W] + lg2[:, ROUTER_W:] + rb_ref[...]
    lane = lax.broadcasted_iota(jnp.int32, lg.shape, 1).astype(F32)
    big = float(ROUTER_W)

    def first_lane(mask):
        return jnp.min(jnp.where(mask, lane, big), axis=-1, keepdims=True)

    is_g = lane < N_GROUPS
    gmax = jnp.max(jnp.where(is_g, lg, NEG_INF), axis=-1, keepdims=True)
    g_sel = first_lane(jnp.where(is_g, lg, NEG_INF) == gmax)
    p_g = 1.0 / jnp.sum(jnp.where(is_g, jnp.exp(lg - gmax), 0.0), axis=-1, keepdims=True)
    lo = N_GROUPS + g_sel * EXPERTS_PER_GROUP
    in_grp = (lane >= lo) & (lane < lo + EXPERTS_PER_GROUP)
    le = jnp.where(in_grp, lg, NEG_INF)
    v1 = jnp.max(le, axis=-1, keepdims=True)
    i1 = first_lane(le == v1)
    le = jnp.where(lane == i1, NEG_INF, le)
    v2 = jnp.max(le, axis=-1, keepdims=True)
    i2 = first_lane(le == v2)
    e2 = jnp.exp(v2 - v1)
    p1 = p_g / (1.0 + e2)
    p2 = p_g * e2 / (1.0 + e2)
    route = jnp.where(lane == 0, i1 - N_GROUPS, jnp.where(lane == 1, i2 - N_GROUPS,
                      jnp.where(lane == 2, p1, jnp.where(lane == 3, p2, 0.0))))
    route_ref[0, rows, :] = route
    rt_ref[0, :, rows] = route.T[0:ROUTE_T_ROWS, :]


def _merge(x, mod, mod_row, ya, yh, yw, gates, w_branch, w_out, g2, rw, rb):
    bsz, s, d = x.shape
    tm = min(512, s)
    row = lambda b, t: (b, t, 0)
    return pl.pallas_call(
        _merge_kernel,
        out_shape=[jax.ShapeDtypeStruct((bsz, s, d), F32), jax.ShapeDtypeStruct((bsz, s, d // 2), jnp.uint32),
                   jax.ShapeDtypeStruct((bsz, s, ROUTER_W), F32), jax.ShapeDtypeStruct((bsz, ROUTE_T_ROWS, s), F32)],
        grid=(bsz, s // tm),
        in_specs=[
            pl.BlockSpec((1, tm, d), row),
            pl.BlockSpec((1, N_MOD, d), lambda b, t: (mod_row(b), 0, 0)),
            pl.BlockSpec((1, tm, QA_W), row), pl.BlockSpec((1, tm, HY_WIDTH), row), pl.BlockSpec((1, tm, QW_W), row),
            pl.BlockSpec((1, tm, N_BRANCH * d), row),
            _const_spec((N_BRANCH, HY_WIDTH, d)), _const_spec((d, d)), _const_spec((1, d)),
            _const_spec((d, 2 * ROUTER_W)), _const_spec((1, ROUTER_W)),
        ],
        out_specs=[pl.BlockSpec((1, tm, d), row), pl.BlockSpec((1, tm, d // 2), row),
                   pl.BlockSpec((1, tm, ROUTER_W), row), pl.BlockSpec((1, ROUTE_T_ROWS, tm), lambda b, t: (b, 0, t))],
        compiler_params=_cparams(("parallel", "parallel")),
        name="merge_router",
    )(x, mod, ya, yh, yw, gates, w_branch, w_out, g2, rw, rb)


def _route_plan(flat):
    n_tiles = -(-flat.shape[0] // MOE_TM) + N_EXPERTS
    onehot = (flat[:, None] == jnp.arange(N_EXPERTS, dtype=jnp.int32)[None, :]).astype(jnp.int32)
    csum = jnp.cumsum(onehot, axis=0)
    count = csum[-1]
    rank = jnp.sum((csum - onehot) * onehot, axis=1)
    size = ((count + MOE_TM - 1) // MOE_TM) * MOE_TM
    end = jnp.cumsum(size)
    start = end - size
    slot = jnp.sum(onehot * start[None, :], axis=1) + rank
    tile_row = jnp.arange(n_tiles, dtype=jnp.int32) * MOE_TM
    tile_e = jnp.minimum(jnp.sum((tile_row[:, None] >= end[None, :]).astype(jnp.int32), axis=1), N_EXPERTS - 1)
    tile_oh = (tile_e[:, None] == jnp.arange(N_EXPERTS, dtype=jnp.int32)[None, :]).astype(jnp.int32)
    filled = jnp.sum(tile_oh * (start + count)[None, :], axis=1)
    tile_rows = jnp.clip(filled - tile_row, 0, MOE_TM)
    return slot, tile_e, tile_rows, n_tiles


def _sc_mesh():
    info = plsc.get_sparse_core_info()
    mesh = plsc.VectorSubcoreMesh(core_axis_name="c", subcore_axis_name="s")
    return mesh, info.num_cores, info.num_cores * info.num_subcores


def _sc_scatter_rows(sources, slot, n_out):
    mesh, n_cores, n_workers = _sc_mesh()
    n = sum(src.shape[0] for src in sources)
    w = sources[0].shape[1]
    dtype = sources[0].dtype
    n_k = slot.shape[0] // n
    per_worker = n // n_workers
    n_chunks = per_worker // SC_CHUNK
    assert per_worker * n_workers == n and n_chunks * SC_CHUNK == per_worker
    assert all(src.shape[0] % SC_CHUNK == 0 for src in sources)
    n_src = len(sources)

    @functools.partial(
        pl.kernel, mesh=mesh,
        out_type=jax.ShapeDtypeStruct((n_out, w), dtype),
        scratch_types=[pltpu.VMEM((SC_CHUNK,), jnp.int32), pltpu.VMEM((SC_CHUNK, w), dtype), pltpu.SemaphoreType.DMA],
    )
    def scatter(*refs):
        src_hbm = refs[:n_src]
        slot_hbm, out_hbm, idx_v, rows_v, sem = refs[n_src:]
        base = (lax.axis_index("s") * n_cores + lax.axis_index("c")) * per_worker

        @pl.loop(0, n_chunks)
        def _(j):
            off = base + j * SC_CHUNK
            lo = 0
            for src in src_hbm:
                hi = lo + src.shape[0]

                @pl.when((off >= lo) & (off < hi))
                def _(src=src, lo=lo):
                    pltpu.sync_copy(src.at[pl.ds(off - lo, SC_CHUNK)], rows_v)

                lo = hi
            for q in range(n_k):
                pltpu.sync_copy(slot_hbm.at[pl.ds(q * n + off, SC_CHUNK)], idx_v)
                pltpu.async_copy(rows_v, out_hbm.at[idx_v], sem).wait()

    return scatter(*sources, slot)


def _sc_gather_rows(table, idx):
    mesh, n_cores, n_workers = _sc_mesh()
    m = idx.shape[0]
    w = table.shape[1]
    per_worker = m // n_workers
    n_chunks = per_worker // SC_CHUNK
    assert per_worker * n_workers == m and n_chunks * SC_CHUNK == per_worker

    @functools.partial(
        pl.kernel, mesh=mesh,
        out_type=jax.ShapeDtypeStruct((m, w), table.dtype),
        scratch_types=[pltpu.VMEM((SC_CHUNK,), jnp.int32), pltpu.VMEM((SC_CHUNK, w), table.dtype),
                       pltpu.SemaphoreType.DMA],
    )
    def gather(table_hbm, idx_hbm, out_hbm, idx_v, rows_v, sem):
        base = (lax.axis_index("s") * n_cores + lax.axis_index("c")) * per_worker

        @pl.loop(0, n_chunks)
        def _(j):
            off = base + j * SC_CHUNK
            pltpu.sync_copy(idx_hbm.at[pl.ds(off, SC_CHUNK)], idx_v)
            pltpu.async_copy(table_hbm.at[idx_v], rows_v, sem).wait()
            pltpu.sync_copy(rows_v, out_hbm.at[pl.ds(off, SC_CHUNK)])

    return gather(table, idx)


def _experts_kernel(te_ref, tr_ref, x_ref, w1_ref, w3_ref, w2_ref, o_ref, w1_b, w3_b, w2_b):
    t = pl.program_id(0)
    n_rows = tr_ref[t]

    @pl.when((t == 0) | (te_ref[t] != te_ref[jnp.maximum(t - 1, 0)]))
    def _():
        w1_b[...] = w1_ref[0, 0].astype(BF16)
        w3_b[...] = w3_ref[0, 0].astype(BF16)
        w2_b[...] = w2_ref[0, 0].astype(BF16)

    @pl.when(n_rows > 0)
    def _():
        xp = x_ref[...]
        row = lax.broadcasted_iota(jnp.int32, xp.shape, 0)
        x = _unpack_bf16_pairs(jnp.where(row < n_rows, xp, jnp.uint32(0))).astype(BF16)
        h1 = _dot(x, w1_b[...])
        h = (h1 * jax.nn.sigmoid(h1)) * _dot(x, w3_b[...])
        o_ref[...] = _pack_bf16_pairs(_dot(h.astype(BF16), w2_b[...]))

    @pl.when(n_rows == 0)
    def _():
        o_ref[...] = jnp.zeros_like(o_ref)


def _experts(x_sorted, tile_e, tile_rows, layer, w1, w3, w2):
    n_slots, wp = x_sorted.shape
    d = 2 * wp
    wspec = lambda shape: pl.BlockSpec(shape, lambda t, te, tr: (layer, te[t], 0, 0))
    return pl.pallas_call(
        _experts_kernel,
        out_shape=jax.ShapeDtypeStruct((n_slots, wp), jnp.uint32),
        grid_spec=pltpu.PrefetchScalarGridSpec(
            num_scalar_prefetch=2,
            grid=(n_slots // MOE_TM,),
            in_specs=[pl.BlockSpec((MOE_TM, wp), lambda t, te, tr: (t, 0)),
                      wspec((1, 1, d, D_EXPERT)), wspec((1, 1, d, D_EXPERT)), wspec((1, 1, D_EXPERT, d))],
            out_specs=pl.BlockSpec((MOE_TM, wp), lambda t, te, tr: (t, 0)),
            scratch_shapes=[pltpu.VMEM((d, D_EXPERT), BF16), pltpu.VMEM((d, D_EXPERT), BF16),
                            pltpu.VMEM((D_EXPERT, d), BF16)],
        ),
        compiler_params=_cparams(("arbitrary",)),
        name="moe_experts",
    )(tile_e, tile_rows, x_sorted, w1, w3, w2)


def _combine_kernel(x_ref, mod_ref, ya_ref, yb_ref, route_ref, o_ref):
    route = route_ref[0]
    y = route[:, 2:3] * _unpack_bf16_pairs(ya_ref[...]) + route[:, 3:4] * _unpack_bf16_pairs(yb_ref[...])
    o_ref[0] = x_ref[0] + mod_ref[0, 5:6, :] * y


def _combine(x, mod, mod_row, y_tok, route, row_off, n_all):
    bsz, s, d = x.shape
    tm = min(512, s)
    nt = s // tm
    assert row_off % tm == 0 and n_all % tm == 0
    off0 = row_off // tm
    off1 = (n_all + row_off) // tm
    row = lambda b, t: (b, t, 0)
    return pl.pallas_call(
        _combine_kernel,
        out_shape=jax.ShapeDtypeStruct((bsz, s, d), F32),
        grid=(bsz, nt),
        in_specs=[
            pl.BlockSpec((1, tm, d), row),
            pl.BlockSpec((1, N_MOD, d), lambda b, t: (mod_row(b), 0, 0)),
            pl.BlockSpec((tm, d // 2), lambda b, t: (off0 + b * nt + t, 0)),
            pl.BlockSpec((tm, d // 2), lambda b, t: (off1 + b * nt + t, 0)),
            pl.BlockSpec((1, tm, ROUTER_W), row),
        ],
        out_specs=pl.BlockSpec((1, tm, d), row),
        compiler_params=_cparams(("parallel", "parallel")),
        name="moe_combine",
    )(x, mod, y_tok, y_tok, route)


def _moe(streams, layer, w1, w3, w2):
    toks = [st[2].reshape(-1, st[2].shape[-1]) for st in streams]
    n_all = sum(t.shape[0] for t in toks)
    flat = jnp.concatenate([st[4][:, k, :].reshape(-1) for k in range(2) for st in streams]).astype(jnp.int32)
    slot, tile_e, tile_rows, n_tiles = _route_plan(flat)
    x_sorted = _sc_scatter_rows(toks, slot, n_tiles * MOE_TM)
    y_sorted = _experts(x_sorted, tile_e, tile_rows, layer, w1, w3, w2)
    return _sc_gather_rows(y_sorted, slot), n_all


def _rope_tables(n_tok):
    rows = n_tok // GRID_W
    row_id = jnp.repeat(jnp.arange(rows), GRID_W)
    col_id = jnp.tile(jnp.arange(GRID_W), rows)
    inv_freq = ROPE_THETA ** (-jnp.arange(ROPE_FREQS, dtype=F32) / ROPE_FREQS)
    ang_r = row_id[:, None] * inv_freq
    ang_c = col_id[:, None] * inv_freq
    cos_h = jnp.concatenate([jnp.cos(ang_r), jnp.cos(ang_r), jnp.cos(ang_c), jnp.cos(ang_c)], axis=-1)
    sin_h = jnp.concatenate([-jnp.sin(ang_r), jnp.sin(ang_r), -jnp.sin(ang_c), jnp.sin(ang_c)], axis=-1)
    reps = KA_W // HEAD_DIM
    return jnp.tile(cos_h, (1, reps)), jnp.tile(sin_h, (1, reps))


def kernel(x, c, ctx, c_ctx, ada_w, ada_b, norm1_g, norm2_g, w_in, qk_g, hy_short_w, hy_short_b, hy_f_w1, hy_f_b1,
           hy_f_freq, hy_f_w2, hy_f_b2, hy_f_w3, hy_bias, sinks, w_branch, w_out, router_g_w, router_g_b,
           router_e_w, router_e_b, exp_w1, exp_w3, exp_w2):
    bsz, n_tok, d = x.shape
    n_ctx = ctx.shape[1]
    depth = ada_w.shape[0]
    assert d == D_MODEL and bsz < MOD_ROWS and n_tok % 256 == 0 and n_ctx % 256 == 0

    cc = jnp.zeros((MOD_ROWS, d), F32).at[:bsz].set(c).at[bsz].set(c_ctx)
    mods = _ada_table(cc, ada_w, ada_b)
    ctx_row = lambda b: bsz

    rope_tabs = _rope_tables(n_tok)
    head_ones = (jnp.arange(QA_W)[:, None] // HEAD_DIM == jnp.arange(QA_W)[None, :] // HEAD_DIM).astype(BF16)
    tabs_lat = _dft_tables(n_tok)
    tabs_ctx = _dft_tables(n_ctx)
    bands = jnp.linspace(1e-4, HY_BANDS - 1, HY_BANDS, dtype=F32)
    band_row = jnp.zeros((1, LANES), F32).at[0, 1:1 + HY_BANDS].set(bands).at[0, 1 + HY_BANDS:1 + 2 * HY_BANDS].set(bands)
    band_row = 2 * math.pi * band_row
    deltas = jnp.abs(jnp.linspace(math.log(HY_TARGET) / HY_SLOW_DECAY, math.log(HY_TARGET) / HY_FAST_DECAY,
                                  HY_WIDTH, dtype=F32))[None, :]

    lat_row = lambda b: b
    xc = ctx
    pend_lat = pend_ctx = None
    for l in range(depth):
        update_ctx = l < depth - 1
        mod = mods[l]
        w_in_b = w_in[l].astype(BF16)
        gq = [jnp.tile(qk_g[l, 0], A_HEADS)[None], jnp.tile(qk_g[l, 1], A_KV)[None],
              jnp.tile(qk_g[l, 2], W_HEADS)[None], jnp.tile(qk_g[l, 3], W_KV)[None]]
        g1 = norm1_g[l][None]

        w1p = jnp.zeros((LANES, HY_HIDDEN), F32).at[:hy_f_w1.shape[1]].set(hy_f_w1[l])
        fargs = (band_row, w1p, hy_f_b1[l][None], hy_f_freq[l][None], hy_f_w2[l], hy_f_b2[l][None], hy_f_w3[l], deltas)
        skip = hy_bias[l][None]
        sb = hy_short_b[l][None]

        def hyena_filter(n, tabs):
            fsum, fdiff, nrm = _hyena_filter(n, *fargs)
            spec = _filter_spectrum(tabs[0], jnp.concatenate([fsum, fdiff], axis=1))
            return spec, spec[n:n + 1, :HY_WIDTH], nrm

        filt_lat = hyena_filter(n_tok, tabs_lat)
        filt_ctx = hyena_filter(n_ctx, tabs_ctx) if update_ctx else None

        wb = w_branch[l].astype(BF16)
        wo = w_out[l].astype(BF16)
        g2 = norm2_g[l][None]
        rw = jnp.zeros((d, ROUTER_W), F32).at[:, :N_GROUPS].set(router_g_w[l]).at[:, N_GROUPS:N_GROUPS + N_EXPERTS].set(router_e_w[l])
        rb = jnp.zeros((1, ROUTER_W), F32).at[0, :N_GROUPS].set(router_g_b[l]).at[0, N_GROUPS:N_GROUPS + N_EXPERTS].set(router_e_b[l])
        rw_hi = rw.astype(BF16)
        rw = jnp.concatenate([rw_hi, (rw - rw_hi.astype(F32)).astype(BF16)], axis=1)

        proj = _in_proj(x, mod, lat_row, g1, w_in_b, gq, head_ones, rope_tabs, pend_lat)
        proj_c = _in_proj(xc, mod, ctx_row, g1, w_in_b, gq, head_ones, None, pend_ctx)
        if pend_lat is not None:
            x = proj[-1]
        if pend_ctx is not None:
            xc = proj_c[-1]
        qa, ka, va, uh, qw, kw, vw, gts = proj[:8]
        qa_c, ka_c, va_c, uh_c, qw_c, kw_c, vw_c, gts_c = proj_c[:8]

        ya = _attention(qa, ka_c, va_c, mode="global", kl=ka, vl=va, tq=1024)
        yw = _attention(qw, kw_c, vw_c, mode="window", kl=kw, vl=vw, sinks=sinks[l], tq=512)
        yh = _hyena(uh, tabs_lat, *filt_lat, hy_short_w[l], sb, skip)
        x, tok, route, route_t = _merge(x, mod, lat_row, ya, yh, yw, gts, wb, wo, g2, rw, rb)
        streams = [(x, lat_row, tok, route, route_t)]
        if update_ctx:
            ya_c = _attention(qa_c, ka_c, va_c, mode="ctx", tq=256)
            yw_c = _attention(qw_c, kw_c, vw_c, mode="ctx_sink", sinks=sinks[l], tq=256)
            yh_c = _hyena(uh_c, tabs_ctx, *filt_ctx, hy_short_w[l], sb, skip)
            xc, tok_c, route_c, route_ct = _merge(xc, mod, ctx_row, ya_c, yh_c, yw_c, gts_c, wb, wo, g2, rw, rb)
            streams.append((xc, ctx_row, tok_c, route_c, route_ct))
        y_tok, n_all = _moe(streams, l, exp_w1, exp_w3, exp_w2)
        pend_lat = (mod, y_tok, route, 0, n_all)
        pend_ctx = (mod, y_tok, route_c, bsz * n_tok, n_all) if update_ctx else None
    return _combine(x, *pend_lat[:1], lat_row, *pend_lat[1:])
```

```python
import functools
import math

import jax
import jax.numpy as jnp
from jax import lax
from jax.experimental import pallas as pl
from jax.experimental.pallas import tpu as pltpu
from jax.experimental.pallas import tpu_sc as plsc

F32 = jnp.float32
BF16 = jnp.bfloat16
HIGHEST = lax.Precision.HIGHEST

D_MODEL = 1024
GRID_W = 64
HEAD_DIM = 64
ROPE_FREQS = HEAD_DIM // 4
ROPE_THETA = 10000.0
ATTN_SCALE = HEAD_DIM ** -0.5
LOG2E = math.log2(math.e)
Q_SCALE = ATTN_SCALE * LOG2E
V_AUG = 128
NEG_INF = -1e30
WINDOW = 128
A_HEADS = 8
A_KV = 2
W_HEADS = 8
W_KV = 2
HY_WIDTH = 512
HY_BANDS = 16
HY_HIDDEN = 64
HY_TARGET = 1e-2
HY_FAST_DECAY = 0.3
HY_SLOW_DECAY = 1.5
N_BRANCH = 3
N_GROUPS = 4
EXPERTS_PER_GROUP = 4
N_EXPERTS = N_GROUPS * EXPERTS_PER_GROUP
D_EXPERT = 512
N_MOD = 6
EPS = 1e-6

QA_W = A_HEADS * HEAD_DIM
KA_W = A_KV * HEAD_DIM
QW_W = W_HEADS * HEAD_DIM
KW_W = W_KV * HEAD_DIM
O_QA = 0
O_KA = O_QA + QA_W
O_VA = O_KA + KA_W
O_UH = O_VA + KA_W
O_QW = O_UH + 3 * HY_WIDTH
O_KW = O_QW + QW_W
O_VW = O_KW + KW_W
O_GT = O_VW + KW_W
D_IN = O_GT + N_BRANCH * D_MODEL

MOD_ROWS = 24
LANES = 128
ROUTER_W = LANES
ROUTE_T_ROWS = 8
VMEM_LIMIT = 56 * 1024 * 1024
MOE_TM = 1024
MOE_TM_SMALL = 256
SC_CHUNK = 64


def _cparams(sem):
    return pltpu.CompilerParams(dimension_semantics=sem, vmem_limit_bytes=VMEM_LIMIT)


def _const_spec(shape):
    nd = len(shape)
    return pl.BlockSpec(shape, lambda *_: (0,) * nd, pipeline_mode=pl.Buffered(1))


def _dot(a, b):
    return jnp.dot(a, b, preferred_element_type=F32)


def _dot_nt(a, b):
    return lax.dot_general(a, b, (((1,), (1,)), ((), ())), preferred_element_type=F32)


def _dot_hi(a, b):
    return jnp.dot(a, b, precision=HIGHEST, preferred_element_type=F32)


def _pack_bf16_pairs(t):
    half = t.shape[-1] // 2
    hi = lax.bitcast_convert_type(t[:, :half].astype(BF16).astype(F32), jnp.uint32)
    lo = lax.bitcast_convert_type(t[:, half:].astype(BF16).astype(F32), jnp.uint32)
    return hi | (lo >> 16)


def _unpack_bf16_pairs(p):
    hi = lax.bitcast_convert_type(p & jnp.uint32(0xFFFF0000), F32)
    lo = lax.bitcast_convert_type(p << 16, F32)
    return jnp.concatenate([hi, lo], axis=-1)


def _ada_kernel(cc_ref, w_ref, b_ref, o_ref):
    cc = cc_ref[...]
    act = cc * jax.nn.sigmoid(cc)
    o_ref[0] = _dot_hi(act, w_ref[0]) + b_ref[0]


def _ada_table(cc, ada_w, ada_b):
    depth = ada_w.shape[0]
    tn = 1536
    nmod = ada_w.shape[2]
    out = pl.pallas_call(
        _ada_kernel,
        out_shape=jax.ShapeDtypeStruct((depth, MOD_ROWS, nmod), F32),
        grid=(depth, nmod // tn),
        in_specs=[
            pl.BlockSpec((MOD_ROWS, D_MODEL), lambda l, j: (0, 0)),
            pl.BlockSpec((1, D_MODEL, tn), lambda l, j: (l, 0, j)),
            pl.BlockSpec((1, 1, tn), lambda l, j: (l, 0, j)),
        ],
        out_specs=pl.BlockSpec((1, MOD_ROWS, tn), lambda l, j: (l, 0, j)),
        compiler_params=_cparams(("arbitrary", "arbitrary")),
        name="ada_table",
    )(cc, ada_w, ada_b.reshape(depth, 1, nmod))
    return out.reshape(depth, MOD_ROWS, N_MOD, D_MODEL)


def _in_proj_kernel(*refs, rope, fuse):
    refs = list(refs)
    x_ref, mod_ref, g1_ref, w_ref, gqa_ref, gka_ref, gqw_ref, gkw_ref, hs_ref = refs[:9]
    del refs[:9]
    cos_ref = sin_ref = None
    if rope:
        cos_ref, sin_ref = refs[:2]
        del refs[:2]
    moe_refs = None
    if fuse:
        moe_refs = tuple(refs[:4]) + (refs[-1],)
        del refs[:4]
        del refs[-1]
    qa_ref, ka_ref, va_ref, uh_ref, qw_ref, kw_ref, vw_ref, gt_ref = refs
    tm = x_ref.shape[1]
    n_part = 2 if tm % 512 == 0 else 1
    for part in range(n_part):
        rows = pl.ds(part * (tm // n_part), tm // n_part)
        _in_proj_rows(rows, rope, moe_refs, x_ref, mod_ref, g1_ref, w_ref, gqa_ref, gka_ref, gqw_ref, gkw_ref, hs_ref,
                      cos_ref, sin_ref, qa_ref, ka_ref, va_ref, uh_ref, qw_ref, kw_ref, vw_ref, gt_ref)


def _in_proj_rows(rows, rope, moe_refs, x_ref, mod_ref, g1_ref, w_ref, gqa_ref, gka_ref, gqw_ref, gkw_ref, hs_ref,
                  cos_ref, sin_ref, qa_ref, ka_ref, va_ref, uh_ref, qw_ref, kw_ref, vw_ref, gt_ref):
    x = x_ref[0, rows, :]
    if moe_refs is not None:
        modp_ref, ya_ref, yb_ref, route_ref, xo_ref = moe_refs
        route = route_ref[0, rows, :]
        y = route[:, 2:3] * _unpack_bf16_pairs(ya_ref[rows, :]) + route[:, 3:4] * _unpack_bf16_pairs(yb_ref[rows, :])
        x = x + modp_ref[0, 5:6, :] * y
        xo_ref[0, rows, :] = x
    ms = jnp.mean(x * x, axis=-1, keepdims=True)
    h = x * lax.rsqrt(ms + EPS) * g1_ref[...]
    h = (h * (1.0 + mod_ref[0, 1:2, :]) + mod_ref[0, 0:1, :]).astype(BF16)

    def proj(lo, width):
        return _dot(h, w_ref[:, lo:lo + width])

    def head_norm_rope(t, g_ref):
        width = t.shape[-1]
        ssum = _dot((t * t).astype(BF16), hs_ref[0:width, 0:width])
        tn = t * lax.rsqrt(ssum * (1.0 / HEAD_DIM) + EPS) * g_ref[...]
        if rope:
            reps = width // cos_ref.shape[-1]
            cos = cos_ref[rows, :]
            sin = sin_ref[rows, :]
            if reps > 1:
                cos = jnp.concatenate([cos] * reps, axis=-1)
                sin = jnp.concatenate([sin] * reps, axis=-1)
            lane = lax.broadcasted_iota(jnp.int32, tn.shape, 1)
            first = (lane & (2 * ROPE_FREQS - 1)) < ROPE_FREQS
            partner = jnp.where(first, pltpu.roll(tn, width - ROPE_FREQS, axis=1), pltpu.roll(tn, ROPE_FREQS, axis=1))
            tn = tn * cos + partner * sin
        return tn

    def store_heads(o_ref, t):
        for hh in range(t.shape[-1] // HEAD_DIM):
            o_ref[0, hh, rows, :] = t[:, hh * HEAD_DIM:(hh + 1) * HEAD_DIM].astype(o_ref.dtype)

    def store_values(o_ref, t):
        tail = (lax.broadcasted_iota(jnp.int32, (t.shape[0], V_AUG - HEAD_DIM), 1) == 0).astype(F32)
        for hh in range(t.shape[-1] // HEAD_DIM):
            o_ref[0, hh, rows, :] = jnp.concatenate([t[:, hh * HEAD_DIM:(hh + 1) * HEAD_DIM], tail],
                                                    axis=-1).astype(o_ref.dtype)

    gt_ref[0, rows, :] = jax.nn.sigmoid(proj(O_GT, N_BRANCH * D_MODEL)).astype(gt_ref.dtype)
    store_heads(qa_ref, head_norm_rope(proj(O_QA, QA_W), gqa_ref) * Q_SCALE)
    store_heads(qw_ref, head_norm_rope(proj(O_QW, QW_W), gqw_ref) * Q_SCALE)
    store_heads(ka_ref, head_norm_rope(proj(O_KA, KA_W), gka_ref))
    store_heads(kw_ref, head_norm_rope(proj(O_KW, KW_W), gkw_ref))
    store_values(va_ref, proj(O_VA, KA_W))
    store_values(vw_ref, proj(O_VW, KW_W))
    uh_ref[0, rows, :] = proj(O_UH, 3 * HY_WIDTH)


def _in_proj(x, mod, mod_row, g1, w_in, gq, hs, rope_tabs, pending=None):
    bsz, s, d = x.shape
    tm = min(512, s)
    nt = s // tm
    rope = rope_tabs is not None
    in_specs = [
        pl.BlockSpec((1, tm, d), lambda b, t: (b, t, 0)),
        pl.BlockSpec((1, N_MOD, d), lambda b, t: (mod_row(b), 0, 0)),
        _const_spec((1, d)),
        _const_spec((d, D_IN)),
        _const_spec((1, QA_W)), _const_spec((1, KA_W)), _const_spec((1, QW_W)), _const_spec((1, KW_W)),
        _const_spec((QA_W, QA_W)),
    ]
    args = [x, mod, g1, w_in, gq[0], gq[1], gq[2], gq[3], hs]
    if rope:
        in_specs += [pl.BlockSpec((tm, KA_W), lambda b, t: (t, 0))] * 2
        args += list(rope_tabs)

    def head_out(n_heads, width=HEAD_DIM):
        return (jax.ShapeDtypeStruct((bsz, n_heads, s, width), BF16),
                pl.BlockSpec((1, n_heads, tm, width), lambda b, t: (b, 0, t, 0)))

    def flat_out(width, dtype):
        return (jax.ShapeDtypeStruct((bsz, s, width), dtype), pl.BlockSpec((1, tm, width), lambda b, t: (b, t, 0)))

    outs = [head_out(A_HEADS), head_out(A_KV), head_out(A_KV, V_AUG), flat_out(3 * HY_WIDTH, F32),
            head_out(W_HEADS), head_out(W_KV), head_out(W_KV, V_AUG), flat_out(N_BRANCH * D_MODEL, BF16)]
    if pending is not None:
        mod_prev, y_tok, route, row_off, n_all = pending
        assert row_off % tm == 0 and n_all % tm == 0
        off0, off1 = row_off // tm, (n_all + row_off) // tm
        in_specs += [pl.BlockSpec((1, N_MOD, d), lambda b, t: (mod_row(b), 0, 0)),
                     pl.BlockSpec((tm, d // 2), lambda b, t: (off0 + b * nt + t, 0)),
                     pl.BlockSpec((tm, d // 2), lambda b, t: (off1 + b * nt + t, 0)),
                     pl.BlockSpec((1, tm, ROUTER_W), lambda b, t: (b, t, 0))]
        args += [mod_prev, y_tok, y_tok, route]
        outs.append(flat_out(d, F32))
    return pl.pallas_call(
        functools.partial(_in_proj_kernel, rope=rope, fuse=pending is not None),
        out_shape=[o[0] for o in outs],
        grid=(bsz, s // tm),
        in_specs=in_specs,
        out_specs=[o[1] for o in outs],
        compiler_params=_cparams(("parallel", "parallel")),
        name="in_proj",
    )(*args)


def _attn_kernel(*refs, n_g, tq, mode, s_len):
    refs = list(refs)
    sink_ref = refs.pop(0) if mode in ("window", "ctx_sink") else None
    q_ref = refs.pop(0)
    if mode == "global":
        kl_ref, vl_ref = refs.pop(0), refs.pop(0)
    elif mode == "window":
        n_win = tq // WINDOW + 2
        kwin_refs = [refs.pop(0) for _ in range(n_win)]
        vwin_refs = [refs.pop(0) for _ in range(n_win)]
    kc_ref, vc_ref, o_ref = refs
    n_kv = kc_ref.shape[1]
    if mode == "window":
        qt = pl.program_id(1)
        qpos = lax.broadcasted_iota(jnp.int32, (tq, n_win * WINDOW), 0) + qt * tq
        kpos = lax.broadcasted_iota(jnp.int32, (tq, n_win * WINDOW), 1) + (qt * tq - WINDOW)
        valid = (kpos - qpos <= WINDOW) & (qpos - kpos <= WINDOW) & (kpos >= 0) & (kpos < s_len)
        band_bias = jnp.where(valid, 0.0, NEG_INF)

    def attend(q, kv, sink):
        sc = _dot_nt(q, kc_ref[0, kv])
        m = jnp.max(sc, axis=-1, keepdims=True)
        if mode == "global":
            sl = _dot_nt(q, kl_ref[0, kv])
        elif mode == "window":
            kl = jnp.concatenate([r[0, kv] for r in kwin_refs], axis=0)
            sl = _dot_nt(q, kl) + band_bias
        if mode in ("global", "window"):
            m = jnp.maximum(m, jnp.max(sl, axis=-1, keepdims=True))
        if sink is not None:
            m = jnp.maximum(m, sink)
        acc = _dot(jnp.exp2(sc - m).astype(BF16), vc_ref[0, kv])
        if mode in ("global", "window"):
            vl = vl_ref[0, kv] if mode == "global" else jnp.concatenate([r[0, kv] for r in vwin_refs], axis=0)
            acc = acc + _dot(jnp.exp2(sl - m).astype(BF16), vl)
        den = acc[:, HEAD_DIM:HEAD_DIM + 1]
        if sink is not None:
            den = den + jnp.exp2(sink - m)
        return acc[:, :HEAD_DIM] / den

    outs = []
    for h in range(n_kv * n_g):
        sink = None if sink_ref is None else sink_ref[h] * LOG2E
        outs.append(attend(q_ref[0, h], h // n_g, sink))
    o_ref[0] = jnp.concatenate(outs, axis=-1).astype(o_ref.dtype)


def _attention(q, kc, vc, *, mode, kl=None, vl=None, sinks=None, tq):
    bsz, n_h, sq, hd = q.shape
    n_kv = kc.shape[1]
    n_g = n_h // n_kv
    sc_len = kc.shape[2]
    tq = min(tq, sq)
    nq = sq // tq
    in_specs, args = [], []
    if mode in ("window", "ctx_sink"):
        in_specs.append(pl.BlockSpec(memory_space=pltpu.SMEM))
        args.append(sinks)
    in_specs.append(pl.BlockSpec((1, n_h, tq, hd), lambda b, t: (b, 0, t, 0)))
    args.append(q)
    whole = lambda a: pl.BlockSpec((1,) + a.shape[1:], lambda b, t: (b, 0, 0, 0))
    if mode == "global":
        in_specs += [whole(kl), whole(vl)]
        args += [kl, vl]
    elif mode == "window":
        per_q = tq // WINDOW
        n_blk = sq // WINDOW
        win = lambda a: [pl.BlockSpec((1, n_kv, WINDOW, a.shape[-1]),
                                      lambda b, t, i=i: (b, 0, jnp.clip(t * per_q - 1 + i, 0, n_blk - 1), 0))
                         for i in range(per_q + 2)]
        in_specs += win(kl) + win(vl)
        args += [kl] * (per_q + 2) + [vl] * (per_q + 2)
    in_specs += [whole(kc), whole(vc)]
    args += [kc, vc]
    return pl.pallas_call(
        functools.partial(_attn_kernel, n_g=n_g, tq=tq, mode=mode, s_len=sq),
        out_shape=jax.ShapeDtypeStruct((bsz, sq, n_h * hd), BF16),
        grid=(bsz, nq),
        in_specs=in_specs,
        out_specs=pl.BlockSpec((1, tq, n_h * hd), lambda b, t: (b, t, 0)),
        compiler_params=_cparams(("parallel", "parallel")),
        name="attn_" + mode,
    )(*args)


def _dft_tables(n):
    kb = 64
    s = jnp.arange(n, dtype=jnp.int32)[None, :]

    def table(mult, count):
        kk = jnp.arange(count, dtype=jnp.int32)[:, None] * mult
        ang = ((kk * s) % (2 * n)).astype(F32) * (math.pi / n)
        return jnp.cos(ang), jnp.sin(ang)

    ca, sa = table(kb, n // kb)
    cb, sb = table(1, kb)
    cos = (ca[:, None, :] * cb[None] - sa[:, None, :] * sb[None]).reshape(n, n)
    sin = (sa[:, None, :] * cb[None] + ca[:, None, :] * sb[None]).reshape(n, n)
    k = jnp.arange(n, dtype=jnp.int32)[:, None]
    sin = jnp.where(k == 0, jnp.where(s % 2 == 0, 1.0, -1.0), sin)
    fwd = jnp.concatenate([cos, sin], axis=0).astype(BF16)
    return fwd, fwd.T


def _filter_kernel(band_ref, w1_ref, b1_ref, fr_ref, w2_ref, b2_ref, w3_ref, dl_ref, fs_ref, fd_ref, nrm_ref, *, n):
    pos_i = lax.broadcasted_iota(jnp.int32, (n, LANES), 0)
    lane = lax.broadcasted_iota(jnp.int32, (n, LANES), 1)
    pos = pos_i.astype(F32)
    t = pos / (n - 1)
    ang = band_ref[...] * pos / n
    z = jnp.where(lane == 0, t,
                  jnp.where(lane <= HY_BANDS, jnp.cos(ang),
                            jnp.where(lane <= 2 * HY_BANDS, -jnp.sin(ang), 0.0)))
    freq = fr_ref[...]
    h = jnp.sin(freq * (_dot_hi(z, w1_ref[...]) + b1_ref[...]))
    h = jnp.sin(freq * (_dot_hi(h, w2_ref[...]) + b2_ref[...]))
    h = _dot_hi(h, w3_ref[...])
    tc = t[:, 0:1]
    win = jnp.exp(-tc * dl_ref[...])
    fwd = h[:, :HY_WIDTH] * win
    bwd = jnp.where(pos_i[:, 0:1] == 0, 0.0, h[:, HY_WIDTH:] * win)
    fs_ref[...] = fwd + bwd
    fd_ref[...] = fwd - bwd
    nrm_ref[...] = jnp.sum(jnp.abs(fwd) + jnp.abs(bwd), axis=0, keepdims=True)


def _hyena_filter(n, band_row, w1p, b1, freq, w2, b2, w3, deltas):
    return pl.pallas_call(
        functools.partial(_filter_kernel, n=n),
        out_shape=[jax.ShapeDtypeStruct((n, HY_WIDTH), F32), jax.ShapeDtypeStruct((n, HY_WIDTH), F32),
                   jax.ShapeDtypeStruct((1, HY_WIDTH), F32)],
        compiler_params=pltpu.CompilerParams(vmem_limit_bytes=VMEM_LIMIT),
        name="hyena_filter",
    )(band_row, w1p, b1, freq, w2, b2, w3, deltas)


def _spectrum_kernel(f_ref, x_ref, o_ref):
    o_ref[...] = _dot(f_ref[...], x_ref[...].astype(BF16))


def _filter_spectrum(fwd_tab, x):
    n2, n = fwd_tab.shape
    w = x.shape[1]
    tc = 256
    return pl.pallas_call(
        _spectrum_kernel,
        out_shape=jax.ShapeDtypeStruct((n2, w), F32),
        grid=(w // tc,),
        in_specs=[_const_spec((n2, n)), pl.BlockSpec((n, tc), lambda j: (0, j))],
        out_specs=pl.BlockSpec((n2, tc), lambda j: (0, j)),
        compiler_params=_cparams(("arbitrary",)),
        name="hyena_filter_spectrum",
    )(fwd_tab, x)


def _short_conv(x, w_ref, b_ref):
    n = x.shape[0]
    row = lax.broadcasted_iota(jnp.int32, x.shape, 0)
    prev = jnp.where(row == 0, 0.0, pltpu.roll(x, 1, axis=0))
    nxt = jnp.where(row == n - 1, 0.0, pltpu.roll(x, n - 1, axis=0))
    return w_ref[0:1, :] * prev + w_ref[1:2, :] * x + w_ref[2:3, :] * nxt + b_ref[...]


def _hyena_fwd_kernel(f_ref, x1_ref, v_ref, w1_ref, b1_ref, wv_ref, bv_ref, kr_ref, ka_ref, kn_ref, nrm_ref,
                      y_ref, u_ref, *, n):
    x1 = _short_conv(x1_ref[0], w1_ref, b1_ref)
    v = _short_conv(v_ref[0], wv_ref, bv_ref)
    ub = (v * x1).astype(BF16)
    u_ref[0] = ub
    spec = _dot(f_ref[...], ub)
    ur = spec[0:n]
    ua = spec[n:2 * n]
    row0 = lax.broadcasted_iota(jnp.int32, ur.shape, 0) == 0
    scale = jnp.where(row0, 1.0 / (2 * n), 1.0 / n) / nrm_ref[...]
    kr = kr_ref[...]
    ka = jnp.where(row0, 0.0, ka_ref[...])
    k4 = jnp.where(row0, kn_ref[...], kr)
    y_ref[0, 0:n] = ((ur * kr - ua * ka) * scale).astype(y_ref.dtype)
    y_ref[0, n:2 * n] = ((ur * ka + ua * k4) * scale).astype(y_ref.dtype)


def _hyena_inv_kernel(ft_ref, y_ref, u_ref, x0_ref, w0_ref, b0_ref, skip_ref, o_ref):
    conv = _dot(ft_ref[...], y_ref[0])
    x0 = _short_conv(x0_ref[0], w0_ref, b0_ref)
    o_ref[0] = (x0 * (conv + skip_ref[...] * u_ref[0].astype(F32))).astype(o_ref.dtype)


def _hyena(uh, tabs, spec, knyq, nrm, short_w, short_b, skip):
    fwd_tab, inv_tab = tabs
    bsz, n, _ = uh.shape
    tc = 256
    nct = HY_WIDTH // tc
    chan = lambda off: (lambda c, b: (b, 0, off + c))
    par = lambda off: (lambda c, b: (0, off + c))
    y, u = pl.pallas_call(
        functools.partial(_hyena_fwd_kernel, n=n),
        out_shape=[jax.ShapeDtypeStruct((bsz, 2 * n, HY_WIDTH), BF16), jax.ShapeDtypeStruct((bsz, n, HY_WIDTH), BF16)],
        grid=(nct, bsz),
        in_specs=[
            _const_spec((2 * n, n)),
            pl.BlockSpec((1, n, tc), chan(nct)), pl.BlockSpec((1, n, tc), chan(2 * nct)),
            pl.BlockSpec((3, tc), par(nct)), pl.BlockSpec((1, tc), par(nct)),
            pl.BlockSpec((3, tc), par(2 * nct)), pl.BlockSpec((1, tc), par(2 * nct)),
            pl.BlockSpec((n, tc), lambda c, b: (0, c)), pl.BlockSpec((n, tc), lambda c, b: (1, nct + c)),
            pl.BlockSpec((1, tc), par(0)), pl.BlockSpec((1, tc), par(0)),
        ],
        out_specs=[pl.BlockSpec((1, 2 * n, tc), lambda c, b: (b, 0, c)), pl.BlockSpec((1, n, tc), lambda c, b: (b, 0, c))],
        compiler_params=_cparams(("arbitrary", "arbitrary")),
        name="hyena_fwd",
    )(fwd_tab, uh, uh, short_w, short_b, short_w, short_b, spec, spec, knyq, nrm)
    return pl.pallas_call(
        _hyena_inv_kernel,
        out_shape=jax.ShapeDtypeStruct((bsz, n, HY_WIDTH), BF16),
        grid=(nct, bsz),
        in_specs=[
            _const_spec((n, 2 * n)),
            pl.BlockSpec((1, 2 * n, tc), lambda c, b: (b, 0, c)), pl.BlockSpec((1, n, tc), lambda c, b: (b, 0, c)),
            pl.BlockSpec((1, n, tc), chan(0)), pl.BlockSpec((3, tc), par(0)), pl.BlockSpec((1, tc), par(0)),
            pl.BlockSpec((1, tc), par(0)),
        ],
        out_specs=pl.BlockSpec((1, n, tc), lambda c, b: (b, 0, c)),
        compiler_params=_cparams(("arbitrary", "arbitrary")),
        name="hyena_inv",
    )(inv_tab, y, u, uh, short_w, short_b, skip)


def _merge_kernel(x_ref, mod_ref, ya_ref, yh_ref, yw_ref, gt_ref, wb_ref, wo_ref, g2_ref, rw_ref, rb_ref,
                  xo_ref, tok_ref, route_ref, rt_ref):
    tm = x_ref.shape[1]
    n_part = 2 if tm % 512 == 0 else 1
    for part in range(n_part):
        rows = pl.ds(part * (tm // n_part), tm // n_part)
        _merge_rows(rows, x_ref, mod_ref, ya_ref, yh_ref, yw_ref, gt_ref, wb_ref, wo_ref, g2_ref, rw_ref, rb_ref,
                    xo_ref, tok_ref, route_ref, rt_ref)


def _merge_rows(rows, x_ref, mod_ref, ya_ref, yh_ref, yw_ref, gt_ref, wb_ref, wo_ref, g2_ref, rw_ref, rb_ref,
                xo_ref, tok_ref, route_ref, rt_ref):
    gt = gt_ref[0, rows, :]
    m = gt[:, 0:D_MODEL].astype(F32) * _dot(ya_ref[0, rows, :], wb_ref[0])
    m = m + gt[:, D_MODEL:2 * D_MODEL].astype(F32) * _dot(yh_ref[0, rows, :], wb_ref[1])
    m = m + gt[:, 2 * D_MODEL:3 * D_MODEL].astype(F32) * _dot(yw_ref[0, rows, :], wb_ref[2])
    x = x_ref[0, rows, :] + mod_ref[0, 2:3, :] * _dot(m.astype(BF16), wo_ref[...])
    xo_ref[0, rows, :] = x
    ms = jnp.mean(x * x, axis=-1, keepdims=True)
    tok = x * lax.rsqrt(ms + EPS) * g2_ref[...]
    tok = tok * (1.0 + mod_ref[0, 4:5, :]) + mod_ref[0, 3:4, :]
    tok_ref[0, rows, :] = _pack_bf16_pairs(tok)

    tok_hi = tok.astype(BF16)
    tok_lo = (tok - tok_hi.astype(F32)).astype(BF16)
    lg2 = _dot(tok_hi, rw_ref[...]) + _dot(tok_lo, rw_ref[...])
    lg = lg2[:, :ROUTER_W] + lg2[:, ROUTER_W:] + rb_ref[...]
    lane = lax.broadcasted_iota(jnp.int32, lg.shape, 1).astype(F32)
    big = float(ROUTER_W)

    def first_lane(mask):
        return jnp.min(jnp.where(mask, lane, big), axis=-1, keepdims=True)

    is_g = lane < N_GROUPS
    gmax = jnp.max(jnp.where(is_g, lg, NEG_INF), axis=-1, keepdims=True)
    g_sel = first_lane(jnp.where(is_g, lg, NEG_INF) == gmax)
    p_g = 1.0 / jnp.sum(jnp.where(is_g, jnp.exp(lg - gmax), 0.0), axis=-1, keepdims=True)
    lo = N_GROUPS + g_sel * EXPERTS_PER_GROUP
    in_grp = (lane >= lo) & (lane < lo + EXPERTS_PER_GROUP)
    le = jnp.where(in_grp, lg, NEG_INF)
    v1 = jnp.max(le, axis=-1, keepdims=True)
    i1 = first_lane(le == v1)
    le = jnp.where(lane == i1, NEG_INF, le)
    v2 = jnp.max(le, axis=-1, keepdims=True)
    i2 = first_lane(le == v2)
    e2 = jnp.exp(v2 - v1)
    p1 = p_g / (1.0 + e2)
    p2 = p_g * e2 / (1.0 + e2)
    route = jnp.where(lane == 0, i1 - N_GROUPS, jnp.where(lane == 1, i2 - N_GROUPS,
                      jnp.where(lane == 2, p1, jnp.where(lane == 3, p2, 0.0))))
    route_ref[0, rows, :] = route
    rt_ref[0, :, rows] = route.T[0:ROUTE_T_ROWS, :]


def _merge(x, mod, mod_row, ya, yh, yw, gates, w_branch, w_out, g2, rw, rb):
    bsz, s, d = x.shape
    tm = min(512, s)
    row = lambda b, t: (b, t, 0)
    return pl.pallas_call(
        _merge_kernel,
        out_shape=[jax.ShapeDtypeStruct((bsz, s, d), F32), jax.ShapeDtypeStruct((bsz, s, d // 2), jnp.uint32),
                   jax.ShapeDtypeStruct((bsz, s, ROUTER_W), F32), jax.ShapeDtypeStruct((bsz, ROUTE_T_ROWS, s), F32)],
        grid=(bsz, s // tm),
        in_specs=[
            pl.BlockSpec((1, tm, d), row),
            pl.BlockSpec((1, N_MOD, d), lambda b, t: (mod_row(b), 0, 0)),
            pl.BlockSpec((1, tm, QA_W), row), pl.BlockSpec((1, tm, HY_WIDTH), row), pl.BlockSpec((1, tm, QW_W), row),
            pl.BlockSpec((1, tm, N_BRANCH * d), row),
            _const_spec((N_BRANCH, HY_WIDTH, d)), _const_spec((d, d)), _const_spec((1, d)),
            _const_spec((d, 2 * ROUTER_W)), _const_spec((1, ROUTER_W)),
        ],
        out_specs=[pl.BlockSpec((1, tm, d), row), pl.BlockSpec((1, tm, d // 2), row),
                   pl.BlockSpec((1, tm, ROUTER_W), row), pl.BlockSpec((1, ROUTE_T_ROWS, tm), lambda b, t: (b, 0, t))],
        compiler_params=_cparams(("parallel", "parallel")),
        name="merge_router",
    )(x, mod, ya, yh, yw, gates, w_branch, w_out, g2, rw, rb)


def _route_plan(flat, tile):
    n_tiles = -(-flat.shape[0] // tile) + N_EXPERTS
    onehot = (flat[:, None] == jnp.arange(N_EXPERTS, dtype=jnp.int32)[None, :]).astype(jnp.int32)
    csum = jnp.cumsum(onehot, axis=0)
    count = csum[-1]
    rank = jnp.sum((csum - onehot) * onehot, axis=1)
    size = ((count + tile - 1) // tile) * tile
    end = jnp.cumsum(size)
    start = end - size
    slot = jnp.sum(onehot * start[None, :], axis=1) + rank
    tile_row = jnp.arange(n_tiles, dtype=jnp.int32) * tile
    tile_e = jnp.minimum(jnp.sum((tile_row[:, None] >= end[None, :]).astype(jnp.int32), axis=1), N_EXPERTS - 1)
    tile_oh = (tile_e[:, None] == jnp.arange(N_EXPERTS, dtype=jnp.int32)[None, :]).astype(jnp.int32)
    filled = jnp.sum(tile_oh * (start + count)[None, :], axis=1)
    tile_rows = jnp.clip(filled - tile_row, 0, tile)
    return slot, tile_e, tile_rows, n_tiles


def _sc_mesh():
    info = plsc.get_sparse_core_info()
    mesh = plsc.VectorSubcoreMesh(core_axis_name="c", subcore_axis_name="s")
    return mesh, info.num_cores, info.num_cores * info.num_subcores


def _sc_scatter_rows(sources, slot, n_out):
    mesh, n_cores, n_workers = _sc_mesh()
    n = sum(src.shape[0] for src in sources)
    w = sources[0].shape[1]
    dtype = sources[0].dtype
    n_k = slot.shape[0] // n
    per_worker = n // n_workers
    n_chunks = per_worker // SC_CHUNK
    assert per_worker * n_workers == n and n_chunks * SC_CHUNK == per_worker
    assert all(src.shape[0] % SC_CHUNK == 0 for src in sources)
    n_src = len(sources)

    @functools.partial(
        pl.kernel, mesh=mesh,
        out_type=jax.ShapeDtypeStruct((n_out, w), dtype),
        scratch_types=[pltpu.VMEM((SC_CHUNK,), jnp.int32), pltpu.VMEM((SC_CHUNK, w), dtype), pltpu.SemaphoreType.DMA],
    )
    def scatter(*refs):
        src_hbm = refs[:n_src]
        slot_hbm, out_hbm, idx_v, rows_v, sem = refs[n_src:]
        base = (lax.axis_index("s") * n_cores + lax.axis_index("c")) * per_worker

        @pl.loop(0, n_chunks)
        def _(j):
            off = base + j * SC_CHUNK
            lo = 0
            for src in src_hbm:
                hi = lo + src.shape[0]

                @pl.when((off >= lo) & (off < hi))
                def _(src=src, lo=lo):
                    pltpu.sync_copy(src.at[pl.ds(off - lo, SC_CHUNK)], rows_v)

                lo = hi
            for q in range(n_k):
                pltpu.sync_copy(slot_hbm.at[pl.ds(q * n + off, SC_CHUNK)], idx_v)
                pltpu.async_copy(rows_v, out_hbm.at[idx_v], sem).wait()

    return scatter(*sources, slot)


def _sc_gather_rows(table, idx):
    mesh, n_cores, n_workers = _sc_mesh()
    m = idx.shape[0]
    w = table.shape[1]
    per_worker = m // n_workers
    n_chunks = per_worker // SC_CHUNK
    assert per_worker * n_workers == m and n_chunks * SC_CHUNK == per_worker

    @functools.partial(
        pl.kernel, mesh=mesh,
        out_type=jax.ShapeDtypeStruct((m, w), table.dtype),
        scratch_types=[pltpu.VMEM((SC_CHUNK,), jnp.int32), pltpu.VMEM((SC_CHUNK, w), table.dtype),
                       pltpu.SemaphoreType.DMA],
    )
    def gather(table_hbm, idx_hbm, out_hbm, idx_v, rows_v, sem):
        base = (lax.axis_index("s") * n_cores + lax.axis_index("c")) * per_worker

        @pl.loop(0, n_chunks)
        def _(j):
            off = base + j * SC_CHUNK
            pltpu.sync_copy(idx_hbm.at[pl.ds(off, SC_CHUNK)], idx_v)
            pltpu.async_copy(table_hbm.at[idx_v], rows_v, sem).wait()
            pltpu.sync_copy(rows_v, out_hbm.at[pl.ds(off, SC_CHUNK)])

    return gather(table, idx)


def _experts_kernel(te_ref, tr_ref, x_ref, w1_ref, w3_ref, w2_ref, o_ref, w1_b, w3_b, w2_b):
    t = pl.program_id(0)
    n_rows = tr_ref[t]

    @pl.when((t == 0) | (te_ref[t] != te_ref[jnp.maximum(t - 1, 0)]))
    def _():
        w1_b[...] = w1_ref[0, 0].astype(BF16)
        w3_b[...] = w3_ref[0, 0].astype(BF16)
        w2_b[...] = w2_ref[0, 0].astype(BF16)

    @pl.when(n_rows > 0)
    def _():
        xp = x_ref[...]
        row = lax.broadcasted_iota(jnp.int32, xp.shape, 0)
        x = _unpack_bf16_pairs(jnp.where(row < n_rows, xp, jnp.uint32(0))).astype(BF16)
        h1 = _dot(x, w1_b[...])
        h = (h1 * jax.nn.sigmoid(h1)) * _dot(x, w3_b[...])
        o_ref[...] = _pack_bf16_pairs(_dot(h.astype(BF16), w2_b[...]))

    @pl.when(n_rows == 0)
    def _():
        o_ref[...] = jnp.zeros_like(o_ref)


def _experts(x_sorted, tile_e, tile_rows, tile, layer, w1, w3, w2):
    n_slots, wp = x_sorted.shape
    d = 2 * wp
    wspec = lambda shape: pl.BlockSpec(shape, lambda t, te, tr: (layer, te[t], 0, 0))
    return pl.pallas_call(
        _experts_kernel,
        out_shape=jax.ShapeDtypeStruct((n_slots, wp), jnp.uint32),
        grid_spec=pltpu.PrefetchScalarGridSpec(
            num_scalar_prefetch=2,
            grid=(n_slots // tile,),
            in_specs=[pl.BlockSpec((tile, wp), lambda t, te, tr: (t, 0)),
                      wspec((1, 1, d, D_EXPERT)), wspec((1, 1, d, D_EXPERT)), wspec((1, 1, D_EXPERT, d))],
            out_specs=pl.BlockSpec((tile, wp), lambda t, te, tr: (t, 0)),
            scratch_shapes=[pltpu.VMEM((d, D_EXPERT), BF16), pltpu.VMEM((d, D_EXPERT), BF16),
                            pltpu.VMEM((D_EXPERT, d), BF16)],
        ),
        compiler_params=_cparams(("arbitrary",)),
        name="moe_experts",
    )(tile_e, tile_rows, x_sorted, w1, w3, w2)


def _combine_kernel(x_ref, mod_ref, ya_ref, yb_ref, route_ref, o_ref):
    route = route_ref[0]
    y = route[:, 2:3] * _unpack_bf16_pairs(ya_ref[...]) + route[:, 3:4] * _unpack_bf16_pairs(yb_ref[...])
    o_ref[0] = x_ref[0] + mod_ref[0, 5:6, :] * y


def _combine(x, mod, mod_row, y_tok, route, row_off, n_all):
    bsz, s, d = x.shape
    tm = min(512, s)
    nt = s // tm
    assert row_off % tm == 0 and n_all % tm == 0
    off0 = row_off // tm
    off1 = (n_all + row_off) // tm
    row = lambda b, t: (b, t, 0)
    return pl.pallas_call(
        _combine_kernel,
        out_shape=jax.ShapeDtypeStruct((bsz, s, d), F32),
        grid=(bsz, nt),
        in_specs=[
            pl.BlockSpec((1, tm, d), row),
            pl.BlockSpec((1, N_MOD, d), lambda b, t: (mod_row(b), 0, 0)),
            pl.BlockSpec((tm, d // 2), lambda b, t: (off0 + b * nt + t, 0)),
            pl.BlockSpec((tm, d // 2), lambda b, t: (off1 + b * nt + t, 0)),
            pl.BlockSpec((1, tm, ROUTER_W), row),
        ],
        out_specs=pl.BlockSpec((1, tm, d), row),
        compiler_params=_cparams(("parallel", "parallel")),
        name="moe_combine",
    )(x, mod, y_tok, y_tok, route)


def _moe(streams, layer, w1, w3, w2):
    toks = [st[2].reshape(-1, st[2].shape[-1]) for st in streams]
    n_all = sum(t.shape[0] for t in toks)
    tile = MOE_TM if 2 * n_all >= N_EXPERTS * 4 * MOE_TM else MOE_TM_SMALL
    flat = jnp.concatenate([st[4][:, k, :].reshape(-1) for k in range(2) for st in streams]).astype(jnp.int32)
    slot, tile_e, tile_rows, n_tiles = _route_plan(flat, tile)
    x_sorted = _sc_scatter_rows(toks, slot, n_tiles * tile)
    y_sorted = _experts(x_sorted, tile_e, tile_rows, tile, layer, w1, w3, w2)
    return _sc_gather_rows(y_sorted, slot), n_all


def _rope_tables(n_tok):
    rows = n_tok // GRID_W
    row_id = jnp.repeat(jnp.arange(rows), GRID_W)
    col_id = jnp.tile(jnp.arange(GRID_W), rows)
    inv_freq = ROPE_THETA ** (-jnp.arange(ROPE_FREQS, dtype=F32) / ROPE_FREQS)
    ang_r = row_id[:, None] * inv_freq
    ang_c = col_id[:, None] * inv_freq
    cos_h = jnp.concatenate([jnp.cos(ang_r), jnp.cos(ang_r), jnp.cos(ang_c), jnp.cos(ang_c)], axis=-1)
    sin_h = jnp.concatenate([-jnp.sin(ang_r), jnp.sin(ang_r), -jnp.sin(ang_c), jnp.sin(ang_c)], axis=-1)
    reps = KA_W // HEAD_DIM
    return jnp.tile(cos_h, (1, reps)), jnp.tile(sin_h, (1, reps))


def kernel(x, c, ctx, c_ctx, ada_w, ada_b, norm1_g, norm2_g, w_in, qk_g, hy_short_w, hy_short_b, hy_f_w1, hy_f_b1,
           hy_f_freq, hy_f_w2, hy_f_b2, hy_f_w3, hy_bias, sinks, w_branch, w_out, router_g_w, router_g_b,
           router_e_w, router_e_b, exp_w1, exp_w3, exp_w2):
    bsz, n_tok, d = x.shape
    n_ctx = ctx.shape[1]
    depth = ada_w.shape[0]
    assert d == D_MODEL and bsz < MOD_ROWS and n_tok % 256 == 0 and n_ctx % 256 == 0

    cc = jnp.zeros((MOD_ROWS, d), F32).at[:bsz].set(c).at[bsz].set(c_ctx)
    mods = _ada_table(cc, ada_w, ada_b)
    ctx_row = lambda b: bsz

    rope_tabs = _rope_tables(n_tok)
    head_ones = (jnp.arange(QA_W)[:, None] // HEAD_DIM == jnp.arange(QA_W)[None, :] // HEAD_DIM).astype(BF16)
    tabs_lat = _dft_tables(n_tok)
    tabs_ctx = _dft_tables(n_ctx)
    bands = jnp.linspace(1e-4, HY_BANDS - 1, HY_BANDS, dtype=F32)
    band_row = jnp.zeros((1, LANES), F32).at[0, 1:1 + HY_BANDS].set(bands).at[0, 1 + HY_BANDS:1 + 2 * HY_BANDS].set(bands)
    band_row = 2 * math.pi * band_row
    deltas = jnp.abs(jnp.linspace(math.log(HY_TARGET) / HY_SLOW_DECAY, math.log(HY_TARGET) / HY_FAST_DECAY,
                                  HY_WIDTH, dtype=F32))[None, :]

    lat_row = lambda b: b
    xc = ctx
    pend_lat = pend_ctx = None
    for l in range(depth):
        update_ctx = l < depth - 1
        mod = mods[l]
        w_in_b = w_in[l].astype(BF16)
        gq = [jnp.tile(qk_g[l, 0], A_HEADS)[None], jnp.tile(qk_g[l, 1], A_KV)[None],
              jnp.tile(qk_g[l, 2], W_HEADS)[None], jnp.tile(qk_g[l, 3], W_KV)[None]]
        g1 = norm1_g[l][None]

        w1p = jnp.zeros((LANES, HY_HIDDEN), F32).at[:hy_f_w1.shape[1]].set(hy_f_w1[l])
        fargs = (band_row, w1p, hy_f_b1[l][None], hy_f_freq[l][None], hy_f_w2[l], hy_f_b2[l][None], hy_f_w3[l], deltas)
        skip = hy_bias[l][None]
        sb = hy_short_b[l][None]

        def hyena_filter(n, tabs):
            fsum, fdiff, nrm = _hyena_filter(n, *fargs)
            spec = _filter_spectrum(tabs[0], jnp.concatenate([fsum, fdiff], axis=1))
            return spec, spec[n:n + 1, :HY_WIDTH], nrm

        filt_lat = hyena_filter(n_tok, tabs_lat)
        filt_ctx = hyena_filter(n_ctx, tabs_ctx) if update_ctx else None

        wb = w_branch[l].astype(BF16)
        wo = w_out[l].astype(BF16)
        g2 = norm2_g[l][None]
        rw = jnp.zeros((d, ROUTER_W), F32).at[:, :N_GROUPS].set(router_g_w[l]).at[:, N_GROUPS:N_GROUPS + N_EXPERTS].set(router_e_w[l])
        rb = jnp.zeros((1, ROUTER_W), F32).at[0, :N_GROUPS].set(router_g_b[l]).at[0, N_GROUPS:N_GROUPS + N_EXPERTS].set(router_e_b[l])
        rw_hi = rw.astype(BF16)
        rw = jnp.concatenate([rw_hi, (rw - rw_hi.astype(F32)).astype(BF16)], axis=1)

        proj = _in_proj(x, mod, lat_row, g1, w_in_b, gq, head_ones, rope_tabs, pend_lat)
        proj_c = _in_proj(xc, mod, ctx_row, g1, w_in_b, gq, head_ones, None, pend_ctx)
        if pend_lat is not None:
            x = proj[-1]
        if pend_ctx is not None:
            xc = proj_c[-1]
        qa, ka, va, uh, qw, kw, vw, gts = proj[:8]
        qa_c, ka_c, va_c, uh_c, qw_c, kw_c, vw_c, gts_c = proj_c[:8]

        ya = _attention(qa, ka_c, va_c, mode="global", kl=ka, vl=va, tq=1024)
        yw = _attention(qw, kw_c, vw_c, mode="window", kl=kw, vl=vw, sinks=sinks[l], tq=512)
        yh = _hyena(uh, tabs_lat, *filt_lat, hy_short_w[l], sb, skip)
        x, tok, route, route_t = _merge(x, mod, lat_row, ya, yh, yw, gts, wb, wo, g2, rw, rb)
        y_tok, n_all = _moe([(x, lat_row, tok, route, route_t)], l, exp_w1, exp_w3, exp_w2)
        pend_lat = (mod, y_tok, route, 0, n_all)
        pend_ctx = None
        if update_ctx:
            ya_c = _attention(qa_c, ka_c, va_c, mode="ctx", tq=256)
            yw_c = _attention(qw_c, kw_c, vw_c, mode="ctx_sink", sinks=sinks[l], tq=256)
            yh_c = _hyena(uh_c, tabs_ctx, *filt_ctx, hy_short_w[l], sb, skip)
            xc, tok_c, route_c, route_ct = _merge(xc, mod, ctx_row, ya_c, yh_c, yw_c, gts_c, wb, wo, g2, rw, rb)
            y_tok_c, n_all_c = _moe([(xc, ctx_row, tok_c, route_c, route_ct)], l, exp_w1, exp_w3, exp_w2)
            pend_ctx = (mod, y_tok_c, route_c, 0, n_all_c)
    return _combine(x, *pend_lat[:1], lat_row, *pend_lat[1:])
```

```python
import functools
import math

import jax
import jax.numpy as jnp
from jax import lax
from jax.experimental import pallas as pl
from jax.experimental.pallas import tpu as pltpu
from jax.experimental.pallas import tpu_sc as plsc

F32 = jnp.float32
BF16 = jnp.bfloat16
HIGHEST = lax.Precision.HIGHEST

D_MODEL = 1024
GRID_W = 64
HEAD_DIM = 64
ROPE_FREQS = HEAD_DIM // 4
ROPE_THETA = 10000.0
ATTN_SCALE = HEAD_DIM ** -0.5
LOG2E = math.log2(math.e)
Q_SCALE = ATTN_SCALE * LOG2E
V_AUG = 128
NEG_INF = -1e30
WINDOW = 128
A_HEADS = 8
A_KV = 2
W_HEADS = 8
W_KV = 2
HY_WIDTH = 512
HY_BANDS = 16
HY_HIDDEN = 64
HY_TARGET = 1e-2
HY_FAST_DECAY = 0.3
HY_SLOW_DECAY = 1.5
N_BRANCH = 3
N_GROUPS = 4
EXPERTS_PER_GROUP = 4
N_EXPERTS = N_GROUPS * EXPERTS_PER_GROUP
D_EXPERT = 512
N_MOD = 6
EPS = 1e-6

QA_W = A_HEADS * HEAD_DIM
KA_W = A_KV * HEAD_DIM
QW_W = W_HEADS * HEAD_DIM
KW_W = W_KV * HEAD_DIM
O_QA = 0
O_KA = O_QA + QA_W
O_VA = O_KA + KA_W
O_UH = O_VA + KA_W
O_QW = O_UH + 3 * HY_WIDTH
O_KW = O_QW + QW_W
O_VW = O_KW + KW_W
O_GT = O_VW + KW_W
D_IN = O_GT + N_BRANCH * D_MODEL

MOD_ROWS = 24
LANES = 128
ROUTER_W = LANES
ROUTE_T_ROWS = 8
VMEM_LIMIT = 56 * 1024 * 1024
MOE_TM = 1024
MOE_TM_SMALL = 512
SC_CHUNK = 64


def _cparams(sem):
    return pltpu.CompilerParams(dimension_semantics=sem, vmem_limit_bytes=VMEM_LIMIT)


def _const_spec(shape):
    nd = len(shape)
    return pl.BlockSpec(shape, lambda *_: (0,) * nd, pipeline_mode=pl.Buffered(1))


def _dot(a, b):
    return jnp.dot(a, b, preferred_element_type=F32)


def _dot_nt(a, b):
    return lax.dot_general(a, b, (((1,), (1,)), ((), ())), preferred_element_type=F32)


def _dot_hi(a, b):
    return jnp.dot(a, b, precision=HIGHEST, preferred_element_type=F32)


def _pack_bf16_pairs(t):
    half = t.shape[-1] // 2
    hi = lax.bitcast_convert_type(t[:, :half].astype(BF16).astype(F32), jnp.uint32)
    lo = lax.bitcast_convert_type(t[:, half:].astype(BF16).astype(F32), jnp.uint32)
    return hi | (lo >> 16)


def _unpack_bf16_pairs(p):
    hi = lax.bitcast_convert_type(p & jnp.uint32(0xFFFF0000), F32)
    lo = lax.bitcast_convert_type(p << 16, F32)
    return jnp.concatenate([hi, lo], axis=-1)


def _ada_kernel(cc_ref, w_ref, b_ref, o_ref):
    cc = cc_ref[...]
    act = cc * jax.nn.sigmoid(cc)
    o_ref[0] = _dot_hi(act, w_ref[0]) + b_ref[0]


def _ada_table(cc, ada_w, ada_b):
    depth = ada_w.shape[0]
    tn = 1536
    nmod = ada_w.shape[2]
    out = pl.pallas_call(
        _ada_kernel,
        out_shape=jax.ShapeDtypeStruct((depth, MOD_ROWS, nmod), F32),
        grid=(depth, nmod // tn),
        in_specs=[
            pl.BlockSpec((MOD_ROWS, D_MODEL), lambda l, j: (0, 0)),
            pl.BlockSpec((1, D_MODEL, tn), lambda l, j: (l, 0, j)),
            pl.BlockSpec((1, 1, tn), lambda l, j: (l, 0, j)),
        ],
        out_specs=pl.BlockSpec((1, MOD_ROWS, tn), lambda l, j: (l, 0, j)),
        compiler_params=_cparams(("arbitrary", "arbitrary")),
        name="ada_table",
    )(cc, ada_w, ada_b.reshape(depth, 1, nmod))
    return out.reshape(depth, MOD_ROWS, N_MOD, D_MODEL)


def _in_proj_kernel(*refs, rope, fuse):
    refs = list(refs)
    x_ref, mod_ref, g1_ref, w_ref, gqa_ref, gka_ref, gqw_ref, gkw_ref, hs_ref = refs[:9]
    del refs[:9]
    cos_ref = sin_ref = None
    if rope:
        cos_ref, sin_ref = refs[:2]
        del refs[:2]
    moe_refs = None
    if fuse:
        moe_refs = tuple(refs[:4]) + (refs[-1],)
        del refs[:4]
        del refs[-1]
    qa_ref, ka_ref, va_ref, uh_ref, qw_ref, kw_ref, vw_ref, gt_ref = refs
    tm = x_ref.shape[1]
    n_part = 2 if tm % 512 == 0 else 1
    for part in range(n_part):
        rows = pl.ds(part * (tm // n_part), tm // n_part)
        _in_proj_rows(rows, rope, moe_refs, x_ref, mod_ref, g1_ref, w_ref, gqa_ref, gka_ref, gqw_ref, gkw_ref, hs_ref,
                      cos_ref, sin_ref, qa_ref, ka_ref, va_ref, uh_ref, qw_ref, kw_ref, vw_ref, gt_ref)


def _in_proj_rows(rows, rope, moe_refs, x_ref, mod_ref, g1_ref, w_ref, gqa_ref, gka_ref, gqw_ref, gkw_ref, hs_ref,
                  cos_ref, sin_ref, qa_ref, ka_ref, va_ref, uh_ref, qw_ref, kw_ref, vw_ref, gt_ref):
    x = x_ref[0, rows, :]
    if moe_refs is not None:
        modp_ref, ya_ref, yb_ref, route_ref, xo_ref = moe_refs
        route = route_ref[0, rows, :]
        y = route[:, 2:3] * _unpack_bf16_pairs(ya_ref[rows, :]) + route[:, 3:4] * _unpack_bf16_pairs(yb_ref[rows, :])
        x = x + modp_ref[0, 5:6, :] * y
        xo_ref[0, rows, :] = x
    ms = jnp.mean(x * x, axis=-1, keepdims=True)
    h = x * lax.rsqrt(ms + EPS) * g1_ref[...]
    h = (h * (1.0 + mod_ref[0, 1:2, :]) + mod_ref[0, 0:1, :]).astype(BF16)

    def proj(lo, width):
        return _dot(h, w_ref[:, lo:lo + width])

    def head_norm_rope(t, g_ref):
        width = t.shape[-1]
        ssum = _dot((t * t).astype(BF16), hs_ref[0:width, 0:width])
        tn = t * lax.rsqrt(ssum * (1.0 / HEAD_DIM) + EPS) * g_ref[...]
        if rope:
            reps = width // cos_ref.shape[-1]
            cos = cos_ref[rows, :]
            sin = sin_ref[rows, :]
            if reps > 1:
                cos = jnp.concatenate([cos] * reps, axis=-1)
                sin = jnp.concatenate([sin] * reps, axis=-1)
            lane = lax.broadcasted_iota(jnp.int32, tn.shape, 1)
            first = (lane & (2 * ROPE_FREQS - 1)) < ROPE_FREQS
            partner = jnp.where(first, pltpu.roll(tn, width - ROPE_FREQS, axis=1), pltpu.roll(tn, ROPE_FREQS, axis=1))
            tn = tn * cos + partner * sin
        return tn

    def store_heads(o_ref, t):
        for hh in range(t.shape[-1] // HEAD_DIM):
            o_ref[0, hh, rows, :] = t[:, hh * HEAD_DIM:(hh + 1) * HEAD_DIM].astype(o_ref.dtype)

    def store_values(o_ref, t):
        tail = (lax.broadcasted_iota(jnp.int32, (t.shape[0], V_AUG - HEAD_DIM), 1) == 0).astype(F32)
        for hh in range(t.shape[-1] // HEAD_DIM):
            o_ref[0, hh, rows, :] = jnp.concatenate([t[:, hh * HEAD_DIM:(hh + 1) * HEAD_DIM], tail],
                                                    axis=-1).astype(o_ref.dtype)

    gt_ref[0, rows, :] = jax.nn.sigmoid(proj(O_GT, N_BRANCH * D_MODEL)).astype(gt_ref.dtype)
    store_heads(qa_ref, head_norm_rope(proj(O_QA, QA_W), gqa_ref) * Q_SCALE)
    store_heads(qw_ref, head_norm_rope(proj(O_QW, QW_W), gqw_ref) * Q_SCALE)
    store_heads(ka_ref, head_norm_rope(proj(O_KA, KA_W), gka_ref))
    store_heads(kw_ref, head_norm_rope(proj(O_KW, KW_W), gkw_ref))
    store_values(va_ref, proj(O_VA, KA_W))
    store_values(vw_ref, proj(O_VW, KW_W))
    uh_ref[0, rows, :] = proj(O_UH, 3 * HY_WIDTH)


def _in_proj(x, mod, mod_row, g1, w_in, gq, hs, rope_tabs, pending=None):
    bsz, s, d = x.shape
    tm = min(512, s)
    nt = s // tm
    rope = rope_tabs is not None
    in_specs = [
        pl.BlockSpec((1, tm, d), lambda b, t: (b, t, 0)),
        pl.BlockSpec((1, N_MOD, d), lambda b, t: (mod_row(b), 0, 0)),
        _const_spec((1, d)),
        _const_spec((d, D_IN)),
        _const_spec((1, QA_W)), _const_spec((1, KA_W)), _const_spec((1, QW_W)), _const_spec((1, KW_W)),
        _const_spec((QA_W, QA_W)),
    ]
    args = [x, mod, g1, w_in, gq[0], gq[1], gq[2], gq[3], hs]
    if rope:
        in_specs += [pl.BlockSpec((tm, KA_W), lambda b, t: (t, 0))] * 2
        args += list(rope_tabs)

    def head_out(n_heads, width=HEAD_DIM):
        return (jax.ShapeDtypeStruct((bsz, n_heads, s, width), BF16),
                pl.BlockSpec((1, n_heads, tm, width), lambda b, t: (b, 0, t, 0)))

    def flat_out(width, dtype):
        return (jax.ShapeDtypeStruct((bsz, s, width), dtype), pl.BlockSpec((1, tm, width), lambda b, t: (b, t, 0)))

    outs = [head_out(A_HEADS), head_out(A_KV), head_out(A_KV, V_AUG), flat_out(3 * HY_WIDTH, F32),
            head_out(W_HEADS), head_out(W_KV), head_out(W_KV, V_AUG), flat_out(N_BRANCH * D_MODEL, BF16)]
    if pending is not None:
        mod_prev, y_tok, route, row_off, n_all = pending
        assert row_off % tm == 0 and n_all % tm == 0
        off0, off1 = row_off // tm, (n_all + row_off) // tm
        in_specs += [pl.BlockSpec((1, N_MOD, d), lambda b, t: (mod_row(b), 0, 0)),
                     pl.BlockSpec((tm, d // 2), lambda b, t: (off0 + b * nt + t, 0)),
                     pl.BlockSpec((tm, d // 2), lambda b, t: (off1 + b * nt + t, 0)),
                     pl.BlockSpec((1, tm, ROUTER_W), lambda b, t: (b, t, 0))]
        args += [mod_prev, y_tok, y_tok, route]
        outs.append(flat_out(d, F32))
    return pl.pallas_call(
        functools.partial(_in_proj_kernel, rope=rope, fuse=pending is not None),
        out_shape=[o[0] for o in outs],
        grid=(bsz, s // tm),
        in_specs=in_specs,
        out_specs=[o[1] for o in outs],
        compiler_params=_cparams(("parallel", "parallel")),
        name="in_proj",
    )(*args)


def _attn_kernel(*refs, n_g, tq, mode, s_len):
    refs = list(refs)
    sink_ref = refs.pop(0) if mode in ("window", "ctx_sink") else None
    q_ref = refs.pop(0)
    if mode == "global":
        kl_ref, vl_ref = refs.pop(0), refs.pop(0)
    elif mode == "window":
        n_win = tq // WINDOW + 2
        kwin_refs = [refs.pop(0) for _ in range(n_win)]
        vwin_refs = [refs.pop(0) for _ in range(n_win)]
    kc_ref, vc_ref, o_ref = refs
    n_kv = kc_ref.shape[1]
    if mode == "window":
        qt = pl.program_id(1)
        qpos = lax.broadcasted_iota(jnp.int32, (tq, n_win * WINDOW), 0) + qt * tq
        kpos = lax.broadcasted_iota(jnp.int32, (tq, n_win * WINDOW), 1) + (qt * tq - WINDOW)
        valid = (kpos - qpos <= WINDOW) & (qpos - kpos <= WINDOW) & (kpos >= 0) & (kpos < s_len)
        band_bias = jnp.where(valid, 0.0, NEG_INF)

    def attend(q, kv, sink):
        sc = _dot_nt(q, kc_ref[0, kv])
        m = jnp.max(sc, axis=-1, keepdims=True)
        if mode == "global":
            sl = _dot_nt(q, kl_ref[0, kv])
        elif mode == "window":
            kl = jnp.concatenate([r[0, kv] for r in kwin_refs], axis=0)
            sl = _dot_nt(q, kl) + band_bias
        if mode in ("global", "window"):
            m = jnp.maximum(m, jnp.max(sl, axis=-1, keepdims=True))
        if sink is not None:
            m = jnp.maximum(m, sink)
        acc = _dot(jnp.exp2(sc - m).astype(BF16), vc_ref[0, kv])
        if mode in ("global", "window"):
            vl = vl_ref[0, kv] if mode == "global" else jnp.concatenate([r[0, kv] for r in vwin_refs], axis=0)
            acc = acc + _dot(jnp.exp2(sl - m).astype(BF16), vl)
        den = acc[:, HEAD_DIM:HEAD_DIM + 1]
        if sink is not None:
            den = den + jnp.exp2(sink - m)
        return acc[:, :HEAD_DIM] / den

    outs = []
    for h in range(n_kv * n_g):
        sink = None if sink_ref is None else sink_ref[h] * LOG2E
        outs.append(attend(q_ref[0, h], h // n_g, sink))
    o_ref[0] = jnp.concatenate(outs, axis=-1).astype(o_ref.dtype)


def _attention(q, kc, vc, *, mode, kl=None, vl=None, sinks=None, tq):
    bsz, n_h, sq, hd = q.shape
    n_kv = kc.shape[1]
    n_g = n_h // n_kv
    sc_len = kc.shape[2]
    tq = min(tq, sq)
    nq = sq // tq
    in_specs, args = [], []
    if mode in ("window", "ctx_sink"):
        in_specs.append(pl.BlockSpec(memory_space=pltpu.SMEM))
        args.append(sinks)
    in_specs.append(pl.BlockSpec((1, n_h, tq, hd), lambda b, t: (b, 0, t, 0)))
    args.append(q)
    whole = lambda a: pl.BlockSpec((1,) + a.shape[1:], lambda b, t: (b, 0, 0, 0))
    if mode == "global":
        in_specs += [whole(kl), whole(vl)]
        args += [kl, vl]
    elif mode == "window":
        per_q = tq // WINDOW
        n_blk = sq // WINDOW
        win = lambda a: [pl.BlockSpec((1, n_kv, WINDOW, a.shape[-1]),
                                      lambda b, t, i=i: (b, 0, jnp.clip(t * per_q - 1 + i, 0, n_blk - 1), 0))
                         for i in range(per_q + 2)]
        in_specs += win(kl) + win(vl)
        args += [kl] * (per_q + 2) + [vl] * (per_q + 2)
    in_specs += [whole(kc), whole(vc)]
    args += [kc, vc]
    return pl.pallas_call(
        functools.partial(_attn_kernel, n_g=n_g, tq=tq, mode=mode, s_len=sq),
        out_shape=jax.ShapeDtypeStruct((bsz, sq, n_h * hd), BF16),
        grid=(bsz, nq),
        in_specs=in_specs,
        out_specs=pl.BlockSpec((1, tq, n_h * hd), lambda b, t: (b, t, 0)),
        compiler_params=_cparams(("parallel", "parallel")),
        name="attn_" + mode,
    )(*args)


def _dft_tables(n):
    kb = 64
    s = jnp.arange(n, dtype=jnp.int32)[None, :]

    def table(mult, count):
        kk = jnp.arange(count, dtype=jnp.int32)[:, None] * mult
        ang = ((kk * s) % (2 * n)).astype(F32) * (math.pi / n)
        return jnp.cos(ang), jnp.sin(ang)

    ca, sa = table(kb, n // kb)
    cb, sb = table(1, kb)
    cos = (ca[:, None, :] * cb[None] - sa[:, None, :] * sb[None]).reshape(n, n)
    sin = (sa[:, None, :] * cb[None] + ca[:, None, :] * sb[None]).reshape(n, n)
    k = jnp.arange(n, dtype=jnp.int32)[:, None]
    sin = jnp.where(k == 0, jnp.where(s % 2 == 0, 1.0, -1.0), sin)
    fwd = jnp.concatenate([cos, sin], axis=0).astype(BF16)
    return fwd, fwd.T


def _filter_kernel(band_ref, w1_ref, b1_ref, fr_ref, w2_ref, b2_ref, w3_ref, dl_ref, fs_ref, fd_ref, nrm_ref, *, n):
    pos_i = lax.broadcasted_iota(jnp.int32, (n, LANES), 0)
    lane = lax.broadcasted_iota(jnp.int32, (n, LANES), 1)
    pos = pos_i.astype(F32)
    t = pos / (n - 1)
    ang = band_ref[...] * pos / n
    z = jnp.where(lane == 0, t,
                  jnp.where(lane <= HY_BANDS, jnp.cos(ang),
                            jnp.where(lane <= 2 * HY_BANDS, -jnp.sin(ang), 0.0)))
    freq = fr_ref[...]
    h = jnp.sin(freq * (_dot_hi(z, w1_ref[...]) + b1_ref[...]))
    h = jnp.sin(freq * (_dot_hi(h, w2_ref[...]) + b2_ref[...]))
    h = _dot_hi(h, w3_ref[...])
    tc = t[:, 0:1]
    win = jnp.exp(-tc * dl_ref[...])
    fwd = h[:, :HY_WIDTH] * win
    bwd = jnp.where(pos_i[:, 0:1] == 0, 0.0, h[:, HY_WIDTH:] * win)
    fs_ref[...] = fwd + bwd
    fd_ref[...] = fwd - bwd
    nrm_ref[...] = jnp.sum(jnp.abs(fwd) + jnp.abs(bwd), axis=0, keepdims=True)


def _hyena_filter(n, band_row, w1p, b1, freq, w2, b2, w3, deltas):
    return pl.pallas_call(
        functools.partial(_filter_kernel, n=n),
        out_shape=[jax.ShapeDtypeStruct((n, HY_WIDTH), F32), jax.ShapeDtypeStruct((n, HY_WIDTH), F32),
                   jax.ShapeDtypeStruct((1, HY_WIDTH), F32)],
        compiler_params=pltpu.CompilerParams(vmem_limit_bytes=VMEM_LIMIT),
        name="hyena_filter",
    )(band_row, w1p, b1, freq, w2, b2, w3, deltas)


def _spectrum_kernel(f_ref, x_ref, o_ref):
    o_ref[...] = _dot(f_ref[...], x_ref[...].astype(BF16))


def _filter_spectrum(fwd_tab, x):
    n2, n = fwd_tab.shape
    w = x.shape[1]
    tc = 256
    return pl.pallas_call(
        _spectrum_kernel,
        out_shape=jax.ShapeDtypeStruct((n2, w), F32),
        grid=(w // tc,),
        in_specs=[_const_spec((n2, n)), pl.BlockSpec((n, tc), lambda j: (0, j))],
        out_specs=pl.BlockSpec((n2, tc), lambda j: (0, j)),
        compiler_params=_cparams(("arbitrary",)),
        name="hyena_filter_spectrum",
    )(fwd_tab, x)


def _short_conv(x, w_ref, b_ref):
    n = x.shape[0]
    row = lax.broadcasted_iota(jnp.int32, x.shape, 0)
    prev = jnp.where(row == 0, 0.0, pltpu.roll(x, 1, axis=0))
    nxt = jnp.where(row == n - 1, 0.0, pltpu.roll(x, n - 1, axis=0))
    return w_ref[0:1, :] * prev + w_ref[1:2, :] * x + w_ref[2:3, :] * nxt + b_ref[...]


def _hyena_fwd_kernel(f_ref, x1_ref, v_ref, w1_ref, b1_ref, wv_ref, bv_ref, kr_ref, ka_ref, kn_ref, nrm_ref,
                      y_ref, u_ref, *, n):
    x1 = _short_conv(x1_ref[0], w1_ref, b1_ref)
    v = _short_conv(v_ref[0], wv_ref, bv_ref)
    ub = (v * x1).astype(BF16)
    u_ref[0] = ub
    spec = _dot(f_ref[...], ub)
    ur = spec[0:n]
    ua = spec[n:2 * n]
    row0 = lax.broadcasted_iota(jnp.int32, ur.shape, 0) == 0
    scale = jnp.where(row0, 1.0 / (2 * n), 1.0 / n) / nrm_ref[...]
    kr = kr_ref[...]
    ka = jnp.where(row0, 0.0, ka_ref[...])
    k4 = jnp.where(row0, kn_ref[...], kr)
    y_ref[0, 0:n] = ((ur * kr - ua * ka) * scale).astype(y_ref.dtype)
    y_ref[0, n:2 * n] = ((ur * ka + ua * k4) * scale).astype(y_ref.dtype)


def _hyena_inv_kernel(ft_ref, y_ref, u_ref, x0_ref, w0_ref, b0_ref, skip_ref, o_ref):
    conv = _dot(ft_ref[...], y_ref[0])
    x0 = _short_conv(x0_ref[0], w0_ref, b0_ref)
    o_ref[0] = (x0 * (conv + skip_ref[...] * u_ref[0].astype(F32))).astype(o_ref.dtype)


def _hyena(uh, tabs, spec, knyq, nrm, short_w, short_b, skip):
    fwd_tab, inv_tab = tabs
    bsz, n, _ = uh.shape
    tc = 256
    nct = HY_WIDTH // tc
    chan = lambda off: (lambda c, b: (b, 0, off + c))
    par = lambda off: (lambda c, b: (0, off + c))
    y, u = pl.pallas_call(
        functools.partial(_hyena_fwd_kernel, n=n),
        out_shape=[jax.ShapeDtypeStruct((bsz, 2 * n, HY_WIDTH), BF16), jax.ShapeDtypeStruct((bsz, n, HY_WIDTH), BF16)],
        grid=(nct, bsz),
        in_specs=[
            _const_spec((2 * n, n)),
            pl.BlockSpec((1, n, tc), chan(nct)), pl.BlockSpec((1, n, tc), chan(2 * nct)),
            pl.BlockSpec((3, tc), par(nct)), pl.BlockSpec((1, tc), par(nct)),
            pl.BlockSpec((3, tc), par(2 * nct)), pl.BlockSpec((1, tc), par(2 * nct)),
            pl.BlockSpec((n, tc), lambda c, b: (0, c)), pl.BlockSpec((n, tc), lambda c, b: (1, nct + c)),
            pl.BlockSpec((1, tc), par(0)), pl.BlockSpec((1, tc), par(0)),
        ],
        out_specs=[pl.BlockSpec((1, 2 * n, tc), lambda c, b: (b, 0, c)), pl.BlockSpec((1, n, tc), lambda c, b: (b, 0, c))],
        compiler_params=_cparams(("arbitrary", "arbitrary")),
        name="hyena_fwd",
    )(fwd_tab, uh, uh, short_w, short_b, short_w, short_b, spec, spec, knyq, nrm)
    return pl.pallas_call(
        _hyena_inv_kernel,
        out_shape=jax.ShapeDtypeStruct((bsz, n, HY_WIDTH), BF16),
        grid=(nct, bsz),
        in_specs=[
            _const_spec((n, 2 * n)),
            pl.BlockSpec((1, 2 * n, tc), lambda c, b: (b, 0, c)), pl.BlockSpec((1, n, tc), lambda c, b: (b, 0, c)),
            pl.BlockSpec((1, n, tc), chan(0)), pl.BlockSpec((3, tc), par(0)), pl.BlockSpec((1, tc), par(0)),
            pl.BlockSpec((1, tc), par(0)),
        ],
        out_specs=pl.BlockSpec((1, n, tc), lambda c, b: (b, 0, c)),
        compiler_params=_cparams(("arbitrary", "arbitrary")),
        name="hyena_inv",
    )(inv_tab, y, u, uh, short_w, short_b, skip)


def _merge_kernel(x_ref, mod_ref, ya_ref, yh_ref, yw_ref, gt_ref, wb_ref, wo_ref, g2_ref, rw_ref, rb_ref,
                  xo_ref, tok_ref, route_ref, rt_ref):
    tm = x_ref.shape[1]
    n_part = 2 if tm % 512 == 0 else 1
    for part in range(n_part):
        rows = pl.ds(part * (tm // n_part), tm // n_part)
        _merge_rows(rows, x_ref, mod_ref, ya_ref, yh_ref, yw_ref, gt_ref, wb_ref, wo_ref, g2_ref, rw_ref, rb_ref,
                    xo_ref, tok_ref, route_ref, rt_ref)


def _merge_rows(rows, x_ref, mod_ref, ya_ref, yh_ref, yw_ref, gt_ref, wb_ref, wo_ref, g2_ref, rw_ref, rb_ref,
                xo_ref, tok_ref, route_ref, rt_ref):
    gt = gt_ref[0, rows, :]
    m = gt[:, 0:D_MODEL].astype(F32) * _dot(ya_ref[0, rows, :], wb_ref[0])
    m = m + gt[:, D_MODEL:2 * D_MODEL].astype(F32) * _dot(yh_ref[0, rows, :], wb_ref[1])
    m = m + gt[:, 2 * D_MODEL:3 * D_MODEL].astype(F32) * _dot(yw_ref[0, rows, :], wb_ref[2])
    x = x_ref[0, rows, :] + mod_ref[0, 2:3, :] * _dot(m.astype(BF16), wo_ref[...])
    xo_ref[0, rows, :] = x
    ms = jnp.mean(x * x, axis=-1, keepdims=True)
    tok = x * lax.rsqrt(ms + EPS) * g2_ref[...]
    tok = tok * (1.0 + mod_ref[0, 4:5, :]) + mod_ref[0, 3:4, :]
    tok_ref[0, rows, :] = _pack_bf16_pairs(tok)

    tok_hi = tok.astype(BF16)
    tok_lo = (tok - tok_hi.astype(F32)).astype(BF16)
    lg2 = _dot(tok_hi, rw_ref[...]) + _dot(tok_lo, rw_ref[...])
    lg = lg2[:, :ROUTER_W] + lg2[:, ROUTER_W:] + rb_ref[...]
    lane = lax.broadcasted_iota(jnp.int32, lg.shape, 1).astype(F32)
    big = float(ROUTER_W)

    def first_lane(mask):
        return jnp.min(jnp.where(mask, lane, big), axis=-1, keepdims=True)

    is_g = lane < N_GROUPS
    gmax = jnp.max(jnp.where(is_g, lg, NEG_INF), axis=-1, keepdims=True)
    g_sel = first_lane(jnp.where(is_g, lg, NEG_INF) == gmax)
    p_g = 1.0 / jnp.sum(jnp.where(is_g, jnp.exp(lg - gmax), 0.0), axis=-1, keepdims=True)
    lo = N_GROUPS + g_sel * EXPERTS_PER_GROUP
    in_grp = (lane >= lo) & (lane < lo + EXPERTS_PER_GROUP)
    le = jnp.where(in_grp, lg, NEG_INF)
    v1 = jnp.max(le, axis=-1, keepdims=True)
    i1 = first_lane(le == v1)
    le = jnp.where(lane == i1, NEG_INF, le)
    v2 = jnp.max(le, axis=-1, keepdims=True)
    i2 = first_lane(le == v2)
    e2 = jnp.exp(v2 - v1)
    p1 = p_g / (1.0 + e2)
    p2 = p_g * e2 / (1.0 + e2)
    route = jnp.where(lane == 0, i1 - N_GROUPS, jnp.where(lane == 1, i2 - N_GROUPS,
                      jnp.where(lane == 2, p1, jnp.where(lane == 3, p2, 0.0))))
    route_ref[0, rows, :] = route
    rt_ref[0, :, rows] = route.T[0:ROUTE_T_ROWS, :]


def _merge(x, mod, mod_row, ya, yh, yw, gates, w_branch, w_out, g2, rw, rb):
    bsz, s, d = x.shape
    tm = min(512, s)
    row = lambda b, t: (b, t, 0)
    return pl.pallas_call(
        _merge_kernel,
        out_shape=[jax.ShapeDtypeStruct((bsz, s, d), F32), jax.ShapeDtypeStruct((bsz, s, d // 2), jnp.uint32),
                   jax.ShapeDtypeStruct((bsz, s, ROUTER_W), F32), jax.ShapeDtypeStruct((bsz, ROUTE_T_ROWS, s), F32)],
        grid=(bsz, s // tm),
        in_specs=[
            pl.BlockSpec((1, tm, d), row),
            pl.BlockSpec((1, N_MOD, d), lambda b, t: (mod_row(b), 0, 0)),
            pl.BlockSpec((1, tm, QA_W), row), pl.BlockSpec((1, tm, HY_WIDTH), row), pl.BlockSpec((1, tm, QW_W), row),
            pl.BlockSpec((1, tm, N_BRANCH * d), row),
            _const_spec((N_BRANCH, HY_WIDTH, d)), _const_spec((d, d)), _const_spec((1, d)),
            _const_spec((d, 2 * ROUTER_W)), _const_spec((1, ROUTER_W)),
        ],
        out_specs=[pl.BlockSpec((1, tm, d), row), pl.BlockSpec((1, tm, d // 2), row),
                   pl.BlockSpec((1, tm, ROUTER_W), row), pl.BlockSpec((1, ROUTE_T_ROWS, tm), lambda b, t: (b, 0, t))],
        compiler_params=_cparams(("parallel", "parallel")),
        name="merge_router",
    )(x, mod, ya, yh, yw, gates, w_branch, w_out, g2, rw, rb)


def _route_plan(flat, tile):
    n_tiles = -(-flat.shape[0] // tile) + N_EXPERTS
    onehot = (flat[:, None] == jnp.arange(N_EXPERTS, dtype=jnp.int32)[None, :]).astype(jnp.int32)
    csum = jnp.cumsum(onehot, axis=0)
    count = csum[-1]
    rank = jnp.sum((csum - onehot) * onehot, axis=1)
    size = ((count + tile - 1) // tile) * tile
    end = jnp.cumsum(size)
    start = end - size
    slot = jnp.sum(onehot * start[None, :], axis=1) + rank
    tile_row = jnp.arange(n_tiles, dtype=jnp.int32) * tile
    tile_e = jnp.minimum(jnp.sum((tile_row[:, None] >= end[None, :]).astype(jnp.int32), axis=1), N_EXPERTS - 1)
    tile_oh = (tile_e[:, None] == jnp.arange(N_EXPERTS, dtype=jnp.int32)[None, :]).astype(jnp.int32)
    filled = jnp.sum(tile_oh * (start + count)[None, :], axis=1)
    tile_rows = jnp.clip(filled - tile_row, 0, tile)
    return slot, tile_e, tile_rows, n_tiles


def _sc_mesh():
    info = plsc.get_sparse_core_info()
    mesh = plsc.VectorSubcoreMesh(core_axis_name="c", subcore_axis_name="s")
    return mesh, info.num_cores, info.num_cores * info.num_subcores


def _sc_scatter_rows(sources, slot, n_out):
    mesh, n_cores, n_workers = _sc_mesh()
    n = sum(src.shape[0] for src in sources)
    w = sources[0].shape[1]
    dtype = sources[0].dtype
    n_k = slot.shape[0] // n
    per_worker = n // n_workers
    n_chunks = per_worker // SC_CHUNK
    assert per_worker * n_workers == n and n_chunks * SC_CHUNK == per_worker
    assert all(src.shape[0] % SC_CHUNK == 0 for src in sources)
    n_src = len(sources)

    @functools.partial(
        pl.kernel, mesh=mesh,
        out_type=jax.ShapeDtypeStruct((n_out, w), dtype),
        scratch_types=[pltpu.VMEM((SC_CHUNK,), jnp.int32), pltpu.VMEM((SC_CHUNK, w), dtype), pltpu.SemaphoreType.DMA],
    )
    def scatter(*refs):
        src_hbm = refs[:n_src]
        slot_hbm, out_hbm, idx_v, rows_v, sem = refs[n_src:]
        base = (lax.axis_index("s") * n_cores + lax.axis_index("c")) * per_worker

        @pl.loop(0, n_chunks)
        def _(j):
            off = base + j * SC_CHUNK
            lo = 0
            for src in src_hbm:
                hi = lo + src.shape[0]

                @pl.when((off >= lo) & (off < hi))
                def _(src=src, lo=lo):
                    pltpu.sync_copy(src.at[pl.ds(off - lo, SC_CHUNK)], rows_v)

                lo = hi
            for q in range(n_k):
                pltpu.sync_copy(slot_hbm.at[pl.ds(q * n + off, SC_CHUNK)], idx_v)
                pltpu.async_copy(rows_v, out_hbm.at[idx_v], sem).wait()

    return scatter(*sources, slot)


def _sc_gather_rows(table, idx):
    mesh, n_cores, n_workers = _sc_mesh()
    m = idx.shape[0]
    w = table.shape[1]
    per_worker = m // n_workers
    n_chunks = per_worker // SC_CHUNK
    assert per_worker * n_workers == m and n_chunks * SC_CHUNK == per_worker

    @functools.partial(
        pl.kernel, mesh=mesh,
        out_type=jax.ShapeDtypeStruct((m, w), table.dtype),
        scratch_types=[pltpu.VMEM((SC_CHUNK,), jnp.int32), pltpu.VMEM((SC_CHUNK, w), table.dtype),
                       pltpu.SemaphoreType.DMA],
    )
    def gather(table_hbm, idx_hbm, out_hbm, idx_v, rows_v, sem):
        base = (lax.axis_index("s") * n_cores + lax.axis_index("c")) * per_worker

        @pl.loop(0, n_chunks)
        def _(j):
            off = base + j * SC_CHUNK
            pltpu.sync_copy(idx_hbm.at[pl.ds(off, SC_CHUNK)], idx_v)
            pltpu.async_copy(table_hbm.at[idx_v], rows_v, sem).wait()
            pltpu.sync_copy(rows_v, out_hbm.at[pl.ds(off, SC_CHUNK)])

    return gather(table, idx)


def _experts_kernel(te_ref, tr_ref, x_ref, w1_ref, w3_ref, w2_ref, o_ref, w1_b, w3_b, w2_b):
    t = pl.program_id(0)
    n_rows = tr_ref[t]

    @pl.when((t == 0) | (te_ref[t] != te_ref[jnp.maximum(t - 1, 0)]))
    def _():
        w1_b[...] = w1_ref[0, 0].astype(BF16)
        w3_b[...] = w3_ref[0, 0].astype(BF16)
        w2_b[...] = w2_ref[0, 0].astype(BF16)

    @pl.when(n_rows > 0)
    def _():
        xp = x_ref[...]
        row = lax.broadcasted_iota(jnp.int32, xp.shape, 0)
        x = _unpack_bf16_pairs(jnp.where(row < n_rows, xp, jnp.uint32(0))).astype(BF16)
        h1 = _dot(x, w1_b[...])
        h = (h1 * jax.nn.sigmoid(h1)) * _dot(x, w3_b[...])
        o_ref[...] = _pack_bf16_pairs(_dot(h.astype(BF16), w2_b[...]))

    @pl.when(n_rows == 0)
    def _():
        o_ref[...] = jnp.zeros_like(o_ref)


def _experts(x_sorted, tile_e, tile_rows, tile, layer, w1, w3, w2):
    n_slots, wp = x_sorted.shape
    d = 2 * wp
    wspec = lambda shape: pl.BlockSpec(shape, lambda t, te, tr: (layer, te[t], 0, 0))
    return pl.pallas_call(
        _experts_kernel,
        out_shape=jax.ShapeDtypeStruct((n_slots, wp), jnp.uint32),
        grid_spec=pltpu.PrefetchScalarGridSpec(
            num_scalar_prefetch=2,
            grid=(n_slots // tile,),
            in_specs=[pl.BlockSpec((tile, wp), lambda t, te, tr: (t, 0)),
                      wspec((1, 1, d, D_EXPERT)), wspec((1, 1, d, D_EXPERT)), wspec((1, 1, D_EXPERT, d))],
            out_specs=pl.BlockSpec((tile, wp), lambda t, te, tr: (t, 0)),
            scratch_shapes=[pltpu.VMEM((d, D_EXPERT), BF16), pltpu.VMEM((d, D_EXPERT), BF16),
                            pltpu.VMEM((D_EXPERT, d), BF16)],
        ),
        compiler_params=_cparams(("arbitrary",)),
        name="moe_experts",
    )(tile_e, tile_rows, x_sorted, w1, w3, w2)


def _combine_kernel(x_ref, mod_ref, ya_ref, yb_ref, route_ref, o_ref):
    route = route_ref[0]
    y = route[:, 2:3] * _unpack_bf16_pairs(ya_ref[...]) + route[:, 3:4] * _unpack_bf16_pairs(yb_ref[...])
    o_ref[0] = x_ref[0] + mod_ref[0, 5:6, :] * y


def _combine(x, mod, mod_row, y_tok, route, row_off, n_all):
    bsz, s, d = x.shape
    tm = min(512, s)
    nt = s // tm
    assert row_off % tm == 0 and n_all % tm == 0
    off0 = row_off // tm
    off1 = (n_all + row_off) // tm
    row = lambda b, t: (b, t, 0)
    return pl.pallas_call(
        _combine_kernel,
        out_shape=jax.ShapeDtypeStruct((bsz, s, d), F32),
        grid=(bsz, nt),
        in_specs=[
            pl.BlockSpec((1, tm, d), row),
            pl.BlockSpec((1, N_MOD, d), lambda b, t: (mod_row(b), 0, 0)),
            pl.BlockSpec((tm, d // 2), lambda b, t: (off0 + b * nt + t, 0)),
            pl.BlockSpec((tm, d // 2), lambda b, t: (off1 + b * nt + t, 0)),
            pl.BlockSpec((1, tm, ROUTER_W), row),
        ],
        out_specs=pl.BlockSpec((1, tm, d), row),
        compiler_params=_cparams(("parallel", "parallel")),
        name="moe_combine",
    )(x, mod, y_tok, y_tok, route)


def _moe(streams, layer, w1, w3, w2):
    toks = [st[2].reshape(-1, st[2].shape[-1]) for st in streams]
    n_all = sum(t.shape[0] for t in toks)
    tile = MOE_TM if 2 * n_all >= N_EXPERTS * 4 * MOE_TM else MOE_TM_SMALL
    flat = jnp.concatenate([st[4][:, k, :].reshape(-1) for k in range(2) for st in streams]).astype(jnp.int32)
    slot, tile_e, tile_rows, n_tiles = _route_plan(flat, tile)
    x_sorted = _sc_scatter_rows(toks, slot, n_tiles * tile)
    y_sorted = _experts(x_sorted, tile_e, tile_rows, tile, layer, w1, w3, w2)
    return _sc_gather_rows(y_sorted, slot), n_all


def _rope_tables(n_tok):
    rows = n_tok // GRID_W
    row_id = jnp.repeat(jnp.arange(rows), GRID_W)
    col_id = jnp.tile(jnp.arange(GRID_W), rows)
    inv_freq = ROPE_THETA ** (-jnp.arange(ROPE_FREQS, dtype=F32) / ROPE_FREQS)
    ang_r = row_id[:, None] * inv_freq
    ang_c = col_id[:, None] * inv_freq
    cos_h = jnp.concatenate([jnp.cos(ang_r), jnp.cos(ang_r), jnp.cos(ang_c), jnp.cos(ang_c)], axis=-1)
    sin_h = jnp.concatenate([-jnp.sin(ang_r), jnp.sin(ang_r), -jnp.sin(ang_c), jnp.sin(ang_c)], axis=-1)
    reps = KA_W // HEAD_DIM
    return jnp.tile(cos_h, (1, reps)), jnp.tile(sin_h, (1, reps))


def kernel(x, c, ctx, c_ctx, ada_w, ada_b, norm1_g, norm2_g, w_in, qk_g, hy_short_w, hy_short_b, hy_f_w1, hy_f_b1,
           hy_f_freq, hy_f_w2, hy_f_b2, hy_f_w3, hy_bias, sinks, w_branch, w_out, router_g_w, router_g_b,
           router_e_w, router_e_b, exp_w1, exp_w3, exp_w2):
    bsz, n_tok, d = x.shape
    n_ctx = ctx.shape[1]
    depth = ada_w.shape[0]
    assert d == D_MODEL and bsz < MOD_ROWS and n_tok % 256 == 0 and n_ctx % 256 == 0

    cc = jnp.zeros((MOD_ROWS, d), F32).at[:bsz].set(c).at[bsz].set(c_ctx)
    mods = _ada_table(cc, ada_w, ada_b)
    ctx_row = lambda b: bsz

    rope_tabs = _rope_tables(n_tok)
    head_ones = (jnp.arange(QA_W)[:, None] // HEAD_DIM == jnp.arange(QA_W)[None, :] // HEAD_DIM).astype(BF16)
    tabs_lat = _dft_tables(n_tok)
    tabs_ctx = _dft_tables(n_ctx)
    bands = jnp.linspace(1e-4, HY_BANDS - 1, HY_BANDS, dtype=F32)
    band_row = jnp.zeros((1, LANES), F32).at[0, 1:1 + HY_BANDS].set(bands).at[0, 1 + HY_BANDS:1 + 2 * HY_BANDS].set(bands)
    band_row = 2 * math.pi * band_row
    deltas = jnp.abs(jnp.linspace(math.log(HY_TARGET) / HY_SLOW_DECAY, math.log(HY_TARGET) / HY_FAST_DECAY,
                                  HY_WIDTH, dtype=F32))[None, :]

    lat_row = lambda b: b
    xc = ctx
    pend_lat = pend_ctx = None
    for l in range(depth):
        update_ctx = l < depth - 1
        mod = mods[l]
        w_in_b = w_in[l].astype(BF16)
        gq = [jnp.tile(qk_g[l, 0], A_HEADS)[None], jnp.tile(qk_g[l, 1], A_KV)[None],
              jnp.tile(qk_g[l, 2], W_HEADS)[None], jnp.tile(qk_g[l, 3], W_KV)[None]]
        g1 = norm1_g[l][None]

        w1p = jnp.zeros((LANES, HY_HIDDEN), F32).at[:hy_f_w1.shape[1]].set(hy_f_w1[l])
        fargs = (band_row, w1p, hy_f_b1[l][None], hy_f_freq[l][None], hy_f_w2[l], hy_f_b2[l][None], hy_f_w3[l], deltas)
        skip = hy_bias[l][None]
        sb = hy_short_b[l][None]

        def hyena_filter(n, tabs):
            fsum, fdiff, nrm = _hyena_filter(n, *fargs)
            spec = _filter_spectrum(tabs[0], jnp.concatenate([fsum, fdiff], axis=1))
            return spec, spec[n:n + 1, :HY_WIDTH], nrm

        filt_lat = hyena_filter(n_tok, tabs_lat)
        filt_ctx = hyena_filter(n_ctx, tabs_ctx) if update_ctx else None

        wb = w_branch[l].astype(BF16)
        wo = w_out[l].astype(BF16)
        g2 = norm2_g[l][None]
        rw = jnp.zeros((d, ROUTER_W), F32).at[:, :N_GROUPS].set(router_g_w[l]).at[:, N_GROUPS:N_GROUPS + N_EXPERTS].set(router_e_w[l])
        rb = jnp.zeros((1, ROUTER_W), F32).at[0, :N_GROUPS].set(router_g_b[l]).at[0, N_GROUPS:N_GROUPS + N_EXPERTS].set(router_e_b[l])
        rw_hi = rw.astype(BF16)
        rw = jnp.concatenate([rw_hi, (rw - rw_hi.astype(F32)).astype(BF16)], axis=1)

        proj = _in_proj(x, mod, lat_row, g1, w_in_b, gq, head_ones, rope_tabs, pend_lat)
        proj_c = _in_proj(xc, mod, ctx_row, g1, w_in_b, gq, head_ones, None, pend_ctx)
        if pend_lat is not None:
            x = proj[-1]
        if pend_ctx is not None:
            xc = proj_c[-1]
        qa, ka, va, uh, qw, kw, vw, gts = proj[:8]
        qa_c, ka_c, va_c, uh_c, qw_c, kw_c, vw_c, gts_c = proj_c[:8]

        ya = _attention(qa, ka_c, va_c, mode="global", kl=ka, vl=va, tq=1024)
        yw = _attention(qw, kw_c, vw_c, mode="window", kl=kw, vl=vw, sinks=sinks[l], tq=512)
        yh = _hyena(uh, tabs_lat, *filt_lat, hy_short_w[l], sb, skip)
        x, tok, route, route_t = _merge(x, mod, lat_row, ya, yh, yw, gts, wb, wo, g2, rw, rb)
        y_tok, n_all = _moe([(x, lat_row, tok, route, route_t)], l, exp_w1, exp_w3, exp_w2)
        pend_lat = (mod, y_tok, route, 0, n_all)
        pend_ctx = None
        if update_ctx:
            ya_c = _attention(qa_c, ka_c, va_c, mode="ctx", tq=256)
            yw_c = _attention(qw_c, kw_c, vw_c, mode="ctx_sink", sinks=sinks[l], tq=256)
            yh_c = _hyena(uh_c, tabs_ctx, *filt_ctx, hy_short_w[l], sb, skip)
            xc, tok_c, route_c, route_ct = _merge(xc, mod, ctx_row, ya_c, yh_c, yw_c, gts_c, wb, wo, g2, rw, rb)
            y_tok_c, n_all_c = _moe([(xc, ctx_row, tok_c, route_c, route_ct)], l, exp_w1, exp_w3, exp_w2)
            pend_ctx = (mod, y_tok_c, route_c, 0, n_all_c)
    return _combine(x, *pend_lat[:1], lat_row, *pend_lat[1:])
```

```python
import functools
import math

import jax
import jax.numpy as jnp
from jax import lax
from jax.experimental import pallas as pl
from jax.experimental.pallas import tpu as pltpu
from jax.experimental.pallas import tpu_sc as plsc

F32 = jnp.float32
BF16 = jnp.bfloat16
HIGHEST = lax.Precision.HIGHEST

D_MODEL = 1024
GRID_W = 64
HEAD_DIM = 64
ROPE_FREQS = HEAD_DIM // 4
ROPE_THETA = 10000.0
ATTN_SCALE = HEAD_DIM ** -0.5
LOG2E = math.log2(math.e)
Q_SCALE = ATTN_SCALE * LOG2E
V_AUG = 128
NEG_INF = -1e30
WINDOW = 128
A_HEADS = 8
A_KV = 2
W_HEADS = 8
W_KV = 2
HY_WIDTH = 512
HY_BANDS = 16
HY_HIDDEN = 64
HY_TARGET = 1e-2
HY_FAST_DECAY = 0.3
HY_SLOW_DECAY = 1.5
N_BRANCH = 3
N_GROUPS = 4
EXPERTS_PER_GROUP = 4
N_EXPERTS = N_GROUPS * EXPERTS_PER_GROUP
D_EXPERT = 512
N_MOD = 6
EPS = 1e-6

QA_W = A_HEADS * HEAD_DIM
KA_W = A_KV * HEAD_DIM
QW_W = W_HEADS * HEAD_DIM
KW_W = W_KV * HEAD_DIM
O_QA = 0
O_KA = O_QA + QA_W
O_VA = O_KA + KA_W
O_UH = O_VA + KA_W
O_QW = O_UH + 3 * HY_WIDTH
O_KW = O_QW + QW_W
O_VW = O_KW + KW_W
O_GT = O_VW + KW_W
D_IN = O_GT + N_BRANCH * D_MODEL

MOD_ROWS = 24
LANES = 128
ROUTER_W = LANES
ROUTE_T_ROWS = 8
VMEM_LIMIT = 56 * 1024 * 1024
MOE_TM = 1024
MOE_TM_SMALL = 512
SC_CHUNK = 64


def _cparams(sem):
    return pltpu.CompilerParams(dimension_semantics=sem, vmem_limit_bytes=VMEM_LIMIT)


def _const_spec(shape):
    nd = len(shape)
    return pl.BlockSpec(shape, lambda *_: (0,) * nd, pipeline_mode=pl.Buffered(1))


def _dot(a, b):
    return jnp.dot(a, b, preferred_element_type=F32)


def _dot_nt(a, b):
    return lax.dot_general(a, b, (((1,), (1,)), ((), ())), preferred_element_type=F32)


def _dot_hi(a, b):
    return jnp.dot(a, b, precision=HIGHEST, preferred_element_type=F32)


def _pack_bf16_pairs(t):
    half = t.shape[-1] // 2
    hi = lax.bitcast_convert_type(t[:, :half].astype(BF16).astype(F32), jnp.uint32)
    lo = lax.bitcast_convert_type(t[:, half:].astype(BF16).astype(F32), jnp.uint32)
    return hi | (lo >> 16)


def _unpack_bf16_pairs(p):
    hi = lax.bitcast_convert_type(p & jnp.uint32(0xFFFF0000), F32)
    lo = lax.bitcast_convert_type(p << 16, F32)
    return jnp.concatenate([hi, lo], axis=-1)


def _ada_kernel(cc_ref, w_ref, b_ref, o_ref):
    cc = cc_ref[...]
    act = cc * jax.nn.sigmoid(cc)
    o_ref[0] = _dot_hi(act, w_ref[0]) + b_ref[0]


def _ada_table(cc, ada_w, ada_b):
    depth = ada_w.shape[0]
    tn = 1536
    nmod = ada_w.shape[2]
    out = pl.pallas_call(
        _ada_kernel,
        out_shape=jax.ShapeDtypeStruct((depth, MOD_ROWS, nmod), F32),
        grid=(depth, nmod // tn),
        in_specs=[
            pl.BlockSpec((MOD_ROWS, D_MODEL), lambda l, j: (0, 0)),
            pl.BlockSpec((1, D_MODEL, tn), lambda l, j: (l, 0, j)),
            pl.BlockSpec((1, 1, tn), lambda l, j: (l, 0, j)),
        ],
        out_specs=pl.BlockSpec((1, MOD_ROWS, tn), lambda l, j: (l, 0, j)),
        compiler_params=_cparams(("arbitrary", "arbitrary")),
        name="ada_table",
    )(cc, ada_w, ada_b.reshape(depth, 1, nmod))
    return out.reshape(depth, MOD_ROWS, N_MOD, D_MODEL)


def _in_proj_kernel(*refs, rope, fuse, kv_only):
    refs = list(refs)
    x_ref, mod_ref, g1_ref, w_ref, gqa_ref, gka_ref, gqw_ref, gkw_ref, hs_ref = refs[:9]
    del refs[:9]
    cos_ref = sin_ref = None
    if rope:
        cos_ref, sin_ref = refs[:2]
        del refs[:2]
    moe_refs = None
    if fuse:
        moe_refs = tuple(refs[:4]) + (refs[-1],)
        del refs[:4]
        del refs[-1]
    if kv_only:
        ka_ref, va_ref, kw_ref, vw_ref = refs
        qa_ref = uh_ref = qw_ref = gt_ref = None
    else:
        qa_ref, ka_ref, va_ref, uh_ref, qw_ref, kw_ref, vw_ref, gt_ref = refs
    tm = x_ref.shape[1]
    n_part = 2 if tm % 512 == 0 else 1
    for part in range(n_part):
        rows = pl.ds(part * (tm // n_part), tm // n_part)
        _in_proj_rows(rows, rope, moe_refs, x_ref, mod_ref, g1_ref, w_ref, gqa_ref, gka_ref, gqw_ref, gkw_ref, hs_ref,
                      cos_ref, sin_ref, qa_ref, ka_ref, va_ref, uh_ref, qw_ref, kw_ref, vw_ref, gt_ref)


def _in_proj_rows(rows, rope, moe_refs, x_ref, mod_ref, g1_ref, w_ref, gqa_ref, gka_ref, gqw_ref, gkw_ref, hs_ref,
                  cos_ref, sin_ref, qa_ref, ka_ref, va_ref, uh_ref, qw_ref, kw_ref, vw_ref, gt_ref):
    x = x_ref[0, rows, :]
    if moe_refs is not None:
        modp_ref, ya_ref, yb_ref, route_ref, xo_ref = moe_refs
        route = route_ref[0, rows, :]
        y = route[:, 2:3] * _unpack_bf16_pairs(ya_ref[rows, :]) + route[:, 3:4] * _unpack_bf16_pairs(yb_ref[rows, :])
        x = x + modp_ref[0, 5:6, :] * y
        xo_ref[0, rows, :] = x
    ms = jnp.mean(x * x, axis=-1, keepdims=True)
    h = x * lax.rsqrt(ms + EPS) * g1_ref[...]
    h = (h * (1.0 + mod_ref[0, 1:2, :]) + mod_ref[0, 0:1, :]).astype(BF16)

    def proj(lo, width):
        return _dot(h, w_ref[:, lo:lo + width])

    def head_norm_rope(t, g_ref):
        width = t.shape[-1]
        ssum = _dot((t * t).astype(BF16), hs_ref[0:width, 0:width])
        tn = t * lax.rsqrt(ssum * (1.0 / HEAD_DIM) + EPS) * g_ref[...]
        if rope:
            reps = width // cos_ref.shape[-1]
            cos = cos_ref[rows, :]
            sin = sin_ref[rows, :]
            if reps > 1:
                cos = jnp.concatenate([cos] * reps, axis=-1)
                sin = jnp.concatenate([sin] * reps, axis=-1)
            lane = lax.broadcasted_iota(jnp.int32, tn.shape, 1)
            first = (lane & (2 * ROPE_FREQS - 1)) < ROPE_FREQS
            partner = jnp.where(first, pltpu.roll(tn, width - ROPE_FREQS, axis=1), pltpu.roll(tn, ROPE_FREQS, axis=1))
            tn = tn * cos + partner * sin
        return tn

    def store_heads(o_ref, t):
        for hh in range(t.shape[-1] // HEAD_DIM):
            o_ref[0, hh, rows, :] = t[:, hh * HEAD_DIM:(hh + 1) * HEAD_DIM].astype(o_ref.dtype)

    def store_values(o_ref, t):
        tail = (lax.broadcasted_iota(jnp.int32, (t.shape[0], V_AUG - HEAD_DIM), 1) == 0).astype(F32)
        for hh in range(t.shape[-1] // HEAD_DIM):
            o_ref[0, hh, rows, :] = jnp.concatenate([t[:, hh * HEAD_DIM:(hh + 1) * HEAD_DIM], tail],
                                                    axis=-1).astype(o_ref.dtype)

    if gt_ref is not None:
        gt_ref[0, rows, :] = jax.nn.sigmoid(proj(O_GT, N_BRANCH * D_MODEL)).astype(gt_ref.dtype)
        store_heads(qa_ref, head_norm_rope(proj(O_QA, QA_W), gqa_ref) * Q_SCALE)
        store_heads(qw_ref, head_norm_rope(proj(O_QW, QW_W), gqw_ref) * Q_SCALE)
    store_heads(ka_ref, head_norm_rope(proj(O_KA, KA_W), gka_ref))
    store_heads(kw_ref, head_norm_rope(proj(O_KW, KW_W), gkw_ref))
    store_values(va_ref, proj(O_VA, KA_W))
    store_values(vw_ref, proj(O_VW, KW_W))
    if uh_ref is not None:
        uh_ref[0, rows, :] = proj(O_UH, 3 * HY_WIDTH)


def _in_proj(x, mod, mod_row, g1, w_in, gq, hs, rope_tabs, pending=None, kv_only=False):
    bsz, s, d = x.shape
    tm = min(512, s)
    nt = s // tm
    rope = rope_tabs is not None
    in_specs = [
        pl.BlockSpec((1, tm, d), lambda b, t: (b, t, 0)),
        pl.BlockSpec((1, N_MOD, d), lambda b, t: (mod_row(b), 0, 0)),
        _const_spec((1, d)),
        _const_spec((d, D_IN)),
        _const_spec((1, QA_W)), _const_spec((1, KA_W)), _const_spec((1, QW_W)), _const_spec((1, KW_W)),
        _const_spec((QA_W, QA_W)),
    ]
    args = [x, mod, g1, w_in, gq[0], gq[1], gq[2], gq[3], hs]
    if rope:
        in_specs += [pl.BlockSpec((tm, KA_W), lambda b, t: (t, 0))] * 2
        args += list(rope_tabs)

    def head_out(n_heads, width=HEAD_DIM):
        return (jax.ShapeDtypeStruct((bsz, n_heads, s, width), BF16),
                pl.BlockSpec((1, n_heads, tm, width), lambda b, t: (b, 0, t, 0)))

    def flat_out(width, dtype):
        return (jax.ShapeDtypeStruct((bsz, s, width), dtype), pl.BlockSpec((1, tm, width), lambda b, t: (b, t, 0)))

    outs = [head_out(A_HEADS), head_out(A_KV), head_out(A_KV, V_AUG), flat_out(3 * HY_WIDTH, F32),
            head_out(W_HEADS), head_out(W_KV), head_out(W_KV, V_AUG), flat_out(N_BRANCH * D_MODEL, BF16)]
    if kv_only:
        outs = [outs[1], outs[2], outs[5], outs[6]]
    if pending is not None:
        mod_prev, y_tok, route, row_off, n_all = pending
        assert row_off % tm == 0 and n_all % tm == 0
        off0, off1 = row_off // tm, (n_all + row_off) // tm
        in_specs += [pl.BlockSpec((1, N_MOD, d), lambda b, t: (mod_row(b), 0, 0)),
                     pl.BlockSpec((tm, d // 2), lambda b, t: (off0 + b * nt + t, 0)),
                     pl.BlockSpec((tm, d // 2), lambda b, t: (off1 + b * nt + t, 0)),
                     pl.BlockSpec((1, tm, ROUTER_W), lambda b, t: (b, t, 0))]
        args += [mod_prev, y_tok, y_tok, route]
        outs.append(flat_out(d, F32))
    return pl.pallas_call(
        functools.partial(_in_proj_kernel, rope=rope, fuse=pending is not None, kv_only=kv_only),
        out_shape=[o[0] for o in outs],
        grid=(bsz, s // tm),
        in_specs=in_specs,
        out_specs=[o[1] for o in outs],
        compiler_params=_cparams(("parallel", "parallel")),
        name="in_proj",
    )(*args)


def _attn_kernel(*refs, n_g, tq, mode, s_len):
    refs = list(refs)
    sink_ref = refs.pop(0) if mode in ("window", "ctx_sink") else None
    q_ref = refs.pop(0)
    if mode == "global":
        kl_ref, vl_ref = refs.pop(0), refs.pop(0)
    elif mode == "window":
        n_win = tq // WINDOW + 2
        kwin_refs = [refs.pop(0) for _ in range(n_win)]
        vwin_refs = [refs.pop(0) for _ in range(n_win)]
    kc_ref, vc_ref, o_ref = refs
    n_kv = kc_ref.shape[1]
    if mode == "window":
        qt = pl.program_id(1)
        qpos = lax.broadcasted_iota(jnp.int32, (tq, n_win * WINDOW), 0) + qt * tq
        kpos = lax.broadcasted_iota(jnp.int32, (tq, n_win * WINDOW), 1) + (qt * tq - WINDOW)
        valid = (kpos - qpos <= WINDOW) & (qpos - kpos <= WINDOW) & (kpos >= 0) & (kpos < s_len)
        band_bias = jnp.where(valid, 0.0, NEG_INF)

    def attend(q, kv, sink):
        sc = _dot_nt(q, kc_ref[0, kv])
        m = jnp.max(sc, axis=-1, keepdims=True)
        if mode == "global":
            sl = _dot_nt(q, kl_ref[0, kv])
        elif mode == "window":
            kl = jnp.concatenate([r[0, kv] for r in kwin_refs], axis=0)
            sl = _dot_nt(q, kl) + band_bias
        if mode in ("global", "window"):
            m = jnp.maximum(m, jnp.max(sl, axis=-1, keepdims=True))
        if sink is not None:
            m = jnp.maximum(m, sink)
        acc = _dot(jnp.exp2(sc - m).astype(BF16), vc_ref[0, kv])
        if mode in ("global", "window"):
            vl = vl_ref[0, kv] if mode == "global" else jnp.concatenate([r[0, kv] for r in vwin_refs], axis=0)
            acc = acc + _dot(jnp.exp2(sl - m).astype(BF16), vl)
        den = acc[:, HEAD_DIM:HEAD_DIM + 1]
        if sink is not None:
            den = den + jnp.exp2(sink - m)
        return acc[:, :HEAD_DIM] / den

    outs = []
    for h in range(n_kv * n_g):
        sink = None if sink_ref is None else sink_ref[h] * LOG2E
        outs.append(attend(q_ref[0, h], h // n_g, sink))
    o_ref[0] = jnp.concatenate(outs, axis=-1).astype(o_ref.dtype)


def _attention(q, kc, vc, *, mode, kl=None, vl=None, sinks=None, tq):
    bsz, n_h, sq, hd = q.shape
    n_kv = kc.shape[1]
    n_g = n_h // n_kv
    sc_len = kc.shape[2]
    tq = min(tq, sq)
    nq = sq // tq
    in_specs, args = [], []
    if mode in ("window", "ctx_sink"):
        in_specs.append(pl.BlockSpec(memory_space=pltpu.SMEM))
        args.append(sinks)
    in_specs.append(pl.BlockSpec((1, n_h, tq, hd), lambda b, t: (b, 0, t, 0)))
    args.append(q)
    whole = lambda a: pl.BlockSpec((1,) + a.shape[1:], lambda b, t: (b, 0, 0, 0))
    if mode == "global":
        in_specs += [whole(kl), whole(vl)]
        args += [kl, vl]
    elif mode == "window":
        per_q = tq // WINDOW
        n_blk = sq // WINDOW
        win = lambda a: [pl.BlockSpec((1, n_kv, WINDOW, a.shape[-1]),
                                      lambda b, t, i=i: (b, 0, jnp.clip(t * per_q - 1 + i, 0, n_blk - 1), 0))
                         for i in range(per_q + 2)]
        in_specs += win(kl) + win(vl)
        args += [kl] * (per_q + 2) + [vl] * (per_q + 2)
    in_specs += [whole(kc), whole(vc)]
    args += [kc, vc]
    return pl.pallas_call(
        functools.partial(_attn_kernel, n_g=n_g, tq=tq, mode=mode, s_len=sq),
        out_shape=jax.ShapeDtypeStruct((bsz, sq, n_h * hd), BF16),
        grid=(bsz, nq),
        in_specs=in_specs,
        out_specs=pl.BlockSpec((1, tq, n_h * hd), lambda b, t: (b, t, 0)),
        compiler_params=_cparams(("parallel", "parallel")),
        name="attn_" + mode,
    )(*args)


def _dft_tables(n):
    kb = 64
    s = jnp.arange(n, dtype=jnp.int32)[None, :]

    def table(mult, count):
        kk = jnp.arange(count, dtype=jnp.int32)[:, None] * mult
        ang = ((kk * s) % (2 * n)).astype(F32) * (math.pi / n)
        return jnp.cos(ang), jnp.sin(ang)

    ca, sa = table(kb, n // kb)
    cb, sb = table(1, kb)
    cos = (ca[:, None, :] * cb[None] - sa[:, None, :] * sb[None]).reshape(n, n)
    sin = (sa[:, None, :] * cb[None] + ca[:, None, :] * sb[None]).reshape(n, n)
    k = jnp.arange(n, dtype=jnp.int32)[:, None]
    sin = jnp.where(k == 0, jnp.where(s % 2 == 0, 1.0, -1.0), sin)
    fwd = jnp.concatenate([cos, sin], axis=0).astype(BF16)
    return fwd, fwd.T


def _filter_kernel(band_ref, w1_ref, b1_ref, fr_ref, w2_ref, b2_ref, w3_ref, dl_ref, fs_ref, fd_ref, nrm_ref, *, n):
    pos_i = lax.broadcasted_iota(jnp.int32, (n, LANES), 0)
    lane = lax.broadcasted_iota(jnp.int32, (n, LANES), 1)
    pos = pos_i.astype(F32)
    t = pos / (n - 1)
    ang = band_ref[...] * pos / n
    z = jnp.where(lane == 0, t,
                  jnp.where(lane <= HY_BANDS, jnp.cos(ang),
                            jnp.where(lane <= 2 * HY_BANDS, -jnp.sin(ang), 0.0)))
    freq = fr_ref[...]
    h = jnp.sin(freq * (_dot_hi(z, w1_ref[...]) + b1_ref[...]))
    h = jnp.sin(freq * (_dot_hi(h, w2_ref[...]) + b2_ref[...]))
    h = _dot_hi(h, w3_ref[...])
    tc = t[:, 0:1]
    win = jnp.exp(-tc * dl_ref[...])
    fwd = h[:, :HY_WIDTH] * win
    bwd = jnp.where(pos_i[:, 0:1] == 0, 0.0, h[:, HY_WIDTH:] * win)
    fs_ref[...] = fwd + bwd
    fd_ref[...] = fwd - bwd
    nrm_ref[...] = jnp.sum(jnp.abs(fwd) + jnp.abs(bwd), axis=0, keepdims=True)


def _hyena_filter(n, band_row, w1p, b1, freq, w2, b2, w3, deltas):
    return pl.pallas_call(
        functools.partial(_filter_kernel, n=n),
        out_shape=[jax.ShapeDtypeStruct((n, HY_WIDTH), F32), jax.ShapeDtypeStruct((n, HY_WIDTH), F32),
                   jax.ShapeDtypeStruct((1, HY_WIDTH), F32)],
        compiler_params=pltpu.CompilerParams(vmem_limit_bytes=VMEM_LIMIT),
        name="hyena_filter",
    )(band_row, w1p, b1, freq, w2, b2, w3, deltas)


def _spectrum_kernel(f_ref, x_ref, o_ref):
    o_ref[...] = _dot(f_ref[...], x_ref[...].astype(BF16))


def _filter_spectrum(fwd_tab, x):
    n2, n = fwd_tab.shape
    w = x.shape[1]
    tc = 256
    return pl.pallas_call(
        _spectrum_kernel,
        out_shape=jax.ShapeDtypeStruct((n2, w), F32),
        grid=(w // tc,),
        in_specs=[_const_spec((n2, n)), pl.BlockSpec((n, tc), lambda j: (0, j))],
        out_specs=pl.BlockSpec((n2, tc), lambda j: (0, j)),
        compiler_params=_cparams(("arbitrary",)),
        name="hyena_filter_spectrum",
    )(fwd_tab, x)


def _short_conv(x, w_ref, b_ref):
    n = x.shape[0]
    row = lax.broadcasted_iota(jnp.int32, x.shape, 0)
    prev = jnp.where(row == 0, 0.0, pltpu.roll(x, 1, axis=0))
    nxt = jnp.where(row == n - 1, 0.0, pltpu.roll(x, n - 1, axis=0))
    return w_ref[0:1, :] * prev + w_ref[1:2, :] * x + w_ref[2:3, :] * nxt + b_ref[...]


def _hyena_fwd_kernel(f_ref, x1_ref, v_ref, w1_ref, b1_ref, wv_ref, bv_ref, kr_ref, ka_ref, kn_ref, nrm_ref,
                      y_ref, u_ref, *, n):
    x1 = _short_conv(x1_ref[0], w1_ref, b1_ref)
    v = _short_conv(v_ref[0], wv_ref, bv_ref)
    ub = (v * x1).astype(BF16)
    u_ref[0] = ub
    spec = _dot(f_ref[...], ub)
    ur = spec[0:n]
    ua = spec[n:2 * n]
    row0 = lax.broadcasted_iota(jnp.int32, ur.shape, 0) == 0
    scale = jnp.where(row0, 1.0 / (2 * n), 1.0 / n) / nrm_ref[...]
    kr = kr_ref[...]
    ka = jnp.where(row0, 0.0, ka_ref[...])
    k4 = jnp.where(row0, kn_ref[...], kr)
    y_ref[0, 0:n] = ((ur * kr - ua * ka) * scale).astype(y_ref.dtype)
    y_ref[0, n:2 * n] = ((ur * ka + ua * k4) * scale).astype(y_ref.dtype)


def _hyena_inv_kernel(ft_ref, y_ref, u_ref, x0_ref, w0_ref, b0_ref, skip_ref, o_ref):
    conv = _dot(ft_ref[...], y_ref[0])
    x0 = _short_conv(x0_ref[0], w0_ref, b0_ref)
    o_ref[0] = (x0 * (conv + skip_ref[...] * u_ref[0].astype(F32))).astype(o_ref.dtype)


def _hyena(uh, tabs, spec, knyq, nrm, short_w, short_b, skip):
    fwd_tab, inv_tab = tabs
    bsz, n, _ = uh.shape
    tc = 256
    nct = HY_WIDTH // tc
    chan = lambda off: (lambda c, b: (b, 0, off + c))
    par = lambda off: (lambda c, b: (0, off + c))
    y, u = pl.pallas_call(
        functools.partial(_hyena_fwd_kernel, n=n),
        out_shape=[jax.ShapeDtypeStruct((bsz, 2 * n, HY_WIDTH), BF16), jax.ShapeDtypeStruct((bsz, n, HY_WIDTH), BF16)],
        grid=(nct, bsz),
        in_specs=[
            _const_spec((2 * n, n)),
            pl.BlockSpec((1, n, tc), chan(nct)), pl.BlockSpec((1, n, tc), chan(2 * nct)),
            pl.BlockSpec((3, tc), par(nct)), pl.BlockSpec((1, tc), par(nct)),
            pl.BlockSpec((3, tc), par(2 * nct)), pl.BlockSpec((1, tc), par(2 * nct)),
            pl.BlockSpec((n, tc), lambda c, b: (0, c)), pl.BlockSpec((n, tc), lambda c, b: (1, nct + c)),
            pl.BlockSpec((1, tc), par(0)), pl.BlockSpec((1, tc), par(0)),
        ],
        out_specs=[pl.BlockSpec((1, 2 * n, tc), lambda c, b: (b, 0, c)), pl.BlockSpec((1, n, tc), lambda c, b: (b, 0, c))],
        compiler_params=_cparams(("arbitrary", "arbitrary")),
        name="hyena_fwd",
    )(fwd_tab, uh, uh, short_w, short_b, short_w, short_b, spec, spec, knyq, nrm)
    return pl.pallas_call(
        _hyena_inv_kernel,
        out_shape=jax.ShapeDtypeStruct((bsz, n, HY_WIDTH), BF16),
        grid=(nct, bsz),
        in_specs=[
            _const_spec((n, 2 * n)),
            pl.BlockSpec((1, 2 * n, tc), lambda c, b: (b, 0, c)), pl.BlockSpec((1, n, tc), lambda c, b: (b, 0, c)),
            pl.BlockSpec((1, n, tc), chan(0)), pl.BlockSpec((3, tc), par(0)), pl.BlockSpec((1, tc), par(0)),
            pl.BlockSpec((1, tc), par(0)),
        ],
        out_specs=pl.BlockSpec((1, n, tc), lambda c, b: (b, 0, c)),
        compiler_params=_cparams(("arbitrary", "arbitrary")),
        name="hyena_inv",
    )(inv_tab, y, u, uh, short_w, short_b, skip)


def _merge_kernel(x_ref, mod_ref, ya_ref, yh_ref, yw_ref, gt_ref, wb_ref, wo_ref, g2_ref, rw_ref, rb_ref,
                  xo_ref, tok_ref, route_ref, rt_ref):
    tm = x_ref.shape[1]
    n_part = 2 if tm % 512 == 0 else 1
    for part in range(n_part):
        rows = pl.ds(part * (tm // n_part), tm // n_part)
        _merge_rows(rows, x_ref, mod_ref, ya_ref, yh_ref, yw_ref, gt_ref, wb_ref, wo_ref, g2_ref, rw_ref, rb_ref,
                    xo_ref, tok_ref, route_ref, rt_ref)


def _merge_rows(rows, x_ref, mod_ref, ya_ref, yh_ref, yw_ref, gt_ref, wb_ref, wo_ref, g2_ref, rw_ref, rb_ref,
                xo_ref, tok_ref, route_ref, rt_ref):
    gt = gt_ref[0, rows, :]
    m = gt[:, 0:D_MODEL].astype(F32) * _dot(ya_ref[0, rows, :], wb_ref[0])
    m = m + gt[:, D_MODEL:2 * D_MODEL].astype(F32) * _dot(yh_ref[0, rows, :], wb_ref[1])
    m = m + gt[:, 2 * D_MODEL:3 * D_MODEL].astype(F32) * _dot(yw_ref[0, rows, :], wb_ref[2])
    x = x_ref[0, rows, :] + mod_ref[0, 2:3, :] * _dot(m.astype(BF16), wo_ref[...])
    xo_ref[0, rows, :] = x
    ms = jnp.mean(x * x, axis=-1, keepdims=True)
    tok = x * lax.rsqrt(ms + EPS) * g2_ref[...]
    tok = tok * (1.0 + mod_ref[0, 4:5, :]) + mod_ref[0, 3:4, :]
    tok_ref[0, rows, :] = _pack_bf16_pairs(tok)

    tok_hi = tok.astype(BF16)
    tok_lo = (tok - tok_hi.astype(F32)).astype(BF16)
    lg2 = _dot(tok_hi, rw_ref[...]) + _dot(tok_lo, rw_ref[...])
    lg = lg2[:, :ROUTER_W] + lg2[:, ROUTER_W:] + rb_ref[...]
    lane = lax.broadcasted_iota(jnp.int32, lg.shape, 1).astype(F32)
    big = float(ROUTER_W)

    def first_lane(mask):
        return jnp.min(jnp.where(mask, lane, big), axis=-1, keepdims=True)

    is_g = lane < N_GROUPS
    gmax = jnp.max(jnp.where(is_g, lg, NEG_INF), axis=-1, keepdims=True)
    g_sel = first_lane(jnp.where(is_g, lg, NEG_INF) == gmax)
    p_g = 1.0 / jnp.sum(jnp.where(is_g, jnp.exp(lg - gmax), 0.0), axis=-1, keepdims=True)
    lo = N_GROUPS + g_sel * EXPERTS_PER_GROUP
    in_grp = (lane >= lo) & (lane < lo + EXPERTS_PER_GROUP)
    le = jnp.where(in_grp, lg, NEG_INF)
    v1 = jnp.max(le, axis=-1, keepdims=True)
    i1 = first_lane(le == v1)
    le = jnp.where(lane == i1, NEG_INF, le)
    v2 = jnp.max(le, axis=-1, keepdims=True)
    i2 = first_lane(le == v2)
    e2 = jnp.exp(v2 - v1)
    p1 = p_g / (1.0 + e2)
    p2 = p_g * e2 / (1.0 + e2)
    route = jnp.where(lane == 0, i1 - N_GROUPS, jnp.where(lane == 1, i2 - N_GROUPS,
                      jnp.where(lane == 2, p1, jnp.where(lane == 3, p2, 0.0))))
    route_ref[0, rows, :] = route
    rt_ref[0, :, rows] = route.T[0:ROUTE_T_ROWS, :]


def _merge(x, mod, mod_row, ya, yh, yw, gates, w_branch, w_out, g2, rw, rb):
    bsz, s, d = x.shape
    tm = min(512, s)
    row = lambda b, t: (b, t, 0)
    return pl.pallas_call(
        _merge_kernel,
        out_shape=[jax.ShapeDtypeStruct((bsz, s, d), F32), jax.ShapeDtypeStruct((bsz, s, d // 2), jnp.uint32),
                   jax.ShapeDtypeStruct((bsz, s, ROUTER_W), F32), jax.ShapeDtypeStruct((bsz, ROUTE_T_ROWS, s), F32)],
        grid=(bsz, s // tm),
        in_specs=[
            pl.BlockSpec((1, tm, d), row),
            pl.BlockSpec((1, N_MOD, d), lambda b, t: (mod_row(b), 0, 0)),
            pl.BlockSpec((1, tm, QA_W), row), pl.BlockSpec((1, tm, HY_WIDTH), row), pl.BlockSpec((1, tm, QW_W), row),
            pl.BlockSpec((1, tm, N_BRANCH * d), row),
            _const_spec((N_BRANCH, HY_WIDTH, d)), _const_spec((d, d)), _const_spec((1, d)),
            _const_spec((d, 2 * ROUTER_W)), _const_spec((1, ROUTER_W)),
        ],
        out_specs=[pl.BlockSpec((1, tm, d), row), pl.BlockSpec((1, tm, d // 2), row),
                   pl.BlockSpec((1, tm, ROUTER_W), row), pl.BlockSpec((1, ROUTE_T_ROWS, tm), lambda b, t: (b, 0, t))],
        compiler_params=_cparams(("parallel", "parallel")),
        name="merge_router",
    )(x, mod, ya, yh, yw, gates, w_branch, w_out, g2, rw, rb)


def _route_plan(flat, tile):
    n_tiles = -(-flat.shape[0] // tile) + N_EXPERTS
    onehot = (flat[:, None] == jnp.arange(N_EXPERTS, dtype=jnp.int32)[None, :]).astype(jnp.int32)
    csum = jnp.cumsum(onehot, axis=0)
    count = csum[-1]
    rank = jnp.sum((csum - onehot) * onehot, axis=1)
    size = ((count + tile - 1) // tile) * tile
    end = jnp.cumsum(size)
    start = end - size
    slot = jnp.sum(onehot * start[None, :], axis=1) + rank
    tile_row = jnp.arange(n_tiles, dtype=jnp.int32) * tile
    tile_e = jnp.minimum(jnp.sum((tile_row[:, None] >= end[None, :]).astype(jnp.int32), axis=1), N_EXPERTS - 1)
    tile_oh = (tile_e[:, None] == jnp.arange(N_EXPERTS, dtype=jnp.int32)[None, :]).astype(jnp.int32)
    filled = jnp.sum(tile_oh * (start + count)[None, :], axis=1)
    tile_rows = jnp.clip(filled - tile_row, 0, tile)
    return slot, tile_e, tile_rows, n_tiles


def _sc_mesh():
    info = plsc.get_sparse_core_info()
    mesh = plsc.VectorSubcoreMesh(core_axis_name="c", subcore_axis_name="s")
    return mesh, info.num_cores, info.num_cores * info.num_subcores


def _sc_scatter_rows(sources, slot, n_out):
    mesh, n_cores, n_workers = _sc_mesh()
    n = sum(src.shape[0] for src in sources)
    w = sources[0].shape[1]
    dtype = sources[0].dtype
    n_k = slot.shape[0] // n
    per_worker = n // n_workers
    n_chunks = per_worker // SC_CHUNK
    assert per_worker * n_workers == n and n_chunks * SC_CHUNK == per_worker
    assert all(src.shape[0] % SC_CHUNK == 0 for src in sources)
    n_src = len(sources)

    @functools.partial(
        pl.kernel, mesh=mesh,
        out_type=jax.ShapeDtypeStruct((n_out, w), dtype),
        scratch_types=[pltpu.VMEM((SC_CHUNK,), jnp.int32), pltpu.VMEM((SC_CHUNK, w), dtype), pltpu.SemaphoreType.DMA],
    )
    def scatter(*refs):
        src_hbm = refs[:n_src]
        slot_hbm, out_hbm, idx_v, rows_v, sem = refs[n_src:]
        base = (lax.axis_index("s") * n_cores + lax.axis_index("c")) * per_worker

        @pl.loop(0, n_chunks)
        def _(j):
            off = base + j * SC_CHUNK
            lo = 0
            for src in src_hbm:
                hi = lo + src.shape[0]

                @pl.when((off >= lo) & (off < hi))
                def _(src=src, lo=lo):
                    pltpu.sync_copy(src.at[pl.ds(off - lo, SC_CHUNK)], rows_v)

                lo = hi
            for q in range(n_k):
                pltpu.sync_copy(slot_hbm.at[pl.ds(q * n + off, SC_CHUNK)], idx_v)
                pltpu.async_copy(rows_v, out_hbm.at[idx_v], sem).wait()

    return scatter(*sources, slot)


def _sc_gather_rows(table, idx):
    mesh, n_cores, n_workers = _sc_mesh()
    m = idx.shape[0]
    w = table.shape[1]
    per_worker = m // n_workers
    n_chunks = per_worker // SC_CHUNK
    assert per_worker * n_workers == m and n_chunks * SC_CHUNK == per_worker

    @functools.partial(
        pl.kernel, mesh=mesh,
        out_type=jax.ShapeDtypeStruct((m, w), table.dtype),
        scratch_types=[pltpu.VMEM((SC_CHUNK,), jnp.int32), pltpu.VMEM((SC_CHUNK, w), table.dtype),
                       pltpu.SemaphoreType.DMA],
    )
    def gather(table_hbm, idx_hbm, out_hbm, idx_v, rows_v, sem):
        base = (lax.axis_index("s") * n_cores + lax.axis_index("c")) * per_worker

        @pl.loop(0, n_chunks)
        def _(j):
            off = base + j * SC_CHUNK
            pltpu.sync_copy(idx_hbm.at[pl.ds(off, SC_CHUNK)], idx_v)
            pltpu.async_copy(table_hbm.at[idx_v], rows_v, sem).wait()
            pltpu.sync_copy(rows_v, out_hbm.at[pl.ds(off, SC_CHUNK)])

    return gather(table, idx)


def _experts_kernel(te_ref, tr_ref, x_ref, w1_ref, w3_ref, w2_ref, o_ref, w1_b, w3_b, w2_b):
    t = pl.program_id(0)
    n_rows = tr_ref[t]

    @pl.when((t == 0) | (te_ref[t] != te_ref[jnp.maximum(t - 1, 0)]))
    def _():
        w1_b[...] = w1_ref[0, 0].astype(BF16)
        w3_b[...] = w3_ref[0, 0].astype(BF16)
        w2_b[...] = w2_ref[0, 0].astype(BF16)

    @pl.when(n_rows > 0)
    def _():
        xp = x_ref[...]
        row = lax.broadcasted_iota(jnp.int32, xp.shape, 0)
        x = _unpack_bf16_pairs(jnp.where(row < n_rows, xp, jnp.uint32(0))).astype(BF16)
        h1 = _dot(x, w1_b[...])
        h = (h1 * jax.nn.sigmoid(h1)) * _dot(x, w3_b[...])
        o_ref[...] = _pack_bf16_pairs(_dot(h.astype(BF16), w2_b[...]))

    @pl.when(n_rows == 0)
    def _():
        o_ref[...] = jnp.zeros_like(o_ref)


def _experts(x_sorted, tile_e, tile_rows, tile, layer, w1, w3, w2):
    n_slots, wp = x_sorted.shape
    d = 2 * wp
    wspec = lambda shape: pl.BlockSpec(shape, lambda t, te, tr: (layer, te[t], 0, 0))
    return pl.pallas_call(
        _experts_kernel,
        out_shape=jax.ShapeDtypeStruct((n_slots, wp), jnp.uint32),
        grid_spec=pltpu.PrefetchScalarGridSpec(
            num_scalar_prefetch=2,
            grid=(n_slots // tile,),
            in_specs=[pl.BlockSpec((tile, wp), lambda t, te, tr: (t, 0)),
                      wspec((1, 1, d, D_EXPERT)), wspec((1, 1, d, D_EXPERT)), wspec((1, 1, D_EXPERT, d))],
            out_specs=pl.BlockSpec((tile, wp), lambda t, te, tr: (t, 0)),
            scratch_shapes=[pltpu.VMEM((d, D_EXPERT), BF16), pltpu.VMEM((d, D_EXPERT), BF16),
                            pltpu.VMEM((D_EXPERT, d), BF16)],
        ),
        compiler_params=_cparams(("arbitrary",)),
        name="moe_experts",
    )(tile_e, tile_rows, x_sorted, w1, w3, w2)


def _combine_kernel(x_ref, mod_ref, ya_ref, yb_ref, route_ref, o_ref):
    route = route_ref[0]
    y = route[:, 2:3] * _unpack_bf16_pairs(ya_ref[...]) + route[:, 3:4] * _unpack_bf16_pairs(yb_ref[...])
    o_ref[0] = x_ref[0] + mod_ref[0, 5:6, :] * y


def _combine(x, mod, mod_row, y_tok, route, row_off, n_all):
    bsz, s, d = x.shape
    tm = min(512, s)
    nt = s // tm
    assert row_off % tm == 0 and n_all % tm == 0
    off0 = row_off // tm
    off1 = (n_all + row_off) // tm
    row = lambda b, t: (b, t, 0)
    return pl.pallas_call(
        _combine_kernel,
        out_shape=jax.ShapeDtypeStruct((bsz, s, d), F32),
        grid=(bsz, nt),
        in_specs=[
            pl.BlockSpec((1, tm, d), row),
            pl.BlockSpec((1, N_MOD, d), lambda b, t: (mod_row(b), 0, 0)),
            pl.BlockSpec((tm, d // 2), lambda b, t: (off0 + b * nt + t, 0)),
            pl.BlockSpec((tm, d // 2), lambda b, t: (off1 + b * nt + t, 0)),
            pl.BlockSpec((1, tm, ROUTER_W), row),
        ],
        out_specs=pl.BlockSpec((1, tm, d), row),
        compiler_params=_cparams(("parallel", "parallel")),
        name="moe_combine",
    )(x, mod, y_tok, y_tok, route)


def _moe(streams, layer, w1, w3, w2):
    toks = [st[2].reshape(-1, st[2].shape[-1]) for st in streams]
    n_all = sum(t.shape[0] for t in toks)
    tile = MOE_TM if 2 * n_all >= N_EXPERTS * 4 * MOE_TM else MOE_TM_SMALL
    flat = jnp.concatenate([st[4][:, k, :].reshape(-1) for k in range(2) for st in streams]).astype(jnp.int32)
    slot, tile_e, tile_rows, n_tiles = _route_plan(flat, tile)
    x_sorted = _sc_scatter_rows(toks, slot, n_tiles * tile)
    y_sorted = _experts(x_sorted, tile_e, tile_rows, tile, layer, w1, w3, w2)
    return _sc_gather_rows(y_sorted, slot), n_all


def _rope_tables(n_tok):
    rows = n_tok // GRID_W
    row_id = jnp.repeat(jnp.arange(rows), GRID_W)
    col_id = jnp.tile(jnp.arange(GRID_W), rows)
    inv_freq = ROPE_THETA ** (-jnp.arange(ROPE_FREQS, dtype=F32) / ROPE_FREQS)
    ang_r = row_id[:, None] * inv_freq
    ang_c = col_id[:, None] * inv_freq
    cos_h = jnp.concatenate([jnp.cos(ang_r), jnp.cos(ang_r), jnp.cos(ang_c), jnp.cos(ang_c)], axis=-1)
    sin_h = jnp.concatenate([-jnp.sin(ang_r), jnp.sin(ang_r), -jnp.sin(ang_c), jnp.sin(ang_c)], axis=-1)
    reps = KA_W // HEAD_DIM
    return jnp.tile(cos_h, (1, reps)), jnp.tile(sin_h, (1, reps))


def kernel(x, c, ctx, c_ctx, ada_w, ada_b, norm1_g, norm2_g, w_in, qk_g, hy_short_w, hy_short_b, hy_f_w1, hy_f_b1,
           hy_f_freq, hy_f_w2, hy_f_b2, hy_f_w3, hy_bias, sinks, w_branch, w_out, router_g_w, router_g_b,
           router_e_w, router_e_b, exp_w1, exp_w3, exp_w2):
    bsz, n_tok, d = x.shape
    n_ctx = ctx.shape[1]
    depth = ada_w.shape[0]
    assert d == D_MODEL and bsz < MOD_ROWS and n_tok % 256 == 0 and n_ctx % 256 == 0

    cc = jnp.zeros((MOD_ROWS, d), F32).at[:bsz].set(c).at[bsz].set(c_ctx)
    mods = _ada_table(cc, ada_w, ada_b)
    ctx_row = lambda b: bsz

    rope_tabs = _rope_tables(n_tok)
    head_ones = (jnp.arange(QA_W)[:, None] // HEAD_DIM == jnp.arange(QA_W)[None, :] // HEAD_DIM).astype(BF16)
    tabs_lat = _dft_tables(n_tok)
    tabs_ctx = _dft_tables(n_ctx)
    bands = jnp.linspace(1e-4, HY_BANDS - 1, HY_BANDS, dtype=F32)
    band_row = jnp.zeros((1, LANES), F32).at[0, 1:1 + HY_BANDS].set(bands).at[0, 1 + HY_BANDS:1 + 2 * HY_BANDS].set(bands)
    band_row = 2 * math.pi * band_row
    deltas = jnp.abs(jnp.linspace(math.log(HY_TARGET) / HY_SLOW_DECAY, math.log(HY_TARGET) / HY_FAST_DECAY,
                                  HY_WIDTH, dtype=F32))[None, :]

    lat_row = lambda b: b
    xc = ctx
    pend_lat = pend_ctx = None
    for l in range(depth):
        update_ctx = l < depth - 1
        mod = mods[l]
        w_in_b = w_in[l].astype(BF16)
        gq = [jnp.tile(qk_g[l, 0], A_HEADS)[None], jnp.tile(qk_g[l, 1], A_KV)[None],
              jnp.tile(qk_g[l, 2], W_HEADS)[None], jnp.tile(qk_g[l, 3], W_KV)[None]]
        g1 = norm1_g[l][None]

        w1p = jnp.zeros((LANES, HY_HIDDEN), F32).at[:hy_f_w1.shape[1]].set(hy_f_w1[l])
        fargs = (band_row, w1p, hy_f_b1[l][None], hy_f_freq[l][None], hy_f_w2[l], hy_f_b2[l][None], hy_f_w3[l], deltas)
        skip = hy_bias[l][None]
        sb = hy_short_b[l][None]

        def hyena_filter(n, tabs):
            fsum, fdiff, nrm = _hyena_filter(n, *fargs)
            spec = _filter_spectrum(tabs[0], jnp.concatenate([fsum, fdiff], axis=1))
            return spec, spec[n:n + 1, :HY_WIDTH], nrm

        filt_lat = hyena_filter(n_tok, tabs_lat)
        filt_ctx = hyena_filter(n_ctx, tabs_ctx) if update_ctx else None

        wb = w_branch[l].astype(BF16)
        wo = w_out[l].astype(BF16)
        g2 = norm2_g[l][None]
        rw = jnp.zeros((d, ROUTER_W), F32).at[:, :N_GROUPS].set(router_g_w[l]).at[:, N_GROUPS:N_GROUPS + N_EXPERTS].set(router_e_w[l])
        rb = jnp.zeros((1, ROUTER_W), F32).at[0, :N_GROUPS].set(router_g_b[l]).at[0, N_GROUPS:N_GROUPS + N_EXPERTS].set(router_e_b[l])
        rw_hi = rw.astype(BF16)
        rw = jnp.concatenate([rw_hi, (rw - rw_hi.astype(F32)).astype(BF16)], axis=1)

        proj = _in_proj(x, mod, lat_row, g1, w_in_b, gq, head_ones, rope_tabs, pend_lat)
        proj_c = _in_proj(xc, mod, ctx_row, g1, w_in_b, gq, head_ones, None, pend_ctx, kv_only=not update_ctx)
        if pend_lat is not None:
            x = proj[-1]
        if pend_ctx is not None:
            xc = proj_c[-1]
        qa, ka, va, uh, qw, kw, vw, gts = proj[:8]
        if update_ctx:
            qa_c, ka_c, va_c, uh_c, qw_c, kw_c, vw_c, gts_c = proj_c[:8]
        else:
            ka_c, va_c, kw_c, vw_c = proj_c[:4]

        ya = _attention(qa, ka_c, va_c, mode="global", kl=ka, vl=va, tq=1024)
        yw = _attention(qw, kw_c, vw_c, mode="window", kl=kw, vl=vw, sinks=sinks[l], tq=512)
        yh = _hyena(uh, tabs_lat, *filt_lat, hy_short_w[l], sb, skip)
        x, tok, route, route_t = _merge(x, mod, lat_row, ya, yh, yw, gts, wb, wo, g2, rw, rb)
        y_tok, n_all = _moe([(x, lat_row, tok, route, route_t)], l, exp_w1, exp_w3, exp_w2)
        pend_lat = (mod, y_tok, route, 0, n_all)
        pend_ctx = None
        if update_ctx:
            ya_c = _attention(qa_c, ka_c, va_c, mode="ctx", tq=256)
            yw_c = _attention(qw_c, kw_c, vw_c, mode="ctx_sink", sinks=sinks[l], tq=256)
            yh_c = _hyena(uh_c, tabs_ctx, *filt_ctx, hy_short_w[l], sb, skip)
            xc, tok_c, route_c, route_ct = _merge(xc, mod, ctx_row, ya_c, yh_c, yw_c, gts_c, wb, wo, g2, rw, rb)
            y_tok_c, n_all_c = _moe([(xc, ctx_row, tok_c, route_c, route_ct)], l, exp_w1, exp_w3, exp_w2)
            pend_ctx = (mod, y_tok_c, route_c, 0, n_all_c)
    return _combine(x, *pend_lat[:1], lat_row, *pend_lat[1:])
```
